```python
import math
import jax
import jax.numpy as jnp
from jax import lax
import numpy as np

D_MODEL = 1024
BATCH = 2
SEQ = 8192
DEPTH = 2

CHUNK = 64
D_MIX = 1024

RWKV_HEADS = 6
RWKV_HEAD_DIM = 64
RWKV_WIDTH = 384
DECAY_LORA = 64
ICLR_LORA = 64
GATE_LORA = 128
RWKV_IN = 1408
RWKV_GN_EPS = 64e-5

S5_GROUPS = 16
S5_GROUP_CH = 16
S5_WIDTH = 256
S5_STATE = 64
S5_MIN_STEP = 1e-3
S5_MAX_STEP = 1e-1

ATTN_HEADS = 6
ATTN_HEAD_DIM = 64
ATTN_WIDTH = 384
LEFT_CHUNKS = 8
BAND_CHUNKS = 9
BAND = 576
MAX_REL = 128

N_IN = 2816
D_FF = 2816
N_EXPERTS = 8
TOP_K = 2
N_DENSE_LAYERS = (DEPTH + 1) // 2
N_MOE_LAYERS = DEPTH // 2
RMS_EPS = 1e-6

kernel_name = 'hybrid_chunk_streaming_block'


def rms_norm(x, gain):
    xf = x.astype(jnp.float32)
    y = xf * lax.rsqrt(jnp.mean(xf * xf, axis=-1, keepdims=True) + RMS_EPS)
    return (y * gain.astype(jnp.float32)).astype(x.dtype)


def token_shift(t):
    return jnp.pad(t, ((0, 0), (1, 0), (0, 0)))[:, :-1]


def swiglu(h, w_gate, w_up, w_down):
    return (jax.nn.silu(h @ w_gate) * (h @ w_up)) @ w_down


def rwkv7_mix(p, mu, w0, w_up, a0, a_up, g_up, k_k, k_a, r_k, ln_w, ln_b):
    B, S, _ = p.shape
    H, N = RWKV_HEADS, RWKV_HEAD_DIM
    p = p + (token_shift(p) - p) * mu
    r, k, v, w_lo, a_lo, g_lo = jnp.split(
        p, [RWKV_WIDTH, 2 * RWKV_WIDTH, 3 * RWKV_WIDTH,
            3 * RWKV_WIDTH + DECAY_LORA, 3 * RWKV_WIDTH + DECAY_LORA + ICLR_LORA], axis=-1)
    w = -jax.nn.softplus(-(w0 + jnp.tanh(w_lo) @ w_up)) - 0.5
    decay = jnp.exp(-jnp.exp(w.astype(jnp.float32)))
    a = jax.nn.sigmoid(a0 + a_lo @ a_up)
    g = jax.nn.sigmoid(g_lo) @ g_up
    kk = (k * k_k).reshape(B, S, H, N).astype(jnp.float32)
    kk = kk / jnp.maximum(jnp.sqrt(jnp.sum(kk * kk, axis=-1, keepdims=True)), 1e-12)
    k = k * (1.0 + (a - 1.0) * k_a)
    rh = r.reshape(B, S, H, N)
    kh = k.reshape(B, S, H, N)
    vh = v.reshape(B, S, H, N)

    def seq_first(t):
        return jnp.moveaxis(t.reshape(B, S, H, N).astype(jnp.float32), 1, 0)

    def step(state, inp):
        r_t, w_t, k_t, v_t, kk_t, a_t = inp
        sa = jnp.einsum('bhvk,bhk->bhv', state, -kk_t)
        state = (state * w_t[:, :, None, :] + sa[..., None] * (kk_t * a_t)[:, :, None, :]
                 + v_t[..., None] * k_t[:, :, None, :])
        return state, jnp.einsum('bhvk,bhk->bhv', state, r_t)

    state0 = jnp.zeros((B, H, N, N), jnp.float32)
    xs = (seq_first(rh), seq_first(decay), seq_first(kh), seq_first(vh), jnp.moveaxis(kk, 1, 0), seq_first(a))
    _, o = lax.scan(step, state0, xs)
    o = jnp.moveaxis(o, 0, 1)
    mean = jnp.mean(o, axis=-1, keepdims=True)
    var = jnp.mean(jnp.square(o - mean), axis=-1, keepdims=True)
    o = ((o - mean) * lax.rsqrt(var + RWKV_GN_EPS)).reshape(B, S, RWKV_WIDTH)
    o = o * ln_w.astype(jnp.float32) + ln_b.astype(jnp.float32)
    bonus = jnp.sum(rh * kh * r_k, axis=-1, keepdims=True) * vh
    o = o + bonus.reshape(B, S, RWKV_WIDTH).astype(jnp.float32)
    return (o * g.astype(jnp.float32)).astype(p.dtype)


def s5_mix(u, a_re, a_im, log_step, b_re, b_im, c_re, c_im, d, w_glu, b_glu, gain):
    B, S, _ = u.shape
    ug = u.astype(jnp.float32).reshape(B, S, S5_GROUPS, S5_GROUP_CH)
    lam = lax.complex(jnp.minimum(a_re.astype(jnp.float32), -1e-4), a_im.astype(jnp.float32))
    step = jnp.exp(log_step.astype(jnp.float32))[:, None]
    lam_bar = jnp.exp(lam * step)
    b = lax.complex(b_re.astype(jnp.float32), b_im.astype(jnp.float32))
    b_bar = ((lam_bar - 1.0) / lam)[..., None] * b
    bu = jnp.einsum('gpc,bsgc->bsgp', b_bar, ug.astype(jnp.complex64))
    lam_seq = jnp.broadcast_to(lam_bar, bu.shape)

    def combine(e1, e2):
        a1, x1 = e1
        a2, x2 = e2
        return a2 * a1, a2 * x1 + x2

    _, states = lax.associative_scan(combine, (lam_seq, bu), axis=1)
    c = lax.complex(c_re.astype(jnp.float32), c_im.astype(jnp.float32))
    y = jnp.real(jnp.einsum('gcp,bsgp->bsgc', c, states)) + d.astype(jnp.float32) * ug
    y = jax.nn.gelu(y.reshape(B, S, S5_WIDTH))
    y = y * jax.nn.sigmoid(y @ w_glu.astype(jnp.float32) + b_glu.astype(jnp.float32))
    return rms_norm(y.astype(u.dtype), gain)


def chunk_attention(q, k, v, rel_bias, gain):
    B, S, _ = q.shape
    NC = S // CHUNK
    H, Dh = ATTN_HEADS, ATTN_HEAD_DIM

    def blocks(t):
        return t.reshape(B, NC, CHUNK, H, Dh)

    band_idx = jnp.arange(NC)[:, None] + jnp.arange(BAND_CHUNKS)[None, :]

    def band(t):
        tp = jnp.pad(blocks(t), ((0, 0), (LEFT_CHUNKS, 0), (0, 0), (0, 0), (0, 0)))
        return tp[:, band_idx].reshape(B, NC, BAND, H, Dh)

    qb = blocks(q)
    kb, vb = band(k), band(v)
    s = jnp.einsum('bcqhd,bckhd->bchqk', qb, kb).astype(jnp.float32) * (Dh ** -0.5)
    qpos = LEFT_CHUNKS * CHUNK + np.arange(CHUNK)
    kpos = np.arange(BAND)
    rel_idx = np.clip(qpos[:, None] - kpos[None, :], -MAX_REL, MAX_REL) + MAX_REL
    s = s + rel_bias.astype(jnp.float32)[:, rel_idx]
    key_abs = (jnp.arange(NC)[:, None] - LEFT_CHUNKS) * CHUNK + jnp.arange(BAND)[None, :]
    valid = key_abs >= 0
    s = jnp.where(valid[None, :, None, None, :], s, jnp.finfo(jnp.float32).min)
    prob = jax.nn.softmax(s, axis=-1).astype(v.dtype)
    o = jnp.einsum('bchqk,bckhd->bcqhd', prob, vb).reshape(B, S, ATTN_WIDTH)
    return rms_norm(o, gain)


def moe_swiglu(h, router, w_gate, w_up, w_down):
    B, S, D = h.shape
    t = h.reshape(B * S, D)
    logits = (t @ router).astype(jnp.float32)
    top_val, top_idx = lax.top_k(logits, TOP_K)
    gates = jax.nn.softmax(top_val, axis=-1)
    combine = jnp.sum(jax.nn.one_hot(top_idx, N_EXPERTS, dtype=jnp.float32) * gates[..., None], axis=1)
    out = jnp.zeros_like(t)
    for e in range(N_EXPERTS):
        out = out + combine[:, e:e + 1].astype(t.dtype) * swiglu(t, w_gate[e], w_up[e], w_down[e])
    return out.reshape(B, S, D)


def setup_inputs(seed: int = 0) -> dict:
    key = jax.random.key(seed)
    ks = iter(jax.random.split(key, 48))
    L, Ld, Lm = DEPTH, N_DENSE_LAYERS, N_MOE_LAYERS
    G, Cg, P = S5_GROUPS, S5_GROUP_CH, S5_STATE

    def nrm(shape, scale):
        return scale * jax.random.normal(next(ks), shape, jnp.float32)

    def gain(shape):
        return 1.0 + 0.02 * jax.random.normal(next(ks), shape, jnp.float32)

    decay_base = jnp.linspace(-6.0, -1.0, RWKV_WIDTH, dtype=jnp.float32) + 0.5
    n = jnp.arange(P, dtype=jnp.float32)
    return {
        'x': nrm((BATCH, SEQ, D_MODEL), 1.0),
        'norm_mix_pre': gain((L, D_MODEL)),
        'norm_mix_post': gain((L, D_MODEL)),
        'norm_ffn_pre': gain((L, D_MODEL)),
        'norm_ffn_post': gain((L, D_MODEL)),
        'w_in': nrm((L, D_MODEL, N_IN), D_MODEL ** -0.5),
        'w_out': nrm((L, D_MIX, D_MODEL), D_MIX ** -0.5),
        'rwkv_mu': jax.random.uniform(next(ks), (L, RWKV_IN), jnp.float32),
        'rwkv_w0': decay_base + nrm((L, RWKV_WIDTH), 0.1),
        'rwkv_w_up': nrm((L, DECAY_LORA, RWKV_WIDTH), 0.05),
        'rwkv_a0': nrm((L, RWKV_WIDTH), 0.1),
        'rwkv_a_up': nrm((L, ICLR_LORA, RWKV_WIDTH), 0.5 * ICLR_LORA ** -0.5),
        'rwkv_g_up': nrm((L, GATE_LORA, RWKV_WIDTH), GATE_LORA ** -0.5),
        'rwkv_k_k': 0.85 + nrm((L, RWKV_WIDTH), 0.02),
        'rwkv_k_a': 1.0 + nrm((L, RWKV_WIDTH), 0.02),
        'rwkv_r_k': nrm((L, RWKV_HEADS, RWKV_HEAD_DIM), 0.1),
        'rwkv_ln_w': gain((L, RWKV_WIDTH)),
        'rwkv_ln_b': nrm((L, RWKV_WIDTH), 0.01),
        's5_a_re': -0.5 + nrm((L, G, P), 0.01),
        's5_a_im': jnp.broadcast_to(math.pi * n, (L, G, P)),
        's5_log_step': jax.random.uniform(next(ks), (L, G), jnp.float32,
                                          minval=math.log(S5_MIN_STEP), maxval=math.log(S5_MAX_STEP)),
        's5_b_re': nrm((L, G, P, Cg), (2 * Cg) ** -0.5),
        's5_b_im': nrm((L, G, P, Cg), (2 * Cg) ** -0.5),
        's5_c_re': nrm((L, G, Cg, P), (2 * P) ** -0.5),
        's5_c_im': nrm((L, G, Cg, P), (2 * P) ** -0.5),
        's5_d': nrm((L, G, Cg), 1.0),
        's5_w_glu': nrm((L, S5_WIDTH, S5_WIDTH), S5_WIDTH ** -0.5),
        's5_b_glu': nrm((L, S5_WIDTH), 0.01),
        's5_norm': gain((L, S5_WIDTH)),
        'attn_rel_bias': nrm((L, ATTN_HEADS, 2 * MAX_REL + 1), 0.2),
        'attn_norm': gain((L, ATTN_WIDTH)),
        'ffn_w_gate': nrm((Ld, D_MODEL, D_FF), D_MODEL ** -0.5),
        'ffn_w_up': nrm((Ld, D_MODEL, D_FF), D_MODEL ** -0.5),
        'ffn_w_down': nrm((Ld, D_FF, D_MODEL), D_FF ** -0.5),
        'moe_router': nrm((Lm, D_MODEL, N_EXPERTS), D_MODEL ** -0.5),
        'moe_w_gate': nrm((Lm, N_EXPERTS, D_MODEL, D_FF), D_MODEL ** -0.5),
        'moe_w_up': nrm((Lm, N_EXPERTS, D_MODEL, D_FF), D_MODEL ** -0.5),
        'moe_w_down': nrm((Lm, N_EXPERTS, D_FF, D_MODEL), D_FF ** -0.5),
    }


def reference(x, norm_mix_pre, norm_mix_post, norm_ffn_pre, norm_ffn_post, w_in, w_out,
              rwkv_mu, rwkv_w0, rwkv_w_up, rwkv_a0, rwkv_a_up, rwkv_g_up, rwkv_k_k, rwkv_k_a, rwkv_r_k,
              rwkv_ln_w, rwkv_ln_b,
              s5_a_re, s5_a_im, s5_log_step, s5_b_re, s5_b_im, s5_c_re, s5_c_im, s5_d, s5_w_glu, s5_b_glu,
              s5_norm, attn_rel_bias, attn_norm,
              ffn_w_gate, ffn_w_up, ffn_w_down, moe_router, moe_w_gate, moe_w_up, moe_w_down):
    split_at = [RWKV_IN, RWKV_IN + S5_WIDTH, RWKV_IN + S5_WIDTH + ATTN_WIDTH,
                RWKV_IN + S5_WIDTH + 2 * ATTN_WIDTH]
    for layer in range(DEPTH):
        h = rms_norm(x, norm_mix_pre[layer])
        proj = h @ w_in[layer]
        p_rwkv, u_s5, q, k, v = jnp.split(proj, split_at, axis=-1)
        o_rwkv = rwkv7_mix(p_rwkv, rwkv_mu[layer], rwkv_w0[layer], rwkv_w_up[layer], rwkv_a0[layer],
                           rwkv_a_up[layer], rwkv_g_up[layer], rwkv_k_k[layer], rwkv_k_a[layer],
                           rwkv_r_k[layer], rwkv_ln_w[layer], rwkv_ln_b[layer])
        o_s5 = s5_mix(u_s5, s5_a_re[layer], s5_a_im[layer], s5_log_step[layer], s5_b_re[layer],
                      s5_b_im[layer], s5_c_re[layer], s5_c_im[layer], s5_d[layer], s5_w_glu[layer],
                      s5_b_glu[layer], s5_norm[layer])
        o_attn = chunk_attention(q, k, v, attn_rel_bias[layer], attn_norm[layer])
        mixed = jnp.concatenate([o_rwkv, o_s5, o_attn], axis=-1)
        x = x + rms_norm(mixed @ w_out[layer], norm_mix_post[layer])
        h = rms_norm(x, norm_ffn_pre[layer])
        if layer % 2 == 0:
            i = layer // 2
            f = swiglu(h, ffn_w_gate[i], ffn_w_up[i], ffn_w_down[i])
        else:
            i = layer // 2
            f = moe_swiglu(h, moe_router[i], moe_w_gate[i], moe_w_up[i], moe_w_down[i])
        x = x + rms_norm(f, norm_ffn_post[layer])
    return x
```

```python
import functools
import math

import jax
import jax.numpy as jnp
import numpy as np
from jax import lax
from jax.experimental import pallas as pl
from jax.experimental.pallas import tpu as pltpu

F32 = jnp.float32
BF16 = jnp.bfloat16

RMS_EPS = 1e-6
RWKV_GN_EPS = 64e-5
HEAD_DIM = 64
RWKV_HEADS = 6
RWKV_WIDTH = 384
RWKV_IN = 1408
S5_WIDTH = 256
S5_GROUPS = 16
S5_STATE = 64
S5_LANES = S5_GROUPS * S5_STATE
ATTN_HEADS = 6
ATTN_WIDTH = 384
ATTN_CHUNK = 64
ATTN_LEFT = 8
MAX_REL = 128
N_EXPERTS = 8
LANES = 128

RWKV_CHUNK = 64
S5_CHUNK = 128
ATTN_TILE = 2 * ATTN_CHUNK
ATTN_KEY_BLOCKS = (ATTN_LEFT * ATTN_CHUNK) // ATTN_TILE + 1
ROW_TILE = 512
FFN_TILE = 1408
VMEM_LIMIT = 48 * 1024 * 1024

_HI = lax.Precision.HIGHEST


def _hdot(a, b):
    return jnp.dot(a, b, precision=_HI, preferred_element_type=F32)


def _hdot_nt(a, b):
    return lax.dot_general(a, b, (((1,), (1,)), ((), ())), precision=_HI, preferred_element_type=F32)


def _hdot_tn(a, b):
    return lax.dot_general(a, b, (((0,), (0,)), ((), ())), precision=_HI, preferred_element_type=F32)


def _rms(x, gain):
    return x * lax.rsqrt(jnp.mean(x * x, axis=-1, keepdims=True) + RMS_EPS) * gain


def _params(*semantics):
    return pltpu.CompilerParams(dimension_semantics=semantics, vmem_limit_bytes=VMEM_LIMIT)


def _const_spec(shape):
    zeros = (0,) * len(shape)
    return pl.BlockSpec(shape, lambda *_: zeros)


def _inproj_kernel(x_ref, gain_ref, w_ref, p_ref, u_ref, qkv_ref):
    h = _rms(x_ref[...], gain_ref[...]).astype(BF16)
    s5_at, qkv_at = RWKV_IN, RWKV_IN + S5_WIDTH
    p_ref[...] = jnp.dot(h, w_ref[:, :s5_at], preferred_element_type=F32)
    u_ref[...] = jnp.dot(h, w_ref[:, s5_at:qkv_at], preferred_element_type=F32)
    qkv_ref[...] = jnp.dot(h, w_ref[:, qkv_at:], preferred_element_type=F32).astype(BF16)


def _in_projection(x, gain, w_in):
    t, d = x.shape
    n_in = w_in.shape[1]
    n_qkv = n_in - RWKV_IN - S5_WIDTH
    row = lambda n: pl.BlockSpec((ROW_TILE, n), lambda i: (i, 0))
    return pl.pallas_call(
        _inproj_kernel,
        grid=(t // ROW_TILE,),
        in_specs=[row(d), _const_spec((1, d)), _const_spec((d, n_in))],
        out_specs=[row(RWKV_IN), row(S5_WIDTH), row(n_qkv)],
        out_shape=[jax.ShapeDtypeStruct((t, RWKV_IN), F32), jax.ShapeDtypeStruct((t, S5_WIDTH), F32),
                   jax.ShapeDtypeStruct((t, n_qkv), BF16)],
        compiler_params=_params("parallel"),
    )(x, gain.reshape(1, d), w_in.astype(BF16))


def _rwkv_kernel(p_ref, mu_ref, w0_ref, wup_ref, a0_ref, aup_ref, gup_ref, kk_ref, ka_ref, rk_ref,
                 lnw_ref, lnb_ref, o_ref, prev_ref, state_ref):
    n = RWKV_CHUNK
    w_at = 3 * RWKV_WIDTH

    @pl.when(pl.program_id(1) == 0)
    def _():
        prev_ref[...] = jnp.zeros_like(prev_ref)
        state_ref[...] = jnp.zeros_like(state_ref)

    p = p_ref[0]
    row = lax.broadcasted_iota(jnp.int32, (n, 1), 0)
    shifted = jnp.where(row == 0, prev_ref[...], pltpu.roll(p, 1, axis=0))
    prev_ref[...] = p[n - 1:n, :]
    p = p + (shifted - p) * mu_ref[...]

    r = p[:, :RWKV_WIDTH]
    k = p[:, RWKV_WIDTH:2 * RWKV_WIDTH]
    v = p[:, 2 * RWKV_WIDTH:w_at]
    w_lo = p[:, w_at:w_at + 64]
    a_lo = p[:, w_at + 64:w_at + 128]
    g_lo = p[:, w_at + 128:]

    w = -jax.nn.softplus(-(w0_ref[...] + _hdot(jnp.tanh(w_lo), wup_ref[...]))) - 0.5
    log_decay = -jnp.exp(w)
    a = jax.nn.sigmoid(a0_ref[...] + _hdot(a_lo, aup_ref[...]))
    gate = _hdot(jax.nn.sigmoid(g_lo), gup_ref[...])

    ch_i = lax.broadcasted_iota(jnp.int32, (RWKV_WIDTH, RWKV_WIDTH), 0) // HEAD_DIM
    ch_j = lax.broadcasted_iota(jnp.int32, (RWKV_WIDTH, RWKV_WIDTH), 1) // HEAD_DIM
    same_head = (ch_i == ch_j).astype(F32)

    kk = k * kk_ref[...]
    kk = kk / jnp.maximum(jnp.sqrt(_hdot(kk * kk, same_head)), 1e-12)
    k = k * (1.0 + (a - 1.0) * ka_ref[...])

    t_i = lax.broadcasted_iota(jnp.int32, (n, n), 0)
    t_j = lax.broadcasted_iota(jnp.int32, (n, n), 1)
    lower = t_i >= t_j
    strictly_lower = t_i > t_j
    eye = (t_i == t_j).astype(F32)

    cum = _hdot(lower.astype(F32), log_decay)
    cum_end = cum[n - 1:n, :]
    inv_g = jnp.exp(-cum)
    to_end = jnp.exp(cum_end - cum)
    beta = kk * a
    r_bar = r * jnp.exp(cum)
    a_bar = -kk * jnp.exp(cum - log_decay)
    b_til = beta * inv_g
    k_til = k * inv_g
    b_end = beta * to_end
    k_end = k * to_end
    g_end = jnp.exp(cum_end)

    outs = []
    for h in range(RWKV_HEADS):
        sl = slice(h * HEAD_DIM, (h + 1) * HEAD_DIM)
        state = state_ref[h]
        ar = jnp.concatenate([a_bar[:, sl], r_bar[:, sl]], axis=0)
        bk = jnp.concatenate([b_til[:, sl], k_til[:, sl]], axis=0)
        m = _hdot_nt(ar, bk)
        a_ab = jnp.where(strictly_lower, m[:n, :n], 0.0)
        a_ak = jnp.where(strictly_lower, m[:n, n:], 0.0)
        a_rb = jnp.where(lower, m[n:, :n], 0.0)
        a_rk = jnp.where(lower, m[n:, n:], 0.0)
        inv = eye + a_ab
        power = a_ab
        for _ in range(int(math.log2(n)) - 1):
            power = _hdot(power, power)
            inv = inv + _hdot(inv, power)
        ars = _hdot_nt(ar, state)
        vh = v[:, sl]
        u = _hdot(inv, ars[:n] + _hdot(a_ak, vh))
        outs.append(ars[n:] + _hdot(a_rb, u) + _hdot(a_rk, vh))
        state_ref[h] = state * g_end[:, sl] + _hdot_tn(u, b_end[:, sl]) + _hdot_tn(vh, k_end[:, sl])
    o = jnp.concatenate(outs, axis=1)

    inv_n = 1.0 / HEAD_DIM
    mean = _hdot(o, same_head) * inv_n
    cen = o - mean
    var = _hdot(cen * cen, same_head) * inv_n
    o = cen * lax.rsqrt(var + RWKV_GN_EPS) * lnw_ref[...] + lnb_ref[...]
    o = o + _hdot(r * k * rk_ref[...], same_head) * v
    o_ref[0] = o * gate


def _rwkv_mix(p, mu, w0, w_up, a0, a_up, g_up, k_k, k_a, r_k, ln_w, ln_b):
    b, s, _ = p.shape
    vec = lambda t: t.reshape(1, -1)
    vecs = [vec(mu), vec(w0), w_up, vec(a0), a_up, g_up, vec(k_k), vec(k_a), vec(r_k), vec(ln_w), vec(ln_b)]
    return pl.pallas_call(
        _rwkv_kernel,
        grid=(b, s // RWKV_CHUNK),
        in_specs=[pl.BlockSpec((1, RWKV_CHUNK, RWKV_IN), lambda i, c: (i, c, 0))]
        + [_const_spec(t.shape) for t in vecs],
        out_specs=pl.BlockSpec((1, RWKV_CHUNK, RWKV_WIDTH), lambda i, c: (i, c, 0)),
        out_shape=jax.ShapeDtypeStruct((b, s, RWKV_WIDTH), F32),
        scratch_shapes=[pltpu.VMEM((1, RWKV_IN), F32), pltpu.VMEM((RWKV_HEADS, HEAD_DIM, HEAD_DIM), F32)],
        compiler_params=_params("parallel", "arbitrary"),
    )(p, *vecs)


def _s5_prep_kernel(are_ref, aim_ref, lstep_ref, bre_ref, bim_ref, bmat_ref, pw_ref):
    lam_re = jnp.minimum(are_ref[...], -1e-4)
    lam_im = aim_ref[...]
    step = jnp.exp(lstep_ref[...])
    z_re, z_im = lam_re * step, lam_im * step
    mag = jnp.exp(z_re)
    num_re, num_im = mag * jnp.cos(z_im) - 1.0, mag * jnp.sin(z_im)
    den = lam_re * lam_re + lam_im * lam_im
    coef_re = (num_re * lam_re + num_im * lam_im) / den
    coef_im = (num_im * lam_re - num_re * lam_im) / den
    bre, bim = bre_ref[...], bim_ref[...]
    bmat_ref[:, :S5_LANES] = (coef_re * bre - coef_im * bim).astype(bmat_ref.dtype)
    bmat_ref[:, S5_LANES:] = (coef_re * bim + coef_im * bre).astype(bmat_ref.dtype)
    steps = (lax.broadcasted_iota(jnp.int32, (S5_CHUNK, 1), 0) + 1).astype(F32)
    pmag = jnp.exp(steps * z_re)
    pw_ref[:, :S5_LANES] = pmag * jnp.cos(steps * z_im)
    pw_ref[:, S5_LANES:] = pmag * jnp.sin(steps * z_im)


def _s5_kernel(u_ref, bmat_ref, pw_ref, cre_ref, cim_ref, d_ref, wglu_ref, bglu_ref, gain_ref, o_ref, carry_ref):
    n = S5_CHUNK

    @pl.when(pl.program_id(1) == 0)
    def _():
        carry_ref[...] = jnp.zeros_like(carry_ref)

    u = u_ref[0]
    bu = jnp.dot(u.astype(BF16), bmat_ref[...], preferred_element_type=F32)
    x_re, x_im = bu[:, :S5_LANES], bu[:, S5_LANES:]
    row = lax.broadcasted_iota(jnp.int32, (n, 1), 0)
    dist = 1
    while dist < n:
        l_re = pw_ref[dist - 1:dist, :S5_LANES]
        l_im = pw_ref[dist - 1:dist, S5_LANES:]
        keep = row >= dist
        s_re = jnp.where(keep, pltpu.roll(x_re, dist, axis=0), 0.0)
        s_im = jnp.where(keep, pltpu.roll(x_im, dist, axis=0), 0.0)
        x_re, x_im = x_re + l_re * s_re - l_im * s_im, x_im + l_re * s_im + l_im * s_re
        dist *= 2
    c_re, c_im = carry_ref[:, :S5_LANES], carry_ref[:, S5_LANES:]
    p_re, p_im = pw_ref[:, :S5_LANES], pw_ref[:, S5_LANES:]
    x_re, x_im = x_re + p_re * c_re - p_im * c_im, x_im + p_re * c_im + p_im * c_re
    carry_ref[:, :S5_LANES] = x_re[n - 1:n, :]
    carry_ref[:, S5_LANES:] = x_im[n - 1:n, :]

    y = (jnp.dot(x_re.astype(BF16), cre_ref[...], preferred_element_type=F32)
         - jnp.dot(x_im.astype(BF16), cim_ref[...], preferred_element_type=F32)) + d_ref[...] * u
    y = jax.nn.gelu(y)
    y = y * jax.nn.sigmoid(jnp.dot(y.astype(BF16), wglu_ref[...], preferred_element_type=F32) + bglu_ref[...])
    o_ref[0] = _rms(y, gain_ref[...])


def _block_diag(t):
    g, rows, cols = t.shape
    return jnp.einsum('grc,gh->grhc', t, jnp.eye(g, dtype=t.dtype)).reshape(g * rows, g * cols)


def _s5_mix(u, a_re, a_im, log_step, b_re, b_im, c_re, c_im, d, w_glu, b_glu, gain):
    b, s, width = u.shape
    lane = lambda t: t.astype(F32).reshape(1, S5_LANES)
    to_cp = lambda t: _block_diag(jnp.swapaxes(t.astype(F32), 1, 2))
    bmat, pw = pl.pallas_call(
        _s5_prep_kernel,
        out_shape=[jax.ShapeDtypeStruct((width, 2 * S5_LANES), BF16),
                   jax.ShapeDtypeStruct((S5_CHUNK, 2 * S5_LANES), F32)],
        compiler_params=pltpu.CompilerParams(vmem_limit_bytes=VMEM_LIMIT),
    )(lane(a_re), lane(a_im), lane(jnp.repeat(log_step[:, None], S5_STATE, axis=1)), to_cp(b_re), to_cp(b_im))
    cre = _block_diag(jnp.swapaxes(c_re, 1, 2)).astype(BF16)
    cim = _block_diag(jnp.swapaxes(c_im, 1, 2)).astype(BF16)
    vec = lambda t: t.astype(F32).reshape(1, width)
    consts = [bmat, pw, cre, cim, vec(d), w_glu.astype(BF16), vec(b_glu), vec(gain)]
    return pl.pallas_call(
        _s5_kernel,
        grid=(b, s // S5_CHUNK),
        in_specs=[pl.BlockSpec((1, S5_CHUNK, width), lambda i, c: (i, c, 0))]
        + [_const_spec(t.shape) for t in consts],
        out_specs=pl.BlockSpec((1, S5_CHUNK, width), lambda i, c: (i, c, 0)),
        out_shape=jax.ShapeDtypeStruct((b, s, width), F32),
        scratch_shapes=[pltpu.VMEM((1, 2 * S5_LANES), F32)],
        compiler_params=_params("parallel", "arbitrary"),
    )(u, *consts)


def _attn_kernel(q_ref, *refs):
    k_refs = refs[:ATTN_KEY_BLOCKS]
    v_refs = refs[ATTN_KEY_BLOCKS:2 * ATTN_KEY_BLOCKS]
    bias_ref, gain_ref, o_ref = refs[2 * ATTN_KEY_BLOCKS:]
    n_keys = ATTN_KEY_BLOCKS * ATTN_TILE
    q = q_ref[0]
    k = jnp.concatenate([kr[0] for kr in k_refs], axis=0)
    v = jnp.concatenate([vr[0] for vr in v_refs], axis=0)
    key_row = pl.program_id(1) * ATTN_TILE + lax.broadcasted_iota(jnp.int32, (1, n_keys), 1)
    before_start = key_row < ATTN_LEFT * ATTN_CHUNK
    neg = jnp.finfo(F32).min
    outs = []
    for h in range(ATTN_HEADS):
        sl = slice(h * HEAD_DIM, (h + 1) * HEAD_DIM)
        s = lax.dot_general(q[:, sl], k[:, sl], (((1,), (1,)), ((), ())), preferred_element_type=F32)
        bias = bias_ref[h]
        s = jnp.where((bias == neg) | before_start, neg, s * (HEAD_DIM ** -0.5) + bias)
        e = jnp.exp(s - jnp.max(s, axis=-1, keepdims=True))
        pv = jnp.dot(e.astype(BF16), v[:, sl], preferred_element_type=F32)
        outs.append(pv / jnp.sum(e, axis=-1, keepdims=True))
    o_ref[0] = _rms(jnp.concatenate(outs, axis=1), gain_ref[...])


def _attn_bias_table(rel_bias):
    n_keys = ATTN_KEY_BLOCKS * ATTN_TILE
    q = np.arange(ATTN_TILE)[:, None]
    j = np.arange(n_keys)[None, :]
    rel = np.clip(ATTN_LEFT * ATTN_CHUNK + q - j, -MAX_REL, MAX_REL) + MAX_REL
    in_band = (j // ATTN_CHUNK >= q // ATTN_CHUNK) & (j // ATTN_CHUNK <= q // ATTN_CHUNK + ATTN_LEFT)
    return jnp.where(in_band[None], rel_bias.astype(F32)[:, rel], jnp.finfo(F32).min)


def _chunk_attention(qkv, rel_bias, gain):
    b, s, _ = qkv.shape
    pad = ((0, 0), (ATTN_LEFT * ATTN_CHUNK, 0), (0, 0))
    kp = jnp.pad(qkv[:, :, ATTN_WIDTH:2 * ATTN_WIDTH], pad)
    vp = jnp.pad(qkv[:, :, 2 * ATTN_WIDTH:], pad)
    bias = _attn_bias_table(rel_bias)
    key_specs = [pl.BlockSpec((1, ATTN_TILE, ATTN_WIDTH), lambda i, c, j=j: (i, c + j, 0))
                 for j in range(ATTN_KEY_BLOCKS)]
    return pl.pallas_call(
        _attn_kernel,
        grid=(b, s // ATTN_TILE),
        in_specs=[pl.BlockSpec((1, ATTN_TILE, ATTN_WIDTH), lambda i, c: (i, c, 0))] + key_specs + key_specs
        + [_const_spec(bias.shape), _const_spec((1, ATTN_WIDTH))],
        out_specs=pl.BlockSpec((1, ATTN_TILE, ATTN_WIDTH), lambda i, c: (i, c, 0)),
        out_shape=jax.ShapeDtypeStruct((b, s, ATTN_WIDTH), F32),
        compiler_params=_params("parallel", "parallel"),
    )(qkv, *([kp] * ATTN_KEY_BLOCKS), *([vp] * ATTN_KEY_BLOCKS), bias, gain.astype(F32).reshape(1, ATTN_WIDTH))


def _outproj_kernel(route, x_ref, orw_ref, os5_ref, oat_ref, w_ref, gpost_ref, gpre_ref, *refs):
    mixed = jnp.concatenate([orw_ref[...], os5_ref[...], oat_ref[...]], axis=1).astype(BF16)
    y = jnp.dot(mixed, w_ref[...], preferred_element_type=F32)
    x = x_ref[...] + _rms(y, gpost_ref[...])
    h = _rms(x, gpre_ref[...])
    if not route:
        x_out, h_out = refs
    else:
        router_ref, x_out, h_out, comb_out = refs
        logits = _hdot(h, router_ref[...])
        lane = lax.broadcasted_iota(jnp.int32, logits.shape, 1)
        neg = jnp.finfo(F32).min
        logits = jnp.where(lane < N_EXPERTS, logits, neg)
        top1 = jnp.max(logits, axis=-1, keepdims=True)
        idx1 = jnp.min(jnp.where(logits == top1, lane, LANES), axis=-1, keepdims=True)
        rest = jnp.where(lane == idx1, neg, logits)
        top2 = jnp.max(rest, axis=-1, keepdims=True)
        idx2 = jnp.min(jnp.where(rest == top2, lane, LANES), axis=-1, keepdims=True)
        e2 = jnp.exp(top2 - top1)
        comb_out[...] = jnp.where(lane == idx1, 1.0 / (1.0 + e2), 0.0) + jnp.where(lane == idx2, e2 / (1.0 + e2), 0.0)
    x_out[...] = x
    h_out[...] = h.astype(BF16)


def _out_projection(x, o_rwkv, o_s5, o_attn, w_out, gain_post, gain_pre, router):
    t, d = x.shape
    row = lambda n: pl.BlockSpec((ROW_TILE, n), lambda i: (i, 0))
    vec = lambda g: g.astype(F32).reshape(1, d)
    ins = [x, o_rwkv, o_s5, o_attn, w_out.astype(BF16), vec(gain_post), vec(gain_pre)]
    in_specs = [row(d), row(o_rwkv.shape[1]), row(o_s5.shape[1]), row(o_attn.shape[1]),
                _const_spec(w_out.shape), _const_spec((1, d)), _const_spec((1, d))]
    out_specs = [row(d), row(d)]
    out_shape = [jax.ShapeDtypeStruct((t, d), F32), jax.ShapeDtypeStruct((t, d), BF16)]
    if router is not None:
        ins.append(jnp.pad(router.astype(F32), ((0, 0), (0, LANES - N_EXPERTS))))
        in_specs.append(_const_spec((d, LANES)))
        out_specs.append(row(LANES))
        out_shape.append(jax.ShapeDtypeStruct((t, LANES), F32))
    return pl.pallas_call(
        functools.partial(_outproj_kernel, router is not None),
        grid=(t // ROW_TILE,),
        in_specs=in_specs, out_specs=out_specs, out_shape=out_shape,
        compiler_params=_params("parallel"),
    )(*ins)


def _ffn_kernel(routed, x_ref, h_ref, *refs):
    if routed:
        comb_ref, wg_ref, wu_ref, wd_ref, gain_ref, o_ref, acc_ref = refs
    else:
        wg_ref, wu_ref, wd_ref, gain_ref, o_ref, acc_ref = refs
    e, f = pl.program_id(1), pl.program_id(2)

    @pl.when((e == 0) & (f == 0))
    def _():
        acc_ref[...] = jnp.zeros_like(acc_ref)

    h = h_ref[...]
    act = jax.nn.silu(jnp.dot(h, wg_ref[0], preferred_element_type=F32)) * jnp.dot(h, wu_ref[0], preferred_element_type=F32)
    y = jnp.dot(act.astype(BF16), wd_ref[0], preferred_element_type=F32)
    if routed:
        comb = comb_ref[...]
        lane = lax.broadcasted_iota(jnp.int32, comb.shape, 1)
        y = y * jnp.sum(jnp.where(lane == e, comb, 0.0), axis=-1, keepdims=True)
    acc_ref[...] += y

    @pl.when((e == pl.num_programs(1) - 1) & (f == pl.num_programs(2) - 1))
    def _():
        o_ref[...] = x_ref[...] + _rms(acc_ref[...], gain_ref[...])


def _ffn(x, h, combine, w_gate, w_up, w_down, gain):
    t, d = x.shape
    n_e, _, d_ff = w_gate.shape
    routed = combine is not None
    row = lambda n: pl.BlockSpec((ROW_TILE, n), lambda i, e, f: (i, 0))
    ins = [x, h] + ([combine] if routed else [])
    in_specs = [row(d), row(d)] + ([row(LANES)] if routed else [])
    ins += [w_gate.astype(BF16), w_up.astype(BF16), w_down.astype(BF16), gain.astype(F32).reshape(1, d)]
    in_specs += [pl.BlockSpec((1, d, FFN_TILE), lambda i, e, f: (e, 0, f)),
                 pl.BlockSpec((1, d, FFN_TILE), lambda i, e, f: (e, 0, f)),
                 pl.BlockSpec((1, FFN_TILE, d), lambda i, e, f: (e, f, 0)),
                 pl.BlockSpec((1, d), lambda i, e, f: (0, 0))]
    return pl.pallas_call(
        functools.partial(_ffn_kernel, routed),
        grid=(t // ROW_TILE, n_e, d_ff // FFN_TILE),
        in_specs=in_specs,
        out_specs=row(d),
        out_shape=jax.ShapeDtypeStruct((t, d), F32),
        scratch_shapes=[pltpu.VMEM((ROW_TILE, d), F32)],
        compiler_params=_params("parallel", "arbitrary", "arbitrary"),
    )(*ins)


def kernel(x, norm_mix_pre, norm_mix_post, norm_ffn_pre, norm_ffn_post, w_in, w_out, rwkv_mu, rwkv_w0, rwkv_w_up, rwkv_a0, rwkv_a_up, rwkv_g_up, rwkv_k_k, rwkv_k_a, rwkv_r_k, rwkv_ln_w, rwkv_ln_b, s5_a_re, s5_a_im, s5_log_step, s5_b_re, s5_b_im, s5_c_re, s5_c_im, s5_d, s5_w_glu, s5_b_glu, s5_norm, attn_rel_bias, attn_norm, ffn_w_gate, ffn_w_up, ffn_w_down, moe_router, moe_w_gate, moe_w_up, moe_w_down):
    b, s, d = x.shape
    depth = w_in.shape[0]
    xt = x.reshape(b * s, d)
    for layer in range(depth):
        p, u, qkv = _in_projection(xt, norm_mix_pre[layer], w_in[layer])
        o_rwkv = _rwkv_mix(p.reshape(b, s, -1), rwkv_mu[layer], rwkv_w0[layer], rwkv_w_up[layer], rwkv_a0[layer],
                           rwkv_a_up[layer], rwkv_g_up[layer], rwkv_k_k[layer], rwkv_k_a[layer], rwkv_r_k[layer],
                           rwkv_ln_w[layer], rwkv_ln_b[layer])
        o_s5 = _s5_mix(u.reshape(b, s, -1), s5_a_re[layer], s5_a_im[layer], s5_log_step[layer], s5_b_re[layer],
                       s5_b_im[layer], s5_c_re[layer], s5_c_im[layer], s5_d[layer], s5_w_glu[layer],
                       s5_b_glu[layer], s5_norm[layer])
        o_attn = _chunk_attention(qkv.reshape(b, s, -1), attn_rel_bias[layer], attn_norm[layer])
        i = layer // 2
        moe = layer % 2 == 1
        outs = _out_projection(xt, o_rwkv.reshape(b * s, -1), o_s5.reshape(b * s, -1), o_attn.reshape(b * s, -1),
                               w_out[layer], norm_mix_post[layer], norm_ffn_pre[layer],
                               moe_router[i] if moe else None)
        if moe:
            xt, h, combine = outs
            xt = _ffn(xt, h, combine, moe_w_gate[i], moe_w_up[i], moe_w_down[i], norm_ffn_post[layer])
        else:
            xt, h = outs
            xt = _ffn(xt, h, None, ffn_w_gate[i][None], ffn_w_up[i][None], ffn_w_down[i][None], norm_ffn_post[layer])
    return xt.reshape(b, s, d)
```

```python
import functools
import math

import jax
import jax.numpy as jnp
import numpy as np
from jax import lax
from jax.experimental import pallas as pl
from jax.experimental.pallas import tpu as pltpu

F32 = jnp.float32
BF16 = jnp.bfloat16

RMS_EPS = 1e-6
RWKV_GN_EPS = 64e-5
HEAD_DIM = 64
RWKV_HEADS = 6
RWKV_WIDTH = 384
RWKV_IN = 1408
S5_WIDTH = 256
S5_GROUPS = 16
S5_STATE = 64
S5_LANES = S5_GROUPS * S5_STATE
ATTN_HEADS = 6
ATTN_WIDTH = 384
ATTN_CHUNK = 64
ATTN_LEFT = 8
MAX_REL = 128
N_EXPERTS = 8
LANES = 128

RWKV_CHUNK = 64
S5_CHUNK = 128
ATTN_TILE = 2 * ATTN_CHUNK
ATTN_KEY_BLOCKS = (ATTN_LEFT * ATTN_CHUNK) // ATTN_TILE + 1
ROW_TILE = 512
FFN_TILE = 1408
VMEM_LIMIT = 48 * 1024 * 1024

_HI = lax.Precision.HIGHEST
_CONTRACT = {"nn": (((1,), (0,)), ((), ())), "nt": (((1,), (1,)), ((), ())), "tn": (((0,), (0,)), ((), ()))}


def _hdot(a, b):
    return jnp.dot(a, b, precision=_HI, preferred_element_type=F32)


def _terms(x, n):
    out = []
    for i in range(n):
        t = x.astype(BF16)
        out.append(t)
        if i + 1 < n:
            x = x - t.astype(F32)
    return tuple(out)


def _mm(a, b, kind="nn"):
    order = max(len(a), len(b))
    acc = None
    for i, ai in enumerate(a):
        for j, bj in enumerate(b):
            if i + j < order:
                d = lax.dot_general(ai, bj, _CONTRACT[kind], preferred_element_type=F32)
                acc = d if acc is None else acc + d
    return acc


def _rms(x, gain):
    return x * lax.rsqrt(jnp.mean(x * x, axis=-1, keepdims=True) + RMS_EPS) * gain


def _params(*semantics):
    return pltpu.CompilerParams(dimension_semantics=semantics, vmem_limit_bytes=VMEM_LIMIT)


def _const_spec(shape):
    zeros = (0,) * len(shape)
    return pl.BlockSpec(shape, lambda *_: zeros)


def _inproj_kernel(x_ref, gain_ref, w_ref, p_ref, u_ref, qkv_ref):
    h = _rms(x_ref[...], gain_ref[...]).astype(BF16)
    s5_at, qkv_at = RWKV_IN, RWKV_IN + S5_WIDTH
    p_ref[...] = jnp.dot(h, w_ref[:, :s5_at], preferred_element_type=F32)
    u_ref[...] = jnp.dot(h, w_ref[:, s5_at:qkv_at], preferred_element_type=F32)
    qkv_ref[...] = jnp.dot(h, w_ref[:, qkv_at:], preferred_element_type=F32).astype(BF16)


def _in_projection(x, gain, w_in):
    t, d = x.shape
    n_in = w_in.shape[1]
    n_qkv = n_in - RWKV_IN - S5_WIDTH
    row = lambda n: pl.BlockSpec((ROW_TILE, n), lambda i: (i, 0))
    return pl.pallas_call(
        _inproj_kernel,
        name="in_projection",
        grid=(t // ROW_TILE,),
        in_specs=[row(d), _const_spec((1, d)), _const_spec((d, n_in))],
        out_specs=[row(RWKV_IN), row(S5_WIDTH), row(n_qkv)],
        out_shape=[jax.ShapeDtypeStruct((t, RWKV_IN), F32), jax.ShapeDtypeStruct((t, S5_WIDTH), F32),
                   jax.ShapeDtypeStruct((t, n_qkv), BF16)],
        compiler_params=_params("parallel"),
    )(x, gain.reshape(1, d), w_in.astype(BF16))


RWKV_TERMS = 2
HEADS_PER_VREG = LANES // HEAD_DIM


def _t(x):
    return _terms(x, RWKV_TERMS)


def _head_sums(x):
    first = lax.broadcasted_iota(jnp.int32, (1, LANES), 1) < HEAD_DIM
    outs = []
    for j in range(x.shape[1] // LANES):
        xp = x[:, j * LANES:(j + 1) * LANES]
        s0 = jnp.sum(jnp.where(first, xp, 0.0), axis=-1, keepdims=True)
        s1 = jnp.sum(jnp.where(first, 0.0, xp), axis=-1, keepdims=True)
        outs.append(jnp.where(first, s0, s1))
    return jnp.concatenate(outs, axis=1)


def _rwkv_kernel(p_ref, mu_ref, w0_ref, wup_ref, a0_ref, aup_ref, gup_ref, kk_ref, ka_ref, rk_ref,
                 lnw_ref, lnb_ref, o_ref, prev_ref, state_ref):
    nb = p_ref.shape[0]
    n = RWKV_CHUNK
    rows = nb * n
    w_at = 3 * RWKV_WIDTH
    pairs = RWKV_WIDTH // LANES

    @pl.when(pl.program_id(0) == 0)
    def _():
        prev_ref[...] = jnp.zeros_like(prev_ref)
        state_ref[...] = jnp.zeros_like(state_ref)

    p = p_ref[...].reshape(rows, RWKV_IN)
    row = lax.broadcasted_iota(jnp.int32, (rows, 1), 0)
    shifted = pltpu.roll(p, 1, axis=0)
    for b in range(nb):
        shifted = jnp.where(row == b * n, prev_ref[b:b + 1, :], shifted)
    for b in range(nb):
        prev_ref[b:b + 1, :] = p[(b + 1) * n - 1:(b + 1) * n, :]
    p = p + (shifted - p) * mu_ref[...]

    r = p[:, :RWKV_WIDTH]
    k = p[:, RWKV_WIDTH:2 * RWKV_WIDTH]
    v = p[:, 2 * RWKV_WIDTH:w_at]
    w_lo = p[:, w_at:w_at + 64]
    a_lo = p[:, w_at + 64:w_at + 128]
    g_lo = p[:, w_at + 128:]

    lora = lambda x, w_ref: _mm(_t(x), tuple(w_ref[i] for i in range(RWKV_TERMS)))
    w = -jax.nn.softplus(-(w0_ref[...] + lora(jnp.tanh(w_lo), wup_ref))) - 0.5
    log_decay = -jnp.exp(w)
    a = jax.nn.sigmoid(a0_ref[...] + lora(a_lo, aup_ref))
    gate = lora(jax.nn.sigmoid(g_lo), gup_ref)

    kk = k * kk_ref[...]
    kk = kk / jnp.maximum(jnp.sqrt(_head_sums(kk * kk)), 1e-12)
    k = k * (1.0 + (a - 1.0) * ka_ref[...])

    s_i = lax.broadcasted_iota(jnp.int32, (rows, rows), 0)
    s_j = lax.broadcasted_iota(jnp.int32, (rows, rows), 1)
    causal = ((s_i // n == s_j // n) & (s_i >= s_j)).astype(BF16)
    cum = _mm((causal,), _terms(log_decay, 3))
    cum_end = cum[n - 1:n, :]
    for b in range(1, nb):
        cum_end = jnp.where(row >= b * n, cum[(b + 1) * n - 1:(b + 1) * n, :], cum_end)
    inv_g = jnp.exp(-cum)
    to_end = jnp.exp(cum_end - cum)
    beta = kk * a
    r_bar = r * jnp.exp(cum)
    a_bar = -kk * jnp.exp(cum - log_decay)
    b_til = beta * inv_g
    k_til = k * inv_g
    b_end = beta * to_end
    k_end = k * to_end
    g_end = jnp.exp(cum_end)

    lane = lax.broadcasted_iota(jnp.int32, (1, LANES), 1)
    q_i = lax.broadcasted_iota(jnp.int32, (n, LANES), 0)
    q_j = lax.broadcasted_iota(jnp.int32, (n, LANES), 1) % n
    lower = q_i >= q_j
    strictly_lower = q_i > q_j
    eye = (lax.broadcasted_iota(jnp.int32, (n, n), 0) == lax.broadcasted_iota(jnp.int32, (n, n), 1)).astype(F32)
    c_i = lax.broadcasted_iota(jnp.int32, (LANES, LANES), 0)
    c_j = lax.broadcasted_iota(jnp.int32, (LANES, LANES), 1)
    same_head = c_i // HEAD_DIM == c_j // HEAD_DIM
    diagonal = c_i == c_j
    zeros = jnp.zeros((n, LANES), F32)

    pair_ids = [(b, j) for b in range(nb) for j in range(pairs)]
    head_ids = [(pi, h) for pi in range(len(pair_ids)) for h in range(HEADS_PER_VREG)]
    heads_of = lambda pi: [hi for hi, (p_i, _) in enumerate(head_ids) if p_i == pi]
    cut = lambda x, pi: x[pair_ids[pi][0] * n:(pair_ids[pi][0] + 1) * n,
                          pair_ids[pi][1] * LANES:(pair_ids[pi][1] + 1) * LANES]
    rb = [cut(r_bar, pi) for pi in range(len(pair_ids))]
    be = [cut(b_end, pi) for pi in range(len(pair_ids))]
    vv = [cut(v, pi) for pi in range(len(pair_ids))]
    bk = [_t(jnp.concatenate([cut(b_til, pi), cut(k_til, pi)], axis=0)) for pi in range(len(pair_ids))]
    mine = [lane // HEAD_DIM == h for _, h in head_ids]
    abm = [jnp.where(mine[hi], cut(a_bar, pi), 0.0) for hi, (pi, _) in enumerate(head_ids)]
    v0 = [_t(jnp.concatenate([zeros, jnp.where(mine[hi], vv[pi], 0.0)], axis=0)) for hi, (pi, _) in enumerate(head_ids)]
    m = [_mm(_t(jnp.concatenate([abm[hi], jnp.where(mine[hi], rb[pi], 0.0)], axis=0)), bk[pi], "nt")
         for hi, (pi, _) in enumerate(head_ids)]
    top = [jnp.where(strictly_lower, x[:n], 0.0) for x in m]
    bot = [jnp.where(lower, x[n:], 0.0) for x in m]
    y = [_mm(_t(top[hi]), v0[hi]) for hi in range(len(head_ids))]
    power = [x[:, :n] for x in top]
    inv = [eye + x for x in power]
    for _ in range(int(math.log2(n)) - 1):
        pt = [_t(x) for x in power]
        power = [_mm(x, x) for x in pt]
        inv = [x + _mm(_t(x), _t(pw)) for x, pw in zip(inv, power)]
    wu = [_mm(_t(inv[hi]), _t(jnp.concatenate([abm[hi], y[hi]], axis=1))) for hi in range(len(head_ids))]
    v_low = [tuple(jnp.concatenate([jnp.zeros((n, LANES), BF16), t[n:]], axis=1) for t in v0[hi])
             for hi in range(len(head_ids))]
    ro = [_mm(_t(bot[hi]), tuple(jnp.concatenate([a, b2], axis=0) for a, b2 in zip(_t(wu[hi]), v_low[hi])))
          for hi in range(len(head_ids))]
    both = lambda xs, pi, sl: sum(xs[hi][:, sl] for hi in heads_of(pi))
    first, second = slice(0, LANES), slice(LANES, 2 * LANES)
    state = [_t(state_ref[pi]) for pi in range(len(pair_ids))]
    out = [_mm(_t(rb[pi] + both(ro, pi, first)), state[pi]) + both(ro, pi, second) for pi in range(len(pair_ids))]
    g_t = [jnp.where(same_head, _mm(_t(be[pi]), _t(both(wu, pi, first)), "tn"), 0.0)
           + jnp.where(diagonal, cut(g_end, pi)[:1], 0.0) for pi in range(len(pair_ids))]
    s2_t = [_mm(_t(jnp.concatenate([be[pi], cut(k_end, pi)], axis=0)),
                _t(jnp.concatenate([both(wu, pi, second), vv[pi]], axis=0)), "tn") for pi in range(len(pair_ids))]
    for pi in range(len(pair_ids)):
        state_ref[pi] = _mm(_t(g_t[pi]), state[pi]) + jnp.where(same_head, s2_t[pi], 0.0)
    o = jnp.concatenate([jnp.concatenate(out[b * pairs:(b + 1) * pairs], axis=1) for b in range(nb)], axis=0)

    inv_n = 1.0 / HEAD_DIM
    cen = o - _head_sums(o) * inv_n
    var = _head_sums(cen * cen) * inv_n
    o = cen * lax.rsqrt(var + RWKV_GN_EPS) * lnw_ref[...] + lnb_ref[...]
    o = o + _head_sums(r * k * rk_ref[...]) * v
    o_ref[...] = (o * gate).reshape(nb, n, RWKV_WIDTH)


def _rwkv_mix(p, mu, w0, w_up, a0, a_up, g_up, k_k, k_a, r_k, ln_w, ln_b):
    b, s, _ = p.shape
    vec = lambda t: t.astype(F32).reshape(1, -1)
    split = lambda t: jnp.stack(_terms(t.astype(F32), RWKV_TERMS))
    consts = [vec(mu), vec(w0), split(w_up), vec(a0), split(a_up), split(g_up), vec(k_k), vec(k_a), vec(r_k),
              vec(ln_w), vec(ln_b)]
    return pl.pallas_call(
        _rwkv_kernel,
        name="rwkv7",
        grid=(s // RWKV_CHUNK,),
        in_specs=[pl.BlockSpec((b, RWKV_CHUNK, RWKV_IN), lambda c: (0, c, 0))]
        + [_const_spec(t.shape) for t in consts],
        out_specs=pl.BlockSpec((b, RWKV_CHUNK, RWKV_WIDTH), lambda c: (0, c, 0)),
        out_shape=jax.ShapeDtypeStruct((b, s, RWKV_WIDTH), F32),
        scratch_shapes=[pltpu.VMEM((b, RWKV_IN), F32),
                        pltpu.VMEM((b * RWKV_WIDTH // LANES, LANES, LANES), F32)],
        compiler_params=_params("arbitrary"),
    )(p, *consts)


def _s5_prep_kernel(are_ref, aim_ref, lstep_ref, bre_ref, bim_ref, bmat_ref, pw_ref):
    lam_re = jnp.minimum(are_ref[...], -1e-4)
    lam_im = aim_ref[...]
    step = jnp.exp(lstep_ref[...])
    z_re, z_im = lam_re * step, lam_im * step
    mag = jnp.exp(z_re)
    num_re, num_im = mag * jnp.cos(z_im) - 1.0, mag * jnp.sin(z_im)
    den = lam_re * lam_re + lam_im * lam_im
    coef_re = (num_re * lam_re + num_im * lam_im) / den
    coef_im = (num_im * lam_re - num_re * lam_im) / den
    bre, bim = bre_ref[...], bim_ref[...]
    bmat_ref[:, :S5_LANES] = (coef_re * bre - coef_im * bim).astype(bmat_ref.dtype)
    bmat_ref[:, S5_LANES:] = (coef_re * bim + coef_im * bre).astype(bmat_ref.dtype)
    steps = (lax.broadcasted_iota(jnp.int32, (S5_CHUNK, 1), 0) + 1).astype(F32)
    pmag = jnp.exp(steps * z_re)
    pw_ref[:, :S5_LANES] = pmag * jnp.cos(steps * z_im)
    pw_ref[:, S5_LANES:] = pmag * jnp.sin(steps * z_im)


def _s5_kernel(u_ref, bmat_ref, pw_ref, cre_ref, cim_ref, d_ref, wglu_ref, bglu_ref, gain_ref, o_ref, carry_ref):
    n = S5_CHUNK

    @pl.when(pl.program_id(1) == 0)
    def _():
        carry_ref[...] = jnp.zeros_like(carry_ref)

    u = u_ref[0]
    bu = jnp.dot(u.astype(BF16), bmat_ref[...], preferred_element_type=F32)
    x_re, x_im = bu[:, :S5_LANES], bu[:, S5_LANES:]
    row = lax.broadcasted_iota(jnp.int32, (n, 1), 0)
    dist = 1
    while dist < n:
        l_re = pw_ref[dist - 1:dist, :S5_LANES]
        l_im = pw_ref[dist - 1:dist, S5_LANES:]
        keep = row >= dist
        s_re = jnp.where(keep, pltpu.roll(x_re, dist, axis=0), 0.0)
        s_im = jnp.where(keep, pltpu.roll(x_im, dist, axis=0), 0.0)
        x_re, x_im = x_re + l_re * s_re - l_im * s_im, x_im + l_re * s_im + l_im * s_re
        dist *= 2
    c_re, c_im = carry_ref[:, :S5_LANES], carry_ref[:, S5_LANES:]
    p_re, p_im = pw_ref[:, :S5_LANES], pw_ref[:, S5_LANES:]
    x_re, x_im = x_re + p_re * c_re - p_im * c_im, x_im + p_re * c_im + p_im * c_re
    carry_ref[:, :S5_LANES] = x_re[n - 1:n, :]
    carry_ref[:, S5_LANES:] = x_im[n - 1:n, :]

    y = (jnp.dot(x_re.astype(BF16), cre_ref[...], preferred_element_type=F32)
         - jnp.dot(x_im.astype(BF16), cim_ref[...], preferred_element_type=F32)) + d_ref[...] * u
    y = jax.nn.gelu(y)
    y = y * jax.nn.sigmoid(jnp.dot(y.astype(BF16), wglu_ref[...], preferred_element_type=F32) + bglu_ref[...])
    o_ref[0] = _rms(y, gain_ref[...])


def _block_diag(t):
    g, rows, cols = t.shape
    return jnp.einsum('grc,gh->grhc', t, jnp.eye(g, dtype=t.dtype)).reshape(g * rows, g * cols)


def _s5_mix(u, a_re, a_im, log_step, b_re, b_im, c_re, c_im, d, w_glu, b_glu, gain):
    b, s, width = u.shape
    lane = lambda t: t.astype(F32).reshape(1, S5_LANES)
    to_cp = lambda t: _block_diag(jnp.swapaxes(t.astype(F32), 1, 2))
    bmat, pw = pl.pallas_call(
        _s5_prep_kernel,
        name="s5_discretise",
        out_shape=[jax.ShapeDtypeStruct((width, 2 * S5_LANES), BF16),
                   jax.ShapeDtypeStruct((S5_CHUNK, 2 * S5_LANES), F32)],
        compiler_params=pltpu.CompilerParams(vmem_limit_bytes=VMEM_LIMIT),
    )(lane(a_re), lane(a_im), lane(jnp.repeat(log_step[:, None], S5_STATE, axis=1)), to_cp(b_re), to_cp(b_im))
    cre = _block_diag(jnp.swapaxes(c_re, 1, 2)).astype(BF16)
    cim = _block_diag(jnp.swapaxes(c_im, 1, 2)).astype(BF16)
    vec = lambda t: t.astype(F32).reshape(1, width)
    consts = [bmat, pw, cre, cim, vec(d), w_glu.astype(BF16), vec(b_glu), vec(gain)]
    return pl.pallas_call(
        _s5_kernel,
        name="s5_scan",
        grid=(b, s // S5_CHUNK),
        in_specs=[pl.BlockSpec((1, S5_CHUNK, width), lambda i, c: (i, c, 0))]
        + [_const_spec(t.shape) for t in consts],
        out_specs=pl.BlockSpec((1, S5_CHUNK, width), lambda i, c: (i, c, 0)),
        out_shape=jax.ShapeDtypeStruct((b, s, width), F32),
        scratch_shapes=[pltpu.VMEM((1, 2 * S5_LANES), F32)],
        compiler_params=_params("parallel", "arbitrary"),
    )(u, *consts)


def _attn_kernel(q_ref, *refs):
    k_refs = refs[:ATTN_KEY_BLOCKS]
    v_refs = refs[ATTN_KEY_BLOCKS:2 * ATTN_KEY_BLOCKS]
    bias_ref, gain_ref, o_ref = refs[2 * ATTN_KEY_BLOCKS:]
    n_keys = ATTN_KEY_BLOCKS * ATTN_TILE
    q = q_ref[0]
    k = jnp.concatenate([kr[0] for kr in k_refs], axis=0)
    v = jnp.concatenate([vr[0] for vr in v_refs], axis=0)
    key_row = pl.program_id(1) * ATTN_TILE + lax.broadcasted_iota(jnp.int32, (1, n_keys), 1)
    before_start = key_row < ATTN_LEFT * ATTN_CHUNK
    neg = jnp.finfo(F32).min
    outs = []
    for h in range(ATTN_HEADS):
        sl = slice(h * HEAD_DIM, (h + 1) * HEAD_DIM)
        s = lax.dot_general(q[:, sl], k[:, sl], (((1,), (1,)), ((), ())), preferred_element_type=F32)
        bias = bias_ref[h]
        s = jnp.where((bias == neg) | before_start, neg, s * (HEAD_DIM ** -0.5) + bias)
        e = jnp.exp(s - jnp.max(s, axis=-1, keepdims=True))
        pv = jnp.dot(e.astype(BF16), v[:, sl], preferred_element_type=F32)
        outs.append(pv / jnp.sum(e, axis=-1, keepdims=True))
    o_ref[0] = _rms(jnp.concatenate(outs, axis=1), gain_ref[...])


def _attn_bias_table(rel_bias):
    n_keys = ATTN_KEY_BLOCKS * ATTN_TILE
    left = ATTN_LEFT * ATTN_CHUNK
    period = n_keys + ATTN_TILE
    rb = rel_bias.astype(F32)
    heads = rb.shape[0]
    far = lambda width: jnp.broadcast_to(rb[:, 2 * MAX_REL:], (heads, width))
    g = jnp.concatenate([far(left - MAX_REL), rb[:, 1:][:, ::-1], far(period - left - MAX_REL)], axis=1)
    toeplitz = jnp.tile(g, (1, ATTN_TILE))[:, :ATTN_TILE * (period - 1)].reshape(heads, ATTN_TILE, period - 1)
    q = np.arange(ATTN_TILE)[:, None]
    j = np.arange(n_keys)[None, :]
    in_band = (j // ATTN_CHUNK >= q // ATTN_CHUNK) & (j // ATTN_CHUNK <= q // ATTN_CHUNK + ATTN_LEFT)
    return jnp.where(in_band[None], toeplitz[:, :, :n_keys], jnp.finfo(F32).min)


def _chunk_attention(qkv, rel_bias, gain):
    b, s, _ = qkv.shape
    pad = ((0, 0), (ATTN_LEFT * ATTN_CHUNK, 0), (0, 0))
    kp = jnp.pad(qkv[:, :, ATTN_WIDTH:2 * ATTN_WIDTH], pad)
    vp = jnp.pad(qkv[:, :, 2 * ATTN_WIDTH:], pad)
    bias = _attn_bias_table(rel_bias)
    key_specs = [pl.BlockSpec((1, ATTN_TILE, ATTN_WIDTH), lambda i, c, j=j: (i, c + j, 0))
                 for j in range(ATTN_KEY_BLOCKS)]
    return pl.pallas_call(
        _attn_kernel,
        name="chunk_attention",
        grid=(b, s // ATTN_TILE),
        in_specs=[pl.BlockSpec((1, ATTN_TILE, ATTN_WIDTH), lambda i, c: (i, c, 0))] + key_specs + key_specs
        + [_const_spec(bias.shape), _const_spec((1, ATTN_WIDTH))],
        out_specs=pl.BlockSpec((1, ATTN_TILE, ATTN_WIDTH), lambda i, c: (i, c, 0)),
        out_shape=jax.ShapeDtypeStruct((b, s, ATTN_WIDTH), F32),
        compiler_params=_params("parallel", "parallel"),
    )(qkv, *([kp] * ATTN_KEY_BLOCKS), *([vp] * ATTN_KEY_BLOCKS), bias, gain.astype(F32).reshape(1, ATTN_WIDTH))


def _outproj_kernel(route, x_ref, orw_ref, os5_ref, oat_ref, w_ref, gpost_ref, gpre_ref, *refs):
    mixed = jnp.concatenate([orw_ref[...], os5_ref[...], oat_ref[...]], axis=1).astype(BF16)
    y = jnp.dot(mixed, w_ref[...], preferred_element_type=F32)
    x = x_ref[...] + _rms(y, gpost_ref[...])
    h = _rms(x, gpre_ref[...])
    if not route:
        x_out, h_out = refs
    else:
        router_ref, x_out, h_out, comb_out = refs
        logits = _hdot(h, router_ref[...])
        lane = lax.broadcasted_iota(jnp.int32, logits.shape, 1)
        neg = jnp.finfo(F32).min
        logits = jnp.where(lane < N_EXPERTS, logits, neg)
        top1 = jnp.max(logits, axis=-1, keepdims=True)
        idx1 = jnp.min(jnp.where(logits == top1, lane, LANES), axis=-1, keepdims=True)
        rest = jnp.where(lane == idx1, neg, logits)
        top2 = jnp.max(rest, axis=-1, keepdims=True)
        idx2 = jnp.min(jnp.where(rest == top2, lane, LANES), axis=-1, keepdims=True)
        e2 = jnp.exp(top2 - top1)
        comb_out[...] = jnp.where(lane == idx1, 1.0 / (1.0 + e2), 0.0) + jnp.where(lane == idx2, e2 / (1.0 + e2), 0.0)
    x_out[...] = x
    h_out[...] = h.astype(BF16)


def _out_projection(x, o_rwkv, o_s5, o_attn, w_out, gain_post, gain_pre, router):
    t, d = x.shape
    row = lambda n: pl.BlockSpec((ROW_TILE, n), lambda i: (i, 0))
    vec = lambda g: g.astype(F32).reshape(1, d)
    ins = [x, o_rwkv, o_s5, o_attn, w_out.astype(BF16), vec(gain_post), vec(gain_pre)]
    in_specs = [row(d), row(o_rwkv.shape[1]), row(o_s5.shape[1]), row(o_attn.shape[1]),
                _const_spec(w_out.shape), _const_spec((1, d)), _const_spec((1, d))]
    out_specs = [row(d), row(d)]
    out_shape = [jax.ShapeDtypeStruct((t, d), F32), jax.ShapeDtypeStruct((t, d), BF16)]
    if router is not None:
        ins.append(jnp.pad(router.astype(F32), ((0, 0), (0, LANES - N_EXPERTS))))
        in_specs.append(_const_spec((d, LANES)))
        out_specs.append(row(LANES))
        out_shape.append(jax.ShapeDtypeStruct((t, LANES), F32))
    return pl.pallas_call(
        functools.partial(_outproj_kernel, router is not None),
        name="out_projection",
        grid=(t // ROW_TILE,),
        in_specs=in_specs, out_specs=out_specs, out_shape=out_shape,
        compiler_params=_params("parallel"),
    )(*ins)


def _ffn_kernel(routed, x_ref, h_ref, *refs):
    if routed:
        comb_ref, wg_ref, wu_ref, wd_ref, gain_ref, o_ref, acc_ref = refs
    else:
        wg_ref, wu_ref, wd_ref, gain_ref, o_ref, acc_ref = refs
    e, f = pl.program_id(1), pl.program_id(2)

    @pl.when((e == 0) & (f == 0))
    def _():
        acc_ref[...] = jnp.zeros_like(acc_ref)

    h = h_ref[...]
    act = jax.nn.silu(jnp.dot(h, wg_ref[0], preferred_element_type=F32)) * jnp.dot(h, wu_ref[0], preferred_element_type=F32)
    y = jnp.dot(act.astype(BF16), wd_ref[0], preferred_element_type=F32)
    if routed:
        comb = comb_ref[...]
        lane = lax.broadcasted_iota(jnp.int32, comb.shape, 1)
        y = y * jnp.sum(jnp.where(lane == e, comb, 0.0), axis=-1, keepdims=True)
    acc_ref[...] += y

    @pl.when((e == pl.num_programs(1) - 1) & (f == pl.num_programs(2) - 1))
    def _():
        o_ref[...] = x_ref[...] + _rms(acc_ref[...], gain_ref[...])


def _ffn(x, h, combine, w_gate, w_up, w_down, gain):
    t, d = x.shape
    n_e, _, d_ff = w_gate.shape
    routed = combine is not None
    row = lambda n: pl.BlockSpec((ROW_TILE, n), lambda i, e, f: (i, 0))
    ins = [x, h] + ([combine] if routed else [])
    in_specs = [row(d), row(d)] + ([row(LANES)] if routed else [])
    ins += [w_gate.astype(BF16), w_up.astype(BF16), w_down.astype(BF16), gain.astype(F32).reshape(1, d)]
    in_specs += [pl.BlockSpec((1, d, FFN_TILE), lambda i, e, f: (e, 0, f)),
                 pl.BlockSpec((1, d, FFN_TILE), lambda i, e, f: (e, 0, f)),
                 pl.BlockSpec((1, FFN_TILE, d), lambda i, e, f: (e, f, 0)),
                 pl.BlockSpec((1, d), lambda i, e, f: (0, 0))]
    return pl.pallas_call(
        functools.partial(_ffn_kernel, routed),
        name="swiglu_ffn",
        grid=(t // ROW_TILE, n_e, d_ff // FFN_TILE),
        in_specs=in_specs,
        out_specs=row(d),
        out_shape=jax.ShapeDtypeStruct((t, d), F32),
        scratch_shapes=[pltpu.VMEM((ROW_TILE, d), F32)],
        compiler_params=_params("parallel", "arbitrary", "arbitrary"),
    )(*ins)


def kernel(x, norm_mix_pre, norm_mix_post, norm_ffn_pre, norm_ffn_post, w_in, w_out, rwkv_mu, rwkv_w0, rwkv_w_up, rwkv_a0, rwkv_a_up, rwkv_g_up, rwkv_k_k, rwkv_k_a, rwkv_r_k, rwkv_ln_w, rwkv_ln_b, s5_a_re, s5_a_im, s5_log_step, s5_b_re, s5_b_im, s5_c_re, s5_c_im, s5_d, s5_w_glu, s5_b_glu, s5_norm, attn_rel_bias, attn_norm, ffn_w_gate, ffn_w_up, ffn_w_down, moe_router, moe_w_gate, moe_w_up, moe_w_down):
    b, s, d = x.shape
    depth = w_in.shape[0]
    xt = x.reshape(b * s, d)
    for layer in range(depth):
        p, u, qkv = _in_projection(xt, norm_mix_pre[layer], w_in[layer])
        o_rwkv = _rwkv_mix(p.reshape(b, s, -1), rwkv_mu[layer], rwkv_w0[layer], rwkv_w_up[layer], rwkv_a0[layer],
                           rwkv_a_up[layer], rwkv_g_up[layer], rwkv_k_k[layer], rwkv_k_a[layer], rwkv_r_k[layer],
                           rwkv_ln_w[layer], rwkv_ln_b[layer])
        o_s5 = _s5_mix(u.reshape(b, s, -1), s5_a_re[layer], s5_a_im[layer], s5_log_step[layer], s5_b_re[layer],
                       s5_b_im[layer], s5_c_re[layer], s5_c_im[layer], s5_d[layer], s5_w_glu[layer],
                       s5_b_glu[layer], s5_norm[layer])
        o_attn = _chunk_attention(qkv.reshape(b, s, -1), attn_rel_bias[layer], attn_norm[layer])
        i = layer // 2
        moe = layer % 2 == 1
        outs = _out_projection(xt, o_rwkv.reshape(b * s, -1), o_s5.reshape(b * s, -1), o_attn.reshape(b * s, -1),
                               w_out[layer], norm_mix_post[layer], norm_ffn_pre[layer],
                               moe_router[i] if moe else None)
        if moe:
            xt, h, combine = outs
            xt = _ffn(xt, h, combine, moe_w_gate[i], moe_w_up[i], moe_w_down[i], norm_ffn_post[layer])
        else:
            xt, h = outs
            xt = _ffn(xt, h, None, ffn_w_gate[i][None], ffn_w_up[i][None], ffn_w_down[i][None], norm_ffn_post[layer])
    return xt.reshape(b, s, d)
```

```python
import functools
import math

import jax
import jax.numpy as jnp
import numpy as np
from jax import lax
from jax.experimental import pallas as pl
from jax.experimental.pallas import tpu as pltpu

F32 = jnp.float32
BF16 = jnp.bfloat16

RMS_EPS = 1e-6
RWKV_GN_EPS = 64e-5
HEAD_DIM = 64
RWKV_HEADS = 6
RWKV_WIDTH = 384
RWKV_IN = 1408
S5_WIDTH = 256
S5_GROUPS = 16
S5_STATE = 64
S5_LANES = S5_GROUPS * S5_STATE
ATTN_HEADS = 6
ATTN_WIDTH = 384
ATTN_CHUNK = 64
ATTN_LEFT = 8
MAX_REL = 128
N_EXPERTS = 8
LANES = 128

RWKV_CHUNK = 64
S5_CHUNK = 128
ATTN_TILE = 2 * ATTN_CHUNK
ATTN_KEY_BLOCKS = (ATTN_LEFT * ATTN_CHUNK) // ATTN_TILE + 1
ROW_TILE = 512
FFN_TILE = 1408
VMEM_LIMIT = 48 * 1024 * 1024

_HI = lax.Precision.HIGHEST
_CONTRACT = {"nn": (((1,), (0,)), ((), ())), "nt": (((1,), (1,)), ((), ())), "tn": (((0,), (0,)), ((), ()))}


def _hdot(a, b):
    return jnp.dot(a, b, precision=_HI, preferred_element_type=F32)


def _terms(x, n):
    out = []
    for i in range(n):
        t = x.astype(BF16)
        out.append(t)
        if i + 1 < n:
            x = x - t.astype(F32)
    return tuple(out)


def _mm(a, b, kind="nn"):
    order = max(len(a), len(b))
    acc = None
    for i, ai in enumerate(a):
        for j, bj in enumerate(b):
            if i + j < order:
                d = lax.dot_general(ai, bj, _CONTRACT[kind], preferred_element_type=F32)
                acc = d if acc is None else acc + d
    return acc


def _rms(x, gain):
    return x * lax.rsqrt(jnp.mean(x * x, axis=-1, keepdims=True) + RMS_EPS) * gain


def _params(*semantics):
    return pltpu.CompilerParams(dimension_semantics=semantics, vmem_limit_bytes=VMEM_LIMIT)


def _const_spec(shape):
    zeros = (0,) * len(shape)
    return pl.BlockSpec(shape, lambda *_: zeros)


def _inproj_kernel(x_ref, gain_ref, w_ref, p_ref, u_ref, qkv_ref):
    h = _rms(x_ref[...], gain_ref[...]).astype(BF16)
    s5_at, qkv_at = RWKV_IN, RWKV_IN + S5_WIDTH
    p_ref[...] = jnp.dot(h, w_ref[:, :s5_at], preferred_element_type=F32)
    u_ref[...] = jnp.dot(h, w_ref[:, s5_at:qkv_at], preferred_element_type=F32)
    qkv_ref[...] = jnp.dot(h, w_ref[:, qkv_at:], preferred_element_type=F32).astype(BF16)


def _in_projection(x, gain, w_in):
    t, d = x.shape
    n_in = w_in.shape[1]
    n_qkv = n_in - RWKV_IN - S5_WIDTH
    row = lambda n: pl.BlockSpec((ROW_TILE, n), lambda i: (i, 0))
    return pl.pallas_call(
        _inproj_kernel,
        name="in_projection",
        grid=(t // ROW_TILE,),
        in_specs=[row(d), _const_spec((1, d)), _const_spec((d, n_in))],
        out_specs=[row(RWKV_IN), row(S5_WIDTH), row(n_qkv)],
        out_shape=[jax.ShapeDtypeStruct((t, RWKV_IN), F32), jax.ShapeDtypeStruct((t, S5_WIDTH), F32),
                   jax.ShapeDtypeStruct((t, n_qkv), BF16)],
        compiler_params=_params("parallel"),
    )(x, gain.reshape(1, d), w_in.astype(BF16))


RWKV_TERMS = 2
HEADS_PER_VREG = LANES // HEAD_DIM


def _t(x):
    return _terms(x, RWKV_TERMS)


def _head_sums(x):
    first = lax.broadcasted_iota(jnp.int32, (1, LANES), 1) < HEAD_DIM
    outs = []
    for j in range(x.shape[1] // LANES):
        xp = x[:, j * LANES:(j + 1) * LANES]
        s0 = jnp.sum(jnp.where(first, xp, 0.0), axis=-1, keepdims=True)
        s1 = jnp.sum(jnp.where(first, 0.0, xp), axis=-1, keepdims=True)
        outs.append(jnp.where(first, s0, s1))
    return jnp.concatenate(outs, axis=1)


def _rwkv_kernel(p_ref, mu_ref, w0_ref, wup_ref, a0_ref, aup_ref, gup_ref, kk_ref, ka_ref, rk_ref,
                 lnw_ref, lnb_ref, o_ref, prev_ref, state_ref):
    nb = p_ref.shape[0]
    n = RWKV_CHUNK
    rows = nb * n
    w_at = 3 * RWKV_WIDTH
    pairs = RWKV_WIDTH // LANES

    @pl.when(pl.program_id(0) == 0)
    def _():
        prev_ref[...] = jnp.zeros_like(prev_ref)
        state_ref[...] = jnp.zeros_like(state_ref)

    p = p_ref[...].reshape(rows, RWKV_IN)
    row = lax.broadcasted_iota(jnp.int32, (rows, 1), 0)
    shifted = pltpu.roll(p, 1, axis=0)
    for b in range(nb):
        shifted = jnp.where(row == b * n, prev_ref[b:b + 1, :], shifted)
    for b in range(nb):
        prev_ref[b:b + 1, :] = p[(b + 1) * n - 1:(b + 1) * n, :]
    p = p + (shifted - p) * mu_ref[...]

    r = p[:, :RWKV_WIDTH]
    k = p[:, RWKV_WIDTH:2 * RWKV_WIDTH]
    v = p[:, 2 * RWKV_WIDTH:w_at]
    w_lo = p[:, w_at:w_at + 64]
    a_lo = p[:, w_at + 64:w_at + 128]
    g_lo = p[:, w_at + 128:]

    lora = lambda x, w_ref: _mm(_t(x), tuple(w_ref[i] for i in range(RWKV_TERMS)))
    w = -jax.nn.softplus(-(w0_ref[...] + lora(jnp.tanh(w_lo), wup_ref))) - 0.5
    log_decay = -jnp.exp(w)
    a = jax.nn.sigmoid(a0_ref[...] + lora(a_lo, aup_ref))
    gate = lora(jax.nn.sigmoid(g_lo), gup_ref)

    kk = k * kk_ref[...]
    kk = kk / jnp.maximum(jnp.sqrt(_head_sums(kk * kk)), 1e-12)
    k = k * (1.0 + (a - 1.0) * ka_ref[...])

    s_i = lax.broadcasted_iota(jnp.int32, (rows, rows), 0)
    s_j = lax.broadcasted_iota(jnp.int32, (rows, rows), 1)
    causal = ((s_i // n == s_j // n) & (s_i >= s_j)).astype(BF16)
    cum = _mm((causal,), _terms(log_decay, 3))
    cum_end = cum[n - 1:n, :]
    for b in range(1, nb):
        cum_end = jnp.where(row >= b * n, cum[(b + 1) * n - 1:(b + 1) * n, :], cum_end)
    inv_g = jnp.exp(-cum)
    to_end = jnp.exp(cum_end - cum)
    beta = kk * a
    r_bar = r * jnp.exp(cum)
    a_bar = -kk * jnp.exp(cum - log_decay)
    b_til = beta * inv_g
    k_til = k * inv_g
    b_end = beta * to_end
    k_end = k * to_end
    g_end = jnp.exp(cum_end)

    lane = lax.broadcasted_iota(jnp.int32, (1, LANES), 1)
    q_i = lax.broadcasted_iota(jnp.int32, (n, LANES), 0)
    q_j = lax.broadcasted_iota(jnp.int32, (n, LANES), 1) % n
    lower = q_i >= q_j
    strictly_lower = q_i > q_j
    eye = (lax.broadcasted_iota(jnp.int32, (n, n), 0) == lax.broadcasted_iota(jnp.int32, (n, n), 1)).astype(F32)
    c_i = lax.broadcasted_iota(jnp.int32, (LANES, LANES), 0)
    c_j = lax.broadcasted_iota(jnp.int32, (LANES, LANES), 1)
    same_head = c_i // HEAD_DIM == c_j // HEAD_DIM
    diagonal = c_i == c_j
    zeros = jnp.zeros((n, LANES), F32)

    pair_ids = [(b, j) for b in range(nb) for j in range(pairs)]
    head_ids = [(pi, h) for pi in range(len(pair_ids)) for h in range(HEADS_PER_VREG)]
    heads_of = lambda pi: [hi for hi, (p_i, _) in enumerate(head_ids) if p_i == pi]
    cut = lambda x, pi: x[pair_ids[pi][0] * n:(pair_ids[pi][0] + 1) * n,
                          pair_ids[pi][1] * LANES:(pair_ids[pi][1] + 1) * LANES]
    rb = [cut(r_bar, pi) for pi in range(len(pair_ids))]
    be = [cut(b_end, pi) for pi in range(len(pair_ids))]
    vv = [cut(v, pi) for pi in range(len(pair_ids))]
    bk = [_t(jnp.concatenate([cut(b_til, pi), cut(k_til, pi)], axis=0)) for pi in range(len(pair_ids))]
    mine = [lane // HEAD_DIM == h for _, h in head_ids]
    abm = [jnp.where(mine[hi], cut(a_bar, pi), 0.0) for hi, (pi, _) in enumerate(head_ids)]
    v0 = [_t(jnp.concatenate([zeros, jnp.where(mine[hi], vv[pi], 0.0)], axis=0)) for hi, (pi, _) in enumerate(head_ids)]
    m = [_mm(_t(jnp.concatenate([abm[hi], jnp.where(mine[hi], rb[pi], 0.0)], axis=0)), bk[pi], "nt")
         for hi, (pi, _) in enumerate(head_ids)]
    top = [jnp.where(strictly_lower, x[:n], 0.0) for x in m]
    bot = [jnp.where(lower, x[n:], 0.0) for x in m]
    y = [_mm(_t(top[hi]), v0[hi]) for hi in range(len(head_ids))]
    power = [x[:, :n] for x in top]
    inv = [eye + x for x in power]
    for _ in range(int(math.log2(n)) - 1):
        pt = [_t(x) for x in power]
        power = [_mm(x, x) for x in pt]
        inv = [x + _mm(_t(x), _t(pw)) for x, pw in zip(inv, power)]
    wu = [_mm(_t(inv[hi]), _t(jnp.concatenate([abm[hi], y[hi]], axis=1))) for hi in range(len(head_ids))]
    v_low = [tuple(jnp.concatenate([jnp.zeros((n, LANES), BF16), t[n:]], axis=1) for t in v0[hi])
             for hi in range(len(head_ids))]
    ro = [_mm(_t(bot[hi]), tuple(jnp.concatenate([a, b2], axis=0) for a, b2 in zip(_t(wu[hi]), v_low[hi])))
          for hi in range(len(head_ids))]
    both = lambda xs, pi, sl: sum(xs[hi][:, sl] for hi in heads_of(pi))
    first, second = slice(0, LANES), slice(LANES, 2 * LANES)
    state = [_t(state_ref[pi]) for pi in range(len(pair_ids))]
    out = [_mm(_t(rb[pi] + both(ro, pi, first)), state[pi]) + both(ro, pi, second) for pi in range(len(pair_ids))]
    g_t = [jnp.where(same_head, _mm(_t(be[pi]), _t(both(wu, pi, first)), "tn"), 0.0)
           + jnp.where(diagonal, cut(g_end, pi)[:1], 0.0) for pi in range(len(pair_ids))]
    s2_t = [_mm(_t(jnp.concatenate([be[pi], cut(k_end, pi)], axis=0)),
                _t(jnp.concatenate([both(wu, pi, second), vv[pi]], axis=0)), "tn") for pi in range(len(pair_ids))]
    for pi in range(len(pair_ids)):
        state_ref[pi] = _mm(_t(g_t[pi]), state[pi]) + jnp.where(same_head, s2_t[pi], 0.0)
    o = jnp.concatenate([jnp.concatenate(out[b * pairs:(b + 1) * pairs], axis=1) for b in range(nb)], axis=0)

    inv_n = 1.0 / HEAD_DIM
    cen = o - _head_sums(o) * inv_n
    var = _head_sums(cen * cen) * inv_n
    o = cen * lax.rsqrt(var + RWKV_GN_EPS) * lnw_ref[...] + lnb_ref[...]
    o = o + _head_sums(r * k * rk_ref[...]) * v
    o_ref[...] = (o * gate).reshape(nb, n, RWKV_WIDTH)


def _rwkv_mix(p, mu, w0, w_up, a0, a_up, g_up, k_k, k_a, r_k, ln_w, ln_b):
    b, s, _ = p.shape
    vec = lambda t: t.astype(F32).reshape(1, -1)
    split = lambda t: jnp.stack(_terms(t.astype(F32), RWKV_TERMS))
    consts = [vec(mu), vec(w0), split(w_up), vec(a0), split(a_up), split(g_up), vec(k_k), vec(k_a), vec(r_k),
              vec(ln_w), vec(ln_b)]
    return pl.pallas_call(
        _rwkv_kernel,
        name="rwkv7",
        grid=(s // RWKV_CHUNK,),
        in_specs=[pl.BlockSpec((b, RWKV_CHUNK, RWKV_IN), lambda c: (0, c, 0))]
        + [_const_spec(t.shape) for t in consts],
        out_specs=pl.BlockSpec((b, RWKV_CHUNK, RWKV_WIDTH), lambda c: (0, c, 0)),
        out_shape=jax.ShapeDtypeStruct((b, s, RWKV_WIDTH), F32),
        scratch_shapes=[pltpu.VMEM((b, RWKV_IN), F32),
                        pltpu.VMEM((b * RWKV_WIDTH // LANES, LANES, LANES), F32)],
        compiler_params=_params("arbitrary"),
    )(p, *consts)


def _s5_prep_kernel(are_ref, aim_ref, lstep_ref, bre_ref, bim_ref, bmat_ref, pw_ref):
    lam_re = jnp.minimum(are_ref[...], -1e-4)
    lam_im = aim_ref[...]
    step = jnp.exp(lstep_ref[...])
    z_re, z_im = lam_re * step, lam_im * step
    mag = jnp.exp(z_re)
    num_re, num_im = mag * jnp.cos(z_im) - 1.0, mag * jnp.sin(z_im)
    den = lam_re * lam_re + lam_im * lam_im
    coef_re = (num_re * lam_re + num_im * lam_im) / den
    coef_im = (num_im * lam_re - num_re * lam_im) / den
    bre, bim = bre_ref[...], bim_ref[...]
    bmat_ref[:, :S5_LANES] = (coef_re * bre - coef_im * bim).astype(bmat_ref.dtype)
    bmat_ref[:, S5_LANES:] = (coef_re * bim + coef_im * bre).astype(bmat_ref.dtype)
    steps = (lax.broadcasted_iota(jnp.int32, (S5_CHUNK, 1), 0) + 1).astype(F32)
    pmag = jnp.exp(steps * z_re)
    pw_ref[:, :S5_LANES] = pmag * jnp.cos(steps * z_im)
    pw_ref[:, S5_LANES:] = pmag * jnp.sin(steps * z_im)


def _s5_kernel(u_ref, bmat_ref, pw_ref, cre_ref, cim_ref, d_ref, wglu_ref, bglu_ref, gain_ref, o_ref, carry_ref):
    n = S5_CHUNK

    @pl.when(pl.program_id(1) == 0)
    def _():
        carry_ref[...] = jnp.zeros_like(carry_ref)

    u = u_ref[0]
    bu = jnp.dot(u.astype(BF16), bmat_ref[...], preferred_element_type=F32)
    x_re, x_im = bu[:, :S5_LANES], bu[:, S5_LANES:]
    row = lax.broadcasted_iota(jnp.int32, (n, 1), 0)
    dist = 1
    while dist < n:
        l_re = pw_ref[dist - 1:dist, :S5_LANES]
        l_im = pw_ref[dist - 1:dist, S5_LANES:]
        keep = row >= dist
        s_re = jnp.where(keep, pltpu.roll(x_re, dist, axis=0), 0.0)
        s_im = jnp.where(keep, pltpu.roll(x_im, dist, axis=0), 0.0)
        x_re, x_im = x_re + l_re * s_re - l_im * s_im, x_im + l_re * s_im + l_im * s_re
        dist *= 2
    c_re, c_im = carry_ref[:, :S5_LANES], carry_ref[:, S5_LANES:]
    p_re, p_im = pw_ref[:, :S5_LANES], pw_ref[:, S5_LANES:]
    x_re, x_im = x_re + p_re * c_re - p_im * c_im, x_im + p_re * c_im + p_im * c_re
    carry_ref[:, :S5_LANES] = x_re[n - 1:n, :]
    carry_ref[:, S5_LANES:] = x_im[n - 1:n, :]

    y = (jnp.dot(x_re.astype(BF16), cre_ref[...], preferred_element_type=F32)
         - jnp.dot(x_im.astype(BF16), cim_ref[...], preferred_element_type=F32)) + d_ref[...] * u
    y = jax.nn.gelu(y)
    y = y * jax.nn.sigmoid(jnp.dot(y.astype(BF16), wglu_ref[...], preferred_element_type=F32) + bglu_ref[...])
    o_ref[0] = _rms(y, gain_ref[...])


def _block_diag(t):
    g, rows, cols = t.shape
    return jnp.einsum('grc,gh->grhc', t, jnp.eye(g, dtype=t.dtype)).reshape(g * rows, g * cols)


def _s5_mix(u, a_re, a_im, log_step, b_re, b_im, c_re, c_im, d, w_glu, b_glu, gain):
    b, s, width = u.shape
    lane = lambda t: t.astype(F32).reshape(1, S5_LANES)
    to_cp = lambda t: _block_diag(jnp.swapaxes(t.astype(F32), 1, 2))
    bmat, pw = pl.pallas_call(
        _s5_prep_kernel,
        name="s5_discretise",
        out_shape=[jax.ShapeDtypeStruct((width, 2 * S5_LANES), BF16),
                   jax.ShapeDtypeStruct((S5_CHUNK, 2 * S5_LANES), F32)],
        compiler_params=pltpu.CompilerParams(vmem_limit_bytes=VMEM_LIMIT),
    )(lane(a_re), lane(a_im), lane(jnp.repeat(log_step[:, None], S5_STATE, axis=1)), to_cp(b_re), to_cp(b_im))
    cre = _block_diag(jnp.swapaxes(c_re, 1, 2)).astype(BF16)
    cim = _block_diag(jnp.swapaxes(c_im, 1, 2)).astype(BF16)
    vec = lambda t: t.astype(F32).reshape(1, width)
    consts = [bmat, pw, cre, cim, vec(d), w_glu.astype(BF16), vec(b_glu), vec(gain)]
    return pl.pallas_call(
        _s5_kernel,
        name="s5_scan",
        grid=(b, s // S5_CHUNK),
        in_specs=[pl.BlockSpec((1, S5_CHUNK, width), lambda i, c: (i, c, 0))]
        + [_const_spec(t.shape) for t in consts],
        out_specs=pl.BlockSpec((1, S5_CHUNK, width), lambda i, c: (i, c, 0)),
        out_shape=jax.ShapeDtypeStruct((b, s, width), F32),
        scratch_shapes=[pltpu.VMEM((1, 2 * S5_LANES), F32)],
        compiler_params=_params("parallel", "arbitrary"),
    )(u, *consts)


def _attn_kernel(q_ref, *refs):
    k_refs = refs[:ATTN_KEY_BLOCKS]
    v_refs = refs[ATTN_KEY_BLOCKS:2 * ATTN_KEY_BLOCKS]
    bias_ref, gain_ref, o_ref = refs[2 * ATTN_KEY_BLOCKS:]
    n_keys = ATTN_KEY_BLOCKS * ATTN_TILE
    q = q_ref[0]
    k = jnp.concatenate([kr[0] for kr in k_refs], axis=0)
    v = jnp.concatenate([vr[0] for vr in v_refs], axis=0)
    key_row = pl.program_id(1) * ATTN_TILE + lax.broadcasted_iota(jnp.int32, (1, n_keys), 1)
    before_start = key_row < ATTN_LEFT * ATTN_CHUNK
    neg = jnp.finfo(F32).min
    outs = []
    for h in range(ATTN_HEADS):
        sl = slice(h * HEAD_DIM, (h + 1) * HEAD_DIM)
        s = lax.dot_general(q[:, sl], k[:, sl], (((1,), (1,)), ((), ())), preferred_element_type=F32)
        bias = bias_ref[h]
        s = jnp.where((bias == neg) | before_start, neg, s * (HEAD_DIM ** -0.5) + bias)
        e = jnp.exp(s - jnp.max(s, axis=-1, keepdims=True))
        pv = jnp.dot(e.astype(BF16), v[:, sl], preferred_element_type=F32)
        outs.append(pv / jnp.sum(e, axis=-1, keepdims=True))
    o_ref[0] = _rms(jnp.concatenate(outs, axis=1), gain_ref[...])


def _attn_bias_table(rel_bias):
    n_keys = ATTN_KEY_BLOCKS * ATTN_TILE
    left = ATTN_LEFT * ATTN_CHUNK
    period = n_keys + ATTN_TILE
    rb = rel_bias.astype(F32)
    heads = rb.shape[0]
    far = lambda width: jnp.broadcast_to(rb[:, 2 * MAX_REL:], (heads, width))
    g = jnp.concatenate([far(left - MAX_REL), rb[:, 1:][:, ::-1], far(period - left - MAX_REL)], axis=1)
    toeplitz = jnp.tile(g, (1, ATTN_TILE))[:, :ATTN_TILE * (period - 1)].reshape(heads, ATTN_TILE, period - 1)
    q = np.arange(ATTN_TILE)[:, None]
    j = np.arange(n_keys)[None, :]
    in_band = (j // ATTN_CHUNK >= q // ATTN_CHUNK) & (j // ATTN_CHUNK <= q // ATTN_CHUNK + ATTN_LEFT)
    return jnp.where(in_band[None], toeplitz[:, :, :n_keys], jnp.finfo(F32).min)


def _chunk_attention(qkv, rel_bias, gain):
    b, s, _ = qkv.shape
    pad = ((0, 0), (ATTN_LEFT * ATTN_CHUNK, 0), (0, 0))
    kp = jnp.pad(qkv[:, :, ATTN_WIDTH:2 * ATTN_WIDTH], pad)
    vp = jnp.pad(qkv[:, :, 2 * ATTN_WIDTH:], pad)
    bias = _attn_bias_table(rel_bias)
    key_specs = [pl.BlockSpec((1, ATTN_TILE, ATTN_WIDTH), lambda i, c, j=j: (i, c + j, 0))
                 for j in range(ATTN_KEY_BLOCKS)]
    return pl.pallas_call(
        _attn_kernel,
        name="chunk_attention",
        grid=(b, s // ATTN_TILE),
        in_specs=[pl.BlockSpec((1, ATTN_TILE, ATTN_WIDTH), lambda i, c: (i, c, 0))] + key_specs + key_specs
        + [_const_spec(bias.shape), _const_spec((1, ATTN_WIDTH))],
        out_specs=pl.BlockSpec((1, ATTN_TILE, ATTN_WIDTH), lambda i, c: (i, c, 0)),
        out_shape=jax.ShapeDtypeStruct((b, s, ATTN_WIDTH), F32),
        compiler_params=_params("parallel", "parallel"),
    )(qkv, *([kp] * ATTN_KEY_BLOCKS), *([vp] * ATTN_KEY_BLOCKS), bias, gain.astype(F32).reshape(1, ATTN_WIDTH))


def _outproj_kernel(route, x_ref, orw_ref, os5_ref, oat_ref, w_ref, gpost_ref, gpre_ref, *refs):
    mixed = jnp.concatenate([orw_ref[...], os5_ref[...], oat_ref[...]], axis=1).astype(BF16)
    y = jnp.dot(mixed, w_ref[...], preferred_element_type=F32)
    x = x_ref[...] + _rms(y, gpost_ref[...])
    h = _rms(x, gpre_ref[...])
    if not route:
        x_out, h_out = refs
    else:
        router_ref, x_out, h_out, route_out = refs
        logits = _hdot(h, router_ref[...])
        lane = lax.broadcasted_iota(jnp.int32, logits.shape, 1)
        neg = jnp.finfo(F32).min
        logits = jnp.where(lane < N_EXPERTS, logits, neg)
        top1 = jnp.max(logits, axis=-1, keepdims=True)
        idx1 = jnp.min(jnp.where(logits == top1, lane, LANES), axis=-1, keepdims=True)
        rest = jnp.where(lane == idx1, neg, logits)
        top2 = jnp.max(rest, axis=-1, keepdims=True)
        idx2 = jnp.min(jnp.where(rest == top2, lane, LANES), axis=-1, keepdims=True)
        e2 = jnp.exp(top2 - top1)
        route_out[...] = (jnp.where(lane == 0, idx1.astype(F32), 0.0) + jnp.where(lane == 1, idx2.astype(F32), 0.0)
                          + jnp.where(lane == 2, 1.0 / (1.0 + e2), 0.0) + jnp.where(lane == 3, e2 / (1.0 + e2), 0.0))
    x_out[...] = x
    h_out[...] = h.astype(h_out.dtype)


def _out_projection(x, o_rwkv, o_s5, o_attn, w_out, gain_post, gain_pre, router):
    t, d = x.shape
    row = lambda n: pl.BlockSpec((ROW_TILE, n), lambda i: (i, 0))
    vec = lambda g: g.astype(F32).reshape(1, d)
    ins = [x, o_rwkv, o_s5, o_attn, w_out.astype(BF16), vec(gain_post), vec(gain_pre)]
    in_specs = [row(d), row(o_rwkv.shape[1]), row(o_s5.shape[1]), row(o_attn.shape[1]),
                _const_spec(w_out.shape), _const_spec((1, d)), _const_spec((1, d))]
    out_specs = [row(d), row(d)]
    out_shape = [jax.ShapeDtypeStruct((t, d), F32), jax.ShapeDtypeStruct((t, d), BF16 if router is None else F32)]
    if router is not None:
        ins.append(jnp.pad(router.astype(F32), ((0, 0), (0, LANES - N_EXPERTS))))
        in_specs.append(_const_spec((d, LANES)))
        out_specs.append(row(LANES))
        out_shape.append(jax.ShapeDtypeStruct((t, LANES), F32))
    return pl.pallas_call(
        functools.partial(_outproj_kernel, router is not None),
        name="out_projection",
        grid=(t // ROW_TILE,),
        in_specs=in_specs, out_specs=out_specs, out_shape=out_shape,
        compiler_params=_params("parallel"),
    )(*ins)


def _ffn_kernel(x_ref, h_ref, wg_ref, wu_ref, wd_ref, gain_ref, o_ref, acc_ref):
    f = pl.program_id(1)

    @pl.when(f == 0)
    def _():
        acc_ref[...] = jnp.zeros_like(acc_ref)

    h = h_ref[...]
    act = jax.nn.silu(jnp.dot(h, wg_ref[...], preferred_element_type=F32)) * jnp.dot(h, wu_ref[...], preferred_element_type=F32)
    acc_ref[...] += jnp.dot(act.astype(BF16), wd_ref[...], preferred_element_type=F32)

    @pl.when(f == pl.num_programs(1) - 1)
    def _():
        o_ref[...] = x_ref[...] + _rms(acc_ref[...], gain_ref[...])


def _ffn(x, h, w_gate, w_up, w_down, gain):
    t, d = x.shape
    d_ff = w_gate.shape[1]
    row = lambda n: pl.BlockSpec((ROW_TILE, n), lambda i, f: (i, 0))
    return pl.pallas_call(
        _ffn_kernel,
        name="swiglu_ffn",
        grid=(t // ROW_TILE, d_ff // FFN_TILE),
        in_specs=[row(d), row(d),
                  pl.BlockSpec((d, FFN_TILE), lambda i, f: (0, f)),
                  pl.BlockSpec((d, FFN_TILE), lambda i, f: (0, f)),
                  pl.BlockSpec((FFN_TILE, d), lambda i, f: (f, 0)),
                  pl.BlockSpec((1, d), lambda i, f: (0, 0))],
        out_specs=row(d),
        out_shape=jax.ShapeDtypeStruct((t, d), F32),
        scratch_shapes=[pltpu.VMEM((ROW_TILE, d), F32)],
        compiler_params=_params("parallel", "arbitrary"),
    )(x, h, w_gate.astype(BF16), w_up.astype(BF16), w_down.astype(BF16), gain.astype(F32).reshape(1, d))


MOE_TILE = 512


def _moe_plan(route, n_experts):
    t = route.shape[0]
    n = 2 * t
    tm = MOE_TILE
    expert = route[:, :2].astype(jnp.int32).reshape(n)
    gates = route[:, 2:4].reshape(n)
    onehot = (expert[:, None] == jnp.arange(n_experts, dtype=jnp.int32)[None, :]).astype(jnp.int32)
    rank = jnp.sum(jnp.cumsum(onehot, axis=0) * onehot, axis=1) - 1
    counts = jnp.sum(onehot, axis=0)
    padded = (counts + tm - 1) // tm * tm
    ends = jnp.cumsum(padded)
    slot = jnp.sum((ends - padded)[None, :] * onehot, axis=1) + rank
    n_tiles = n // tm + n_experts
    slots = n_tiles * tm
    ids = jnp.arange(n, dtype=jnp.int32)
    every = jnp.arange(slots, dtype=jnp.int32)
    put = lambda base, vals: base.at[slot].set(vals, unique_indices=True)
    tok = put(jnp.zeros((slots,), jnp.int32), ids // 2)
    dst = put(n + every % tm + tm * ((every // tm) % 2), ids)
    gate = put(jnp.zeros((slots,), F32), gates)
    tile_start = jnp.arange(n_tiles, dtype=jnp.int32) * tm
    tile_expert = jnp.minimum(jnp.sum((tile_start[:, None] >= ends[None, :]).astype(jnp.int32), axis=1), n_experts - 1)
    tile_valid = (tile_start < ends[-1]).astype(jnp.int32)
    return tok.reshape(n_tiles, 1, tm), dst.reshape(n_tiles, 1, tm), gate.reshape(slots, 1), tile_expert, tile_valid


def _moe_ffn_kernel(te_ref, tv_ref, tok0_ref, tokn_ref, dst_ref, gate_ref, h_hbm, wg_ref, wu_ref, wd_ref,
                    out_hbm, xbuf, xb, ybuf, acc_ref, gsem, ssem):
    t, f = pl.program_id(0), pl.program_id(1)
    n_t, n_f = pl.num_programs(0), pl.num_programs(1)
    tm = MOE_TILE
    slot = t % 2

    def start_gather(tok_ref, s):
        def body(i, carry):
            pltpu.make_async_copy(h_hbm.at[pl.ds(tok_ref[0, 0, i], 1), :], xbuf.at[s, pl.ds(i, 1), :], gsem.at[s]).start()
            return carry
        lax.fori_loop(0, tm, body, 0, unroll=8)

    def wait_scatter(s):
        pltpu.make_async_copy(ybuf.at[s], out_hbm.at[pl.ds(0, tm), :], ssem.at[s]).wait()

    @pl.when(f == 0)
    def _():
        @pl.when(t == 0)
        def _():
            start_gather(tok0_ref, 0)

        pltpu.make_async_copy(h_hbm.at[pl.ds(0, tm), :], xbuf.at[slot], gsem.at[slot]).wait()

        @pl.when(t + 1 < n_t)
        def _():
            start_gather(tokn_ref, 1 - slot)

        xb[...] = xbuf[slot].astype(BF16)
        acc_ref[...] = jnp.zeros_like(acc_ref)

    @pl.when(tv_ref[t] == 1)
    def _():
        x = xb[...]
        act = jax.nn.silu(jnp.dot(x, wg_ref[0], preferred_element_type=F32)) * jnp.dot(x, wu_ref[0], preferred_element_type=F32)
        acc_ref[...] += jnp.dot(act.astype(BF16), wd_ref[0], preferred_element_type=F32)

    @pl.when(f == n_f - 1)
    def _():
        @pl.when(t >= 2)
        def _():
            wait_scatter(slot)

        ybuf[slot] = acc_ref[...] * gate_ref[...]

        def body(i, carry):
            pltpu.make_async_copy(ybuf.at[slot, pl.ds(i, 1), :], out_hbm.at[pl.ds(dst_ref[0, 0, i], 1), :], ssem.at[slot]).start()
            return carry
        lax.fori_loop(0, tm, body, 0, unroll=8)

        @pl.when(t == n_t - 1)
        def _():
            wait_scatter(slot)
            wait_scatter(1 - slot)


def _moe_combine_kernel(x_ref, y_ref, gain_ref, o_ref):
    d = x_ref.shape[1]
    o_ref[...] = x_ref[...] + _rms(y_ref[:, :d] + y_ref[:, d:], gain_ref[...])


def _moe(x, h, route, w_gate, w_up, w_down, gain):
    t, d = x.shape
    n_e, _, d_ff = w_gate.shape
    tm = MOE_TILE
    tok, dst, gate, tile_expert, tile_valid = _moe_plan(route, n_e)
    n_tiles = tok.shape[0]
    idx_spec = lambda shift: pl.BlockSpec((1, 1, tm), lambda i, f, te, tv: (jnp.minimum(i + shift, n_tiles - 1), 0, 0),
                                          memory_space=pltpu.SMEM)
    grid_spec = pltpu.PrefetchScalarGridSpec(
        num_scalar_prefetch=2,
        grid=(n_tiles, d_ff // FFN_TILE),
        in_specs=[idx_spec(0), idx_spec(1), idx_spec(0),
                  pl.BlockSpec((tm, 1), lambda i, f, te, tv: (i, 0)),
                  pl.BlockSpec(memory_space=pl.ANY),
                  pl.BlockSpec((1, d, FFN_TILE), lambda i, f, te, tv: (te[i], 0, f)),
                  pl.BlockSpec((1, d, FFN_TILE), lambda i, f, te, tv: (te[i], 0, f)),
                  pl.BlockSpec((1, FFN_TILE, d), lambda i, f, te, tv: (te[i], f, 0))],
        out_specs=pl.BlockSpec(memory_space=pl.ANY),
        scratch_shapes=[pltpu.VMEM((2, tm, d), F32), pltpu.VMEM((tm, d), BF16), pltpu.VMEM((2, tm, d), F32),
                        pltpu.VMEM((tm, d), F32), pltpu.SemaphoreType.DMA((2,)), pltpu.SemaphoreType.DMA((2,))],
    )
    routed = pl.pallas_call(
        _moe_ffn_kernel,
        name="moe_experts",
        grid_spec=grid_spec,
        out_shape=jax.ShapeDtypeStruct((2 * t + 2 * tm, d), F32),
        compiler_params=_params("arbitrary", "arbitrary"),
    )(tile_expert, tile_valid, tok, tok, dst, gate, h, w_gate.astype(BF16), w_up.astype(BF16), w_down.astype(BF16))
    pairs = routed.reshape(t + tm, 2 * d)
    row = lambda n: pl.BlockSpec((ROW_TILE, n), lambda i: (i, 0))
    return pl.pallas_call(
        _moe_combine_kernel,
        name="moe_combine",
        grid=(t // ROW_TILE,),
        in_specs=[row(d), row(2 * d), _const_spec((1, d))],
        out_specs=row(d),
        out_shape=jax.ShapeDtypeStruct((t, d), F32),
        compiler_params=_params("parallel"),
    )(x, pairs, gain.astype(F32).reshape(1, d))


def kernel(x, norm_mix_pre, norm_mix_post, norm_ffn_pre, norm_ffn_post, w_in, w_out, rwkv_mu, rwkv_w0, rwkv_w_up, rwkv_a0, rwkv_a_up, rwkv_g_up, rwkv_k_k, rwkv_k_a, rwkv_r_k, rwkv_ln_w, rwkv_ln_b, s5_a_re, s5_a_im, s5_log_step, s5_b_re, s5_b_im, s5_c_re, s5_c_im, s5_d, s5_w_glu, s5_b_glu, s5_norm, attn_rel_bias, attn_norm, ffn_w_gate, ffn_w_up, ffn_w_down, moe_router, moe_w_gate, moe_w_up, moe_w_down):
    b, s, d = x.shape
    depth = w_in.shape[0]
    xt = x.reshape(b * s, d)
    for layer in range(depth):
        p, u, qkv = _in_projection(xt, norm_mix_pre[layer], w_in[layer])
        o_rwkv = _rwkv_mix(p.reshape(b, s, -1), rwkv_mu[layer], rwkv_w0[layer], rwkv_w_up[layer], rwkv_a0[layer],
                           rwkv_a_up[layer], rwkv_g_up[layer], rwkv_k_k[layer], rwkv_k_a[layer], rwkv_r_k[layer],
                           rwkv_ln_w[layer], rwkv_ln_b[layer])
        o_s5 = _s5_mix(u.reshape(b, s, -1), s5_a_re[layer], s5_a_im[layer], s5_log_step[layer], s5_b_re[layer],
                       s5_b_im[layer], s5_c_re[layer], s5_c_im[layer], s5_d[layer], s5_w_glu[layer],
                       s5_b_glu[layer], s5_norm[layer])
        o_attn = _chunk_attention(qkv.reshape(b, s, -1), attn_rel_bias[layer], attn_norm[layer])
        i = layer // 2
        moe = layer % 2 == 1
        outs = _out_projection(xt, o_rwkv.reshape(b * s, -1), o_s5.reshape(b * s, -1), o_attn.reshape(b * s, -1),
                               w_out[layer], norm_mix_post[layer], norm_ffn_pre[layer],
                               moe_router[i] if moe else None)
        if moe:
            xt, h, route = outs
            xt = _moe(xt, h, route, moe_w_gate[i], moe_w_up[i], moe_w_down[i], norm_ffn_post[layer])
        else:
            xt, h = outs
            xt = _ffn(xt, h, ffn_w_gate[i], ffn_w_up[i], ffn_w_down[i], norm_ffn_post[layer])
    return xt.reshape(b, s, d)
```

```python
import functools
import math

import jax
import jax.numpy as jnp
import numpy as np
from jax import lax
from jax.experimental import pallas as pl
from jax.experimental.pallas import tpu as pltpu

F32 = jnp.float32
BF16 = jnp.bfloat16

RMS_EPS = 1e-6
RWKV_GN_EPS = 64e-5
HEAD_DIM = 64
RWKV_HEADS = 6
RWKV_WIDTH = 384
RWKV_IN = 1408
S5_WIDTH = 256
S5_GROUPS = 16
S5_STATE = 64
S5_LANES = S5_GROUPS * S5_STATE
ATTN_HEADS = 6
ATTN_WIDTH = 384
ATTN_CHUNK = 64
ATTN_LEFT = 8
MAX_REL = 128
N_EXPERTS = 8
LANES = 128

RWKV_CHUNK = 64
S5_CHUNK = 256
SUBLANES = 8
ATTN_TILE = 2 * ATTN_CHUNK
ATTN_KEY_BLOCKS = (ATTN_LEFT * ATTN_CHUNK) // ATTN_TILE + 1
ROW_TILE = 512
FFN_TILE = 1408
VMEM_LIMIT = 48 * 1024 * 1024

_HI = lax.Precision.HIGHEST
_CONTRACT = {"nn": (((1,), (0,)), ((), ())), "nt": (((1,), (1,)), ((), ())), "tn": (((0,), (0,)), ((), ()))}


def _hdot(a, b):
    return jnp.dot(a, b, precision=_HI, preferred_element_type=F32)


def _terms(x, n):
    out = []
    for i in range(n):
        t = x.astype(BF16)
        out.append(t)
        if i + 1 < n:
            x = x - t.astype(F32)
    return tuple(out)


def _mm(a, b, kind="nn"):
    order = max(len(a), len(b))
    acc = None
    for i, ai in enumerate(a):
        for j, bj in enumerate(b):
            if i + j < order:
                d = lax.dot_general(ai, bj, _CONTRACT[kind], preferred_element_type=F32)
                acc = d if acc is None else acc + d
    return acc


def _rms(x, gain):
    return x * lax.rsqrt(jnp.mean(x * x, axis=-1, keepdims=True) + RMS_EPS) * gain


def _params(*semantics):
    return pltpu.CompilerParams(dimension_semantics=semantics, vmem_limit_bytes=VMEM_LIMIT)


def _const_spec(shape):
    zeros = (0,) * len(shape)
    return pl.BlockSpec(shape, lambda *_: zeros)


def _inproj_kernel(x_ref, gain_ref, w_ref, p_ref, u_ref, qkv_ref):
    h = _rms(x_ref[...], gain_ref[...]).astype(BF16)
    s5_at, qkv_at = RWKV_IN, RWKV_IN + S5_WIDTH
    p_ref[...] = jnp.dot(h, w_ref[:, :s5_at], preferred_element_type=F32)
    u_ref[...] = jnp.dot(h, w_ref[:, s5_at:qkv_at], preferred_element_type=F32)
    qkv_ref[...] = jnp.dot(h, w_ref[:, qkv_at:], preferred_element_type=F32).astype(BF16)


def _in_projection(x, gain, w_in):
    t, d = x.shape
    n_in = w_in.shape[1]
    n_qkv = n_in - RWKV_IN - S5_WIDTH
    row = lambda n: pl.BlockSpec((ROW_TILE, n), lambda i: (i, 0))
    return pl.pallas_call(
        _inproj_kernel,
        name="in_projection",
        grid=(t // ROW_TILE,),
        in_specs=[row(d), _const_spec((1, d)), _const_spec((d, n_in))],
        out_specs=[row(RWKV_IN), row(S5_WIDTH), row(n_qkv)],
        out_shape=[jax.ShapeDtypeStruct((t, RWKV_IN), F32), jax.ShapeDtypeStruct((t, S5_WIDTH), F32),
                   jax.ShapeDtypeStruct((t, n_qkv), BF16)],
        compiler_params=_params("parallel"),
    )(x, gain.reshape(1, d), w_in.astype(BF16))


RWKV_TERMS = 1
HEADS_PER_VREG = LANES // HEAD_DIM


def _t(x):
    return _terms(x, RWKV_TERMS)


def _head_sums(x):
    first = lax.broadcasted_iota(jnp.int32, (1, LANES), 1) < HEAD_DIM
    outs = []
    for j in range(x.shape[1] // LANES):
        xp = x[:, j * LANES:(j + 1) * LANES]
        s0 = jnp.sum(jnp.where(first, xp, 0.0), axis=-1, keepdims=True)
        s1 = jnp.sum(jnp.where(first, 0.0, xp), axis=-1, keepdims=True)
        outs.append(jnp.where(first, s0, s1))
    return jnp.concatenate(outs, axis=1)


def _rwkv_kernel(p_ref, mu_ref, w0_ref, wup_ref, a0_ref, aup_ref, gup_ref, kk_ref, ka_ref, rk_ref,
                 lnw_ref, lnb_ref, o_ref, prev_ref, state_ref):
    nb = p_ref.shape[0]
    n = RWKV_CHUNK
    rows = nb * n
    w_at = 3 * RWKV_WIDTH
    pairs = RWKV_WIDTH // LANES

    @pl.when(pl.program_id(0) == 0)
    def _():
        prev_ref[...] = jnp.zeros_like(prev_ref)
        state_ref[...] = jnp.zeros_like(state_ref)

    p = p_ref[...].reshape(rows, RWKV_IN)
    row = lax.broadcasted_iota(jnp.int32, (rows, 1), 0)
    shifted = pltpu.roll(p, 1, axis=0)
    for b in range(nb):
        shifted = jnp.where(row == b * n, prev_ref[b:b + 1, :], shifted)
    for b in range(nb):
        prev_ref[b:b + 1, :] = p[(b + 1) * n - 1:(b + 1) * n, :]
    p = p + (shifted - p) * mu_ref[...]

    r = p[:, :RWKV_WIDTH]
    k = p[:, RWKV_WIDTH:2 * RWKV_WIDTH]
    v = p[:, 2 * RWKV_WIDTH:w_at]
    w_lo = p[:, w_at:w_at + 64]
    a_lo = p[:, w_at + 64:w_at + 128]
    g_lo = p[:, w_at + 128:]

    lora = lambda x, w_ref: _mm(_t(x), tuple(w_ref[i] for i in range(RWKV_TERMS)))
    w = -jax.nn.softplus(-(w0_ref[...] + lora(jnp.tanh(w_lo), wup_ref))) - 0.5
    log_decay = -jnp.exp(w)
    a = jax.nn.sigmoid(a0_ref[...] + lora(a_lo, aup_ref))
    gate = lora(jax.nn.sigmoid(g_lo), gup_ref)

    kk = k * kk_ref[...]
    kk = kk / jnp.maximum(jnp.sqrt(_head_sums(kk * kk)), 1e-12)
    k = k * (1.0 + (a - 1.0) * ka_ref[...])

    s_i = lax.broadcasted_iota(jnp.int32, (rows, rows), 0)
    s_j = lax.broadcasted_iota(jnp.int32, (rows, rows), 1)
    causal = ((s_i // n == s_j // n) & (s_i >= s_j)).astype(BF16)
    cum = _mm((causal,), _terms(log_decay, 3))
    cum_end = cum[n - 1:n, :]
    for b in range(1, nb):
        cum_end = jnp.where(row >= b * n, cum[(b + 1) * n - 1:(b + 1) * n, :], cum_end)
    inv_g = jnp.exp(-cum)
    to_end = jnp.exp(cum_end - cum)
    beta = kk * a
    r_bar = r * jnp.exp(cum)
    a_bar = -kk * jnp.exp(cum - log_decay)
    b_til = beta * inv_g
    k_til = k * inv_g
    b_end = beta * to_end
    k_end = k * to_end
    g_end = jnp.exp(cum_end)

    lane = lax.broadcasted_iota(jnp.int32, (1, LANES), 1)
    q_i = lax.broadcasted_iota(jnp.int32, (n, LANES), 0)
    q_j = lax.broadcasted_iota(jnp.int32, (n, LANES), 1) % n
    lower = q_i >= q_j
    strictly_lower = q_i > q_j
    eye = (lax.broadcasted_iota(jnp.int32, (n, n), 0) == lax.broadcasted_iota(jnp.int32, (n, n), 1)).astype(F32)
    c_i = lax.broadcasted_iota(jnp.int32, (LANES, LANES), 0)
    c_j = lax.broadcasted_iota(jnp.int32, (LANES, LANES), 1)
    same_head = c_i // HEAD_DIM == c_j // HEAD_DIM
    diagonal = c_i == c_j
    zeros = jnp.zeros((n, LANES), F32)

    pair_ids = [(b, j) for b in range(nb) for j in range(pairs)]
    head_ids = [(pi, h) for pi in range(len(pair_ids)) for h in range(HEADS_PER_VREG)]
    heads_of = lambda pi: [hi for hi, (p_i, _) in enumerate(head_ids) if p_i == pi]
    cut = lambda x, pi: x[pair_ids[pi][0] * n:(pair_ids[pi][0] + 1) * n,
                          pair_ids[pi][1] * LANES:(pair_ids[pi][1] + 1) * LANES]
    rb = [cut(r_bar, pi) for pi in range(len(pair_ids))]
    be = [cut(b_end, pi) for pi in range(len(pair_ids))]
    vv = [cut(v, pi) for pi in range(len(pair_ids))]
    bk = [_t(jnp.concatenate([cut(b_til, pi), cut(k_til, pi)], axis=0)) for pi in range(len(pair_ids))]
    mine = [lane // HEAD_DIM == h for _, h in head_ids]
    abm = [jnp.where(mine[hi], cut(a_bar, pi), 0.0) for hi, (pi, _) in enumerate(head_ids)]
    v0 = [_t(jnp.concatenate([zeros, jnp.where(mine[hi], vv[pi], 0.0)], axis=0)) for hi, (pi, _) in enumerate(head_ids)]
    m = [_mm(_t(jnp.concatenate([abm[hi], jnp.where(mine[hi], rb[pi], 0.0)], axis=0)), bk[pi], "nt")
         for hi, (pi, _) in enumerate(head_ids)]
    top = [jnp.where(strictly_lower, x[:n], 0.0) for x in m]
    bot = [jnp.where(lower, x[n:], 0.0) for x in m]
    y = [_mm(_t(top[hi]), v0[hi]) for hi in range(len(head_ids))]
    power = [x[:, :n] for x in top]
    inv = [eye + x for x in power]
    for _ in range(int(math.log2(n)) - 1):
        pt = [_t(x) for x in power]
        power = [_mm(x, x) for x in pt]
        inv = [x + _mm(_t(x), _t(pw)) for x, pw in zip(inv, power)]
    wu = [_mm(_t(inv[hi]), _t(jnp.concatenate([abm[hi], y[hi]], axis=1))) for hi in range(len(head_ids))]
    v_low = [tuple(jnp.concatenate([jnp.zeros((n, LANES), BF16), t[n:]], axis=1) for t in v0[hi])
             for hi in range(len(head_ids))]
    ro = [_mm(_t(bot[hi]), tuple(jnp.concatenate([a, b2], axis=0) for a, b2 in zip(_t(wu[hi]), v_low[hi])))
          for hi in range(len(head_ids))]
    both = lambda xs, pi, sl: sum(xs[hi][:, sl] for hi in heads_of(pi))
    first, second = slice(0, LANES), slice(LANES, 2 * LANES)
    state = [_t(state_ref[pi]) for pi in range(len(pair_ids))]
    out = [_mm(_t(rb[pi] + both(ro, pi, first)), state[pi]) + both(ro, pi, second) for pi in range(len(pair_ids))]
    g_t = [jnp.where(same_head, _mm(_t(be[pi]), _t(both(wu, pi, first)), "tn"), 0.0)
           + jnp.where(diagonal, cut(g_end, pi)[:1], 0.0) for pi in range(len(pair_ids))]
    s2_t = [_mm(_t(jnp.concatenate([be[pi], cut(k_end, pi)], axis=0)),
                _t(jnp.concatenate([both(wu, pi, second), vv[pi]], axis=0)), "tn") for pi in range(len(pair_ids))]
    for pi in range(len(pair_ids)):
        state_ref[pi] = _mm(_t(g_t[pi]), state[pi]) + jnp.where(same_head, s2_t[pi], 0.0)
    o = jnp.concatenate([jnp.concatenate(out[b * pairs:(b + 1) * pairs], axis=1) for b in range(nb)], axis=0)

    inv_n = 1.0 / HEAD_DIM
    cen = o - _head_sums(o) * inv_n
    var = _head_sums(cen * cen) * inv_n
    o = cen * lax.rsqrt(var + RWKV_GN_EPS) * lnw_ref[...] + lnb_ref[...]
    o = o + _head_sums(r * k * rk_ref[...]) * v
    o_ref[...] = (o * gate).reshape(nb, n, RWKV_WIDTH)


def _rwkv_mix(p, mu, w0, w_up, a0, a_up, g_up, k_k, k_a, r_k, ln_w, ln_b):
    b, s, _ = p.shape
    vec = lambda t: t.astype(F32).reshape(1, -1)
    split = lambda t: jnp.stack(_terms(t.astype(F32), RWKV_TERMS))
    consts = [vec(mu), vec(w0), split(w_up), vec(a0), split(a_up), split(g_up), vec(k_k), vec(k_a), vec(r_k),
              vec(ln_w), vec(ln_b)]
    return pl.pallas_call(
        _rwkv_kernel,
        name="rwkv7",
        grid=(s // RWKV_CHUNK,),
        in_specs=[pl.BlockSpec((b, RWKV_CHUNK, RWKV_IN), lambda c: (0, c, 0))]
        + [_const_spec(t.shape) for t in consts],
        out_specs=pl.BlockSpec((b, RWKV_CHUNK, RWKV_WIDTH), lambda c: (0, c, 0)),
        out_shape=jax.ShapeDtypeStruct((b, s, RWKV_WIDTH), F32),
        scratch_shapes=[pltpu.VMEM((b, RWKV_IN), F32),
                        pltpu.VMEM((b * RWKV_WIDTH // LANES, LANES, LANES), F32)],
        compiler_params=_params("arbitrary"),
    )(p, *consts)


def _s5_prep_kernel(are_ref, aim_ref, lstep_ref, bre_ref, bim_ref, bmat_ref, pw_ref):
    lam_re = jnp.minimum(are_ref[...], -1e-4)
    lam_im = aim_ref[...]
    step = jnp.exp(lstep_ref[...])
    z_re, z_im = lam_re * step, lam_im * step
    mag = jnp.exp(z_re)
    num_re, num_im = mag * jnp.cos(z_im) - 1.0, mag * jnp.sin(z_im)
    den = lam_re * lam_re + lam_im * lam_im
    coef_re = (num_re * lam_re + num_im * lam_im) / den
    coef_im = (num_im * lam_re - num_re * lam_im) / den
    bre, bim = bre_ref[...], bim_ref[...]
    bmat_ref[:, :S5_LANES] = (coef_re * bre - coef_im * bim).astype(bmat_ref.dtype)
    bmat_ref[:, S5_LANES:] = (coef_re * bim + coef_im * bre).astype(bmat_ref.dtype)
    steps = (lax.broadcasted_iota(jnp.int32, (SUBLANES, 1), 0) + 1).astype(F32)
    pmag = jnp.exp(steps * z_re)
    pw_ref[:, :S5_LANES] = pmag * jnp.cos(steps * z_im)
    pw_ref[:, S5_LANES:] = pmag * jnp.sin(steps * z_im)


def _s5_kernel(u_ref, bmat_ref, pw_ref, cre_ref, cim_ref, d_ref, wglu_ref, bglu_ref, gain_ref, o_ref, carry_ref):
    n = S5_CHUNK

    @pl.when(pl.program_id(1) == 0)
    def _():
        carry_ref[...] = jnp.zeros_like(carry_ref)

    u = u_ref[0]
    bu = jnp.dot(u.astype(BF16), bmat_ref[...], preferred_element_type=F32)
    x_re, x_im = bu[:, :S5_LANES], bu[:, S5_LANES:]
    sub = lax.broadcasted_iota(jnp.int32, (n, 1), 0) % SUBLANES
    dist = 1
    while dist < SUBLANES:
        l_re = pw_ref[dist - 1:dist, :S5_LANES]
        l_im = pw_ref[dist - 1:dist, S5_LANES:]
        keep = sub >= dist
        s_re = jnp.where(keep, pltpu.roll(x_re, dist, axis=0), 0.0)
        s_im = jnp.where(keep, pltpu.roll(x_im, dist, axis=0), 0.0)
        x_re, x_im = x_re + l_re * s_re - l_im * s_im, x_im + l_re * s_im + l_im * s_re
        dist *= 2
    c_re, c_im = carry_ref[:, :S5_LANES], carry_ref[:, S5_LANES:]
    p_re, p_im = pw_ref[:, :S5_LANES], pw_ref[:, S5_LANES:]
    groups_re, groups_im = [], []
    for g in range(n // SUBLANES):
        g_re, g_im = x_re[g * SUBLANES:(g + 1) * SUBLANES], x_im[g * SUBLANES:(g + 1) * SUBLANES]
        g_re, g_im = g_re + p_re * c_re - p_im * c_im, g_im + p_re * c_im + p_im * c_re
        c_re, c_im = g_re[SUBLANES - 1:], g_im[SUBLANES - 1:]
        groups_re.append(g_re)
        groups_im.append(g_im)
    carry_ref[:, :S5_LANES] = c_re
    carry_ref[:, S5_LANES:] = c_im
    x_re, x_im = jnp.concatenate(groups_re, axis=0), jnp.concatenate(groups_im, axis=0)

    y = (jnp.dot(x_re.astype(BF16), cre_ref[...], preferred_element_type=F32)
         - jnp.dot(x_im.astype(BF16), cim_ref[...], preferred_element_type=F32)) + d_ref[...] * u
    y = jax.nn.gelu(y)
    y = y * jax.nn.sigmoid(jnp.dot(y.astype(BF16), wglu_ref[...], preferred_element_type=F32) + bglu_ref[...])
    o_ref[0] = _rms(y, gain_ref[...])


def _block_diag(t):
    g, rows, cols = t.shape
    return jnp.einsum('grc,gh->grhc', t, jnp.eye(g, dtype=t.dtype)).reshape(g * rows, g * cols)


def _s5_mix(u, a_re, a_im, log_step, b_re, b_im, c_re, c_im, d, w_glu, b_glu, gain):
    b, s, width = u.shape
    lane = lambda t: t.astype(F32).reshape(1, S5_LANES)
    to_cp = lambda t: _block_diag(jnp.swapaxes(t.astype(F32), 1, 2))
    bmat, pw = pl.pallas_call(
        _s5_prep_kernel,
        name="s5_discretise",
        out_shape=[jax.ShapeDtypeStruct((width, 2 * S5_LANES), BF16),
                   jax.ShapeDtypeStruct((SUBLANES, 2 * S5_LANES), F32)],
        compiler_params=pltpu.CompilerParams(vmem_limit_bytes=VMEM_LIMIT),
    )(lane(a_re), lane(a_im), lane(jnp.repeat(log_step[:, None], S5_STATE, axis=1)), to_cp(b_re), to_cp(b_im))
    cre = _block_diag(jnp.swapaxes(c_re, 1, 2)).astype(BF16)
    cim = _block_diag(jnp.swapaxes(c_im, 1, 2)).astype(BF16)
    vec = lambda t: t.astype(F32).reshape(1, width)
    consts = [bmat, pw, cre, cim, vec(d), w_glu.astype(BF16), vec(b_glu), vec(gain)]
    return pl.pallas_call(
        _s5_kernel,
        name="s5_scan",
        grid=(b, s // S5_CHUNK),
        in_specs=[pl.BlockSpec((1, S5_CHUNK, width), lambda i, c: (i, c, 0))]
        + [_const_spec(t.shape) for t in consts],
        out_specs=pl.BlockSpec((1, S5_CHUNK, width), lambda i, c: (i, c, 0)),
        out_shape=jax.ShapeDtypeStruct((b, s, width), F32),
        scratch_shapes=[pltpu.VMEM((1, 2 * S5_LANES), F32)],
        compiler_params=_params("parallel", "arbitrary"),
    )(u, *consts)


def _attn_kernel(q_ref, *refs):
    k_refs = refs[:ATTN_KEY_BLOCKS]
    v_refs = refs[ATTN_KEY_BLOCKS:2 * ATTN_KEY_BLOCKS]
    bias_ref, gain_ref, o_ref = refs[2 * ATTN_KEY_BLOCKS:]
    n_keys = ATTN_KEY_BLOCKS * ATTN_TILE
    q = q_ref[0] * (HEAD_DIM ** -0.5)
    k = jnp.concatenate([kr[0] for kr in k_refs], axis=0)
    v = jnp.concatenate([vr[0] for vr in v_refs], axis=0)
    key_row = pl.program_id(1) * ATTN_TILE + lax.broadcasted_iota(jnp.int32, (1, n_keys), 1)
    neg = jnp.finfo(F32).min
    masked = (bias_ref[0] == neg) | (key_row < ATTN_LEFT * ATTN_CHUNK)
    heads = [slice(h * HEAD_DIM, (h + 1) * HEAD_DIM) for h in range(ATTN_HEADS)]
    s = [lax.dot_general(q[:, sl], k[:, sl], _CONTRACT["nt"], preferred_element_type=F32) for sl in heads]
    s = [jnp.where(masked, neg, x + bias_ref[h]) for h, x in enumerate(s)]
    e = [jnp.exp(x - jnp.max(x, axis=-1, keepdims=True)) for x in s]
    pv = [jnp.dot(x.astype(BF16), v[:, sl], preferred_element_type=F32) for x, sl in zip(e, heads)]
    outs = [x / jnp.sum(y, axis=-1, keepdims=True) for x, y in zip(pv, e)]
    o_ref[0] = _rms(jnp.concatenate(outs, axis=1), gain_ref[...])


def _attn_bias_table(rel_bias):
    n_keys = ATTN_KEY_BLOCKS * ATTN_TILE
    left = ATTN_LEFT * ATTN_CHUNK
    period = n_keys + ATTN_TILE
    rb = rel_bias.astype(F32)
    heads = rb.shape[0]
    far = lambda width: jnp.broadcast_to(rb[:, 2 * MAX_REL:], (heads, width))
    g = jnp.concatenate([far(left - MAX_REL), rb[:, 1:][:, ::-1], far(period - left - MAX_REL)], axis=1)
    toeplitz = jnp.tile(g, (1, ATTN_TILE))[:, :ATTN_TILE * (period - 1)].reshape(heads, ATTN_TILE, period - 1)
    q = np.arange(ATTN_TILE)[:, None]
    j = np.arange(n_keys)[None, :]
    in_band = (j // ATTN_CHUNK >= q // ATTN_CHUNK) & (j // ATTN_CHUNK <= q // ATTN_CHUNK + ATTN_LEFT)
    return jnp.where(in_band[None], toeplitz[:, :, :n_keys], jnp.finfo(F32).min)


def _chunk_attention(qkv, rel_bias, gain):
    b, s, _ = qkv.shape
    pad = ((0, 0), (ATTN_LEFT * ATTN_CHUNK, 0), (0, 0))
    kp = jnp.pad(qkv[:, :, ATTN_WIDTH:2 * ATTN_WIDTH], pad)
    vp = jnp.pad(qkv[:, :, 2 * ATTN_WIDTH:], pad)
    bias = _attn_bias_table(rel_bias)
    key_specs = [pl.BlockSpec((1, ATTN_TILE, ATTN_WIDTH), lambda i, c, j=j: (i, c + j, 0))
                 for j in range(ATTN_KEY_BLOCKS)]
    return pl.pallas_call(
        _attn_kernel,
        name="chunk_attention",
        grid=(b, s // ATTN_TILE),
        in_specs=[pl.BlockSpec((1, ATTN_TILE, ATTN_WIDTH), lambda i, c: (i, c, 0))] + key_specs + key_specs
        + [_const_spec(bias.shape), _const_spec((1, ATTN_WIDTH))],
        out_specs=pl.BlockSpec((1, ATTN_TILE, ATTN_WIDTH), lambda i, c: (i, c, 0)),
        out_shape=jax.ShapeDtypeStruct((b, s, ATTN_WIDTH), F32),
        compiler_params=_params("parallel", "parallel"),
    )(qkv, *([kp] * ATTN_KEY_BLOCKS), *([vp] * ATTN_KEY_BLOCKS), bias, gain.astype(F32).reshape(1, ATTN_WIDTH))


def _outproj_kernel(route, x_ref, orw_ref, os5_ref, oat_ref, w_ref, gpost_ref, gpre_ref, *refs):
    mixed = jnp.concatenate([orw_ref[...], os5_ref[...], oat_ref[...]], axis=1).astype(BF16)
    y = jnp.dot(mixed, w_ref[...], preferred_element_type=F32)
    x = x_ref[...] + _rms(y, gpost_ref[...])
    h = _rms(x, gpre_ref[...])
    if not route:
        x_out, h_out = refs
    else:
        router_ref, x_out, h_out, route_out = refs
        logits = _mm(_terms(h, 2), _terms(router_ref[...], 2))
        lane = lax.broadcasted_iota(jnp.int32, logits.shape, 1)
        neg = jnp.finfo(F32).min
        logits = jnp.where(lane < N_EXPERTS, logits, neg)
        top1 = jnp.max(logits, axis=-1, keepdims=True)
        idx1 = jnp.min(jnp.where(logits == top1, lane, LANES), axis=-1, keepdims=True)
        rest = jnp.where(lane == idx1, neg, logits)
        top2 = jnp.max(rest, axis=-1, keepdims=True)
        idx2 = jnp.min(jnp.where(rest == top2, lane, LANES), axis=-1, keepdims=True)
        e2 = jnp.exp(top2 - top1)
        route_out[...] = (jnp.where(lane == 0, idx1.astype(F32), 0.0) + jnp.where(lane == 1, idx2.astype(F32), 0.0)
                          + jnp.where(lane == 2, 1.0 / (1.0 + e2), 0.0) + jnp.where(lane == 3, e2 / (1.0 + e2), 0.0))
    x_out[...] = x
    h_out[...] = h.astype(h_out.dtype)


def _out_projection(x, o_rwkv, o_s5, o_attn, w_out, gain_post, gain_pre, router):
    t, d = x.shape
    row = lambda n: pl.BlockSpec((ROW_TILE, n), lambda i: (i, 0))
    vec = lambda g: g.astype(F32).reshape(1, d)
    ins = [x, o_rwkv, o_s5, o_attn, w_out.astype(BF16), vec(gain_post), vec(gain_pre)]
    in_specs = [row(d), row(o_rwkv.shape[1]), row(o_s5.shape[1]), row(o_attn.shape[1]),
                _const_spec(w_out.shape), _const_spec((1, d)), _const_spec((1, d))]
    out_specs = [row(d), row(d)]
    out_shape = [jax.ShapeDtypeStruct((t, d), F32), jax.ShapeDtypeStruct((t, d), BF16 if router is None else F32)]
    if router is not None:
        ins.append(jnp.pad(router.astype(F32), ((0, 0), (0, LANES - N_EXPERTS))))
        in_specs.append(_const_spec((d, LANES)))
        out_specs.append(row(LANES))
        out_shape.append(jax.ShapeDtypeStruct((t, LANES), F32))
    return pl.pallas_call(
        functools.partial(_outproj_kernel, router is not None),
        name="out_projection",
        grid=(t // ROW_TILE,),
        in_specs=in_specs, out_specs=out_specs, out_shape=out_shape,
        compiler_params=_params("parallel"),
    )(*ins)


def _ffn_kernel(x_ref, h_ref, wg_ref, wu_ref, wd_ref, gain_ref, o_ref, acc_ref):
    f = pl.program_id(1)

    @pl.when(f == 0)
    def _():
        acc_ref[...] = jnp.zeros_like(acc_ref)

    h = h_ref[...]
    act = jax.nn.silu(jnp.dot(h, wg_ref[...], preferred_element_type=F32)) * jnp.dot(h, wu_ref[...], preferred_element_type=F32)
    acc_ref[...] += jnp.dot(act.astype(BF16), wd_ref[...], preferred_element_type=F32)

    @pl.when(f == pl.num_programs(1) - 1)
    def _():
        o_ref[...] = x_ref[...] + _rms(acc_ref[...], gain_ref[...])


def _ffn(x, h, w_gate, w_up, w_down, gain):
    t, d = x.shape
    d_ff = w_gate.shape[1]
    row = lambda n: pl.BlockSpec((ROW_TILE, n), lambda i, f: (i, 0))
    return pl.pallas_call(
        _ffn_kernel,
        name="swiglu_ffn",
        grid=(t // ROW_TILE, d_ff // FFN_TILE),
        in_specs=[row(d), row(d),
                  pl.BlockSpec((d, FFN_TILE), lambda i, f: (0, f)),
                  pl.BlockSpec((d, FFN_TILE), lambda i, f: (0, f)),
                  pl.BlockSpec((FFN_TILE, d), lambda i, f: (f, 0)),
                  pl.BlockSpec((1, d), lambda i, f: (0, 0))],
        out_specs=row(d),
        out_shape=jax.ShapeDtypeStruct((t, d), F32),
        scratch_shapes=[pltpu.VMEM((ROW_TILE, d), F32)],
        compiler_params=_params("parallel", "arbitrary"),
    )(x, h, w_gate.astype(BF16), w_up.astype(BF16), w_down.astype(BF16), gain.astype(F32).reshape(1, d))


MOE_TILE = 512


def _moe_plan(route, n_experts):
    t = route.shape[0]
    n = 2 * t
    tm = MOE_TILE
    expert = route[:, :2].astype(jnp.int32).reshape(n)
    onehot = (expert[:, None] == jnp.arange(n_experts, dtype=jnp.int32)[None, :]).astype(jnp.int32)
    rank = jnp.sum(jnp.cumsum(onehot, axis=0) * onehot, axis=1) - 1
    counts = jnp.sum(onehot, axis=0)
    padded = (counts + tm - 1) // tm * tm
    ends = jnp.cumsum(padded)
    slot = jnp.sum((ends - padded)[None, :] * onehot, axis=1) + rank
    n_tiles = n // tm + n_experts
    slots = n_tiles * tm
    ids = jnp.arange(n, dtype=jnp.int32)
    every = jnp.arange(slots, dtype=jnp.int32)
    dump = n + every % tm + tm * ((every // tm) % 2)
    dst = dump.at[slot].set((ids % 2) * t + ids // 2, unique_indices=True)
    tok = jnp.where(dst < n, dst % t, 0)
    tile_start = jnp.arange(n_tiles, dtype=jnp.int32) * tm
    tile_expert = jnp.minimum(jnp.sum((tile_start[:, None] >= ends[None, :]).astype(jnp.int32), axis=1), n_experts - 1)
    tile_valid = (tile_start < ends[-1]).astype(jnp.int32)
    return tok.reshape(n_tiles, 1, tm), dst.reshape(n_tiles, 1, tm), tile_expert, tile_valid


def _moe_ffn_kernel(te_ref, tv_ref, tok0_ref, tokn_ref, dst_ref, h_hbm, wg_ref, wu_ref, wd_ref,
                    out_hbm, xbuf, xb, ybuf, acc_ref, gsem, ssem):
    t, f = pl.program_id(0), pl.program_id(1)
    n_t, n_f = pl.num_programs(0), pl.num_programs(1)
    tm = MOE_TILE
    slot = t % 2

    def start_gather(tok_ref, s):
        def body(i, carry):
            pltpu.make_async_copy(h_hbm.at[pl.ds(tok_ref[0, 0, i], 1), :], xbuf.at[s, pl.ds(i, 1), :], gsem.at[s]).start()
            return carry
        lax.fori_loop(0, tm, body, 0, unroll=8)

    def wait_scatter(s):
        pltpu.make_async_copy(ybuf.at[s], out_hbm.at[pl.ds(0, tm), :], ssem.at[s]).wait()

    @pl.when(f == 0)
    def _():
        @pl.when(t == 0)
        def _():
            start_gather(tok0_ref, 0)

        pltpu.make_async_copy(h_hbm.at[pl.ds(0, tm), :], xbuf.at[slot], gsem.at[slot]).wait()

        @pl.when(t + 1 < n_t)
        def _():
            start_gather(tokn_ref, 1 - slot)

        xb[...] = xbuf[slot].astype(BF16)
        acc_ref[...] = jnp.zeros_like(acc_ref)

    @pl.when(tv_ref[t] == 1)
    def _():
        x = xb[...]
        act = jax.nn.silu(jnp.dot(x, wg_ref[0], preferred_element_type=F32)) * jnp.dot(x, wu_ref[0], preferred_element_type=F32)
        acc_ref[...] += jnp.dot(act.astype(BF16), wd_ref[0], preferred_element_type=F32)

    @pl.when(f == n_f - 1)
    def _():
        @pl.when(t >= 2)
        def _():
            wait_scatter(slot)

        ybuf[slot] = acc_ref[...]

        def body(i, carry):
            pltpu.make_async_copy(ybuf.at[slot, pl.ds(i, 1), :], out_hbm.at[pl.ds(dst_ref[0, 0, i], 1), :], ssem.at[slot]).start()
            return carry
        lax.fori_loop(0, tm, body, 0, unroll=8)

        @pl.when(t == n_t - 1)
        def _():
            wait_scatter(slot)
            wait_scatter(1 - slot)


def _moe_combine_kernel(x_ref, y1_ref, y2_ref, route_ref, gain_ref, o_ref):
    y = route_ref[:, 2:3] * y1_ref[...] + route_ref[:, 3:4] * y2_ref[...]
    o_ref[...] = x_ref[...] + _rms(y, gain_ref[...])


def _moe(x, h, route, w_gate, w_up, w_down, gain):
    t, d = x.shape
    n_e, _, d_ff = w_gate.shape
    tm = MOE_TILE
    tok, dst, tile_expert, tile_valid = _moe_plan(route, n_e)
    n_tiles = tok.shape[0]
    idx_spec = lambda shift: pl.BlockSpec((1, 1, tm), lambda i, f, te, tv: (jnp.minimum(i + shift, n_tiles - 1), 0, 0),
                                          memory_space=pltpu.SMEM)
    grid_spec = pltpu.PrefetchScalarGridSpec(
        num_scalar_prefetch=2,
        grid=(n_tiles, d_ff // FFN_TILE),
        in_specs=[idx_spec(0), idx_spec(1), idx_spec(0),
                  pl.BlockSpec(memory_space=pl.ANY),
                  pl.BlockSpec((1, d, FFN_TILE), lambda i, f, te, tv: (te[i], 0, f)),
                  pl.BlockSpec((1, d, FFN_TILE), lambda i, f, te, tv: (te[i], 0, f)),
                  pl.BlockSpec((1, FFN_TILE, d), lambda i, f, te, tv: (te[i], f, 0))],
        out_specs=pl.BlockSpec(memory_space=pl.ANY),
        scratch_shapes=[pltpu.VMEM((2, tm, d), F32), pltpu.VMEM((tm, d), BF16), pltpu.VMEM((2, tm, d), F32),
                        pltpu.VMEM((tm, d), F32), pltpu.SemaphoreType.DMA((2,)), pltpu.SemaphoreType.DMA((2,))],
    )
    routed = pl.pallas_call(
        _moe_ffn_kernel,
        name="moe_experts",
        grid_spec=grid_spec,
        out_shape=jax.ShapeDtypeStruct((2 * t + 2 * tm, d), F32),
        compiler_params=_params("arbitrary", "arbitrary"),
    )(tile_expert, tile_valid, tok, tok, dst, h, w_gate.astype(BF16), w_up.astype(BF16), w_down.astype(BF16))
    row = lambda n: pl.BlockSpec((ROW_TILE, n), lambda i: (i, 0))
    second = pl.BlockSpec((ROW_TILE, d), lambda i: (i + t // ROW_TILE, 0))
    return pl.pallas_call(
        _moe_combine_kernel,
        name="moe_combine",
        grid=(t // ROW_TILE,),
        in_specs=[row(d), row(d), second, row(LANES), _const_spec((1, d))],
        out_specs=row(d),
        out_shape=jax.ShapeDtypeStruct((t, d), F32),
        compiler_params=_params("parallel"),
    )(x, routed, routed, route, gain.astype(F32).reshape(1, d))


def kernel(x, norm_mix_pre, norm_mix_post, norm_ffn_pre, norm_ffn_post, w_in, w_out, rwkv_mu, rwkv_w0, rwkv_w_up, rwkv_a0, rwkv_a_up, rwkv_g_up, rwkv_k_k, rwkv_k_a, rwkv_r_k, rwkv_ln_w, rwkv_ln_b, s5_a_re, s5_a_im, s5_log_step, s5_b_re, s5_b_im, s5_c_re, s5_c_im, s5_d, s5_w_glu, s5_b_glu, s5_norm, attn_rel_bias, attn_norm, ffn_w_gate, ffn_w_up, ffn_w_down, moe_router, moe_w_gate, moe_w_up, moe_w_down):
    b, s, d = x.shape
    depth = w_in.shape[0]
    xt = x.reshape(b * s, d)
    for layer in range(depth):
        p, u, qkv = _in_projection(xt, norm_mix_pre[layer], w_in[layer])
        o_rwkv = _rwkv_mix(p.reshape(b, s, -1), rwkv_mu[layer], rwkv_w0[layer], rwkv_w_up[layer], rwkv_a0[layer],
                           rwkv_a_up[layer], rwkv_g_up[layer], rwkv_k_k[layer], rwkv_k_a[layer], rwkv_r_k[layer],
                           rwkv_ln_w[layer], rwkv_ln_b[layer])
        o_s5 = _s5_mix(u.reshape(b, s, -1), s5_a_re[layer], s5_a_im[layer], s5_log_step[layer], s5_b_re[layer],
                       s5_b_im[layer], s5_c_re[layer], s5_c_im[layer], s5_d[layer], s5_w_glu[layer],
                       s5_b_glu[layer], s5_norm[layer])
        o_attn = _chunk_attention(qkv.reshape(b, s, -1), attn_rel_bias[layer], attn_norm[layer])
        i = layer // 2
        moe = layer % 2 == 1
        outs = _out_projection(xt, o_rwkv.reshape(b * s, -1), o_s5.reshape(b * s, -1), o_attn.reshape(b * s, -1),
                               w_out[layer], norm_mix_post[layer], norm_ffn_pre[layer],
                               moe_router[i] if moe else None)
        if moe:
            xt, h, route = outs
            xt = _moe(xt, h, route, moe_w_gate[i], moe_w_up[i], moe_w_down[i], norm_ffn_post[layer])
        else:
            xt, h = outs
            xt = _ffn(xt, h, ffn_w_gate[i], ffn_w_up[i], ffn_w_down[i], norm_ffn_post[layer])
    return xt.reshape(b, s, d)
```

```python
import functools
import math

import jax
import jax.numpy as jnp
import numpy as np
from jax import lax
from jax.experimental import pallas as pl
from jax.experimental.pallas import tpu as pltpu

F32 = jnp.float32
BF16 = jnp.bfloat16

RMS_EPS = 1e-6
RWKV_GN_EPS = 64e-5
HEAD_DIM = 64
RWKV_HEADS = 6
RWKV_WIDTH = 384
RWKV_IN = 1408
S5_WIDTH = 256
S5_GROUPS = 16
S5_STATE = 64
S5_LANES = S5_GROUPS * S5_STATE
ATTN_HEADS = 6
ATTN_WIDTH = 384
ATTN_CHUNK = 64
ATTN_LEFT = 8
MAX_REL = 128
N_EXPERTS = 8
LANES = 128

RWKV_CHUNK = 64
RWKV_STEP_CHUNKS = 2
S5_CHUNK = 256
SUBLANES = 8
ATTN_TILE = 2 * ATTN_CHUNK
ATTN_KEY_BLOCKS = (ATTN_LEFT * ATTN_CHUNK) // ATTN_TILE + 1
ROW_TILE = 512
FFN_TILE = 1408
VMEM_LIMIT = 48 * 1024 * 1024

_HI = lax.Precision.HIGHEST
_CONTRACT = {"nn": (((1,), (0,)), ((), ())), "nt": (((1,), (1,)), ((), ())), "tn": (((0,), (0,)), ((), ()))}


def _hdot(a, b):
    return jnp.dot(a, b, precision=_HI, preferred_element_type=F32)


def _terms(x, n):
    out = []
    for i in range(n):
        t = x.astype(BF16)
        out.append(t)
        if i + 1 < n:
            x = x - t.astype(F32)
    return tuple(out)


def _mm(a, b, kind="nn"):
    order = max(len(a), len(b))
    acc = None
    for i, ai in enumerate(a):
        for j, bj in enumerate(b):
            if i + j < order:
                d = lax.dot_general(ai, bj, _CONTRACT[kind], preferred_element_type=F32)
                acc = d if acc is None else acc + d
    return acc


def _rms(x, gain):
    return x * lax.rsqrt(jnp.mean(x * x, axis=-1, keepdims=True) + RMS_EPS) * gain


def _params(*semantics):
    return pltpu.CompilerParams(dimension_semantics=semantics, vmem_limit_bytes=VMEM_LIMIT)


def _const_spec(shape):
    zeros = (0,) * len(shape)
    return pl.BlockSpec(shape, lambda *_: zeros)


def _inproj_kernel(x_ref, gain_ref, w_ref, p_ref, u_ref, qkv_ref):
    h = _rms(x_ref[...], gain_ref[...]).astype(BF16)
    s5_at, qkv_at = RWKV_IN, RWKV_IN + S5_WIDTH
    p_ref[...] = jnp.dot(h, w_ref[:, :s5_at], preferred_element_type=F32)
    u_ref[...] = jnp.dot(h, w_ref[:, s5_at:qkv_at], preferred_element_type=F32)
    qkv_ref[...] = jnp.dot(h, w_ref[:, qkv_at:], preferred_element_type=F32).astype(BF16)


def _in_projection(x, gain, w_in):
    t, d = x.shape
    n_in = w_in.shape[1]
    n_qkv = n_in - RWKV_IN - S5_WIDTH
    row = lambda n: pl.BlockSpec((ROW_TILE, n), lambda i: (i, 0))
    return pl.pallas_call(
        _inproj_kernel,
        name="in_projection",
        grid=(t // ROW_TILE,),
        in_specs=[row(d), _const_spec((1, d)), _const_spec((d, n_in))],
        out_specs=[row(RWKV_IN), row(S5_WIDTH), row(n_qkv)],
        out_shape=[jax.ShapeDtypeStruct((t, RWKV_IN), F32), jax.ShapeDtypeStruct((t, S5_WIDTH), F32),
                   jax.ShapeDtypeStruct((t, n_qkv), BF16)],
        compiler_params=_params("parallel"),
    )(x, gain.reshape(1, d), w_in.astype(BF16))


RWKV_TERMS = 1
HEADS_PER_VREG = LANES // HEAD_DIM


def _t(x):
    return _terms(x, RWKV_TERMS)


def _head_sums(x):
    first = lax.broadcasted_iota(jnp.int32, (1, LANES), 1) < HEAD_DIM
    outs = []
    for j in range(x.shape[1] // LANES):
        xp = x[:, j * LANES:(j + 1) * LANES]
        s0 = jnp.sum(jnp.where(first, xp, 0.0), axis=-1, keepdims=True)
        s1 = jnp.sum(jnp.where(first, 0.0, xp), axis=-1, keepdims=True)
        outs.append(jnp.where(first, s0, s1))
    return jnp.concatenate(outs, axis=1)


def _rwkv_kernel(p_ref, mu_ref, w0_ref, wup_ref, a0_ref, aup_ref, gup_ref, kk_ref, ka_ref, rk_ref,
                 lnw_ref, lnb_ref, o_ref, prev_ref, state_ref):
    nb, per_seq = p_ref.shape[0], p_ref.shape[1]
    n = RWKV_CHUNK
    rows = nb * per_seq
    items = rows // n
    w_at = 3 * RWKV_WIDTH
    pairs = RWKV_WIDTH // LANES

    @pl.when(pl.program_id(0) == 0)
    def _():
        prev_ref[...] = jnp.zeros_like(prev_ref)
        state_ref[...] = jnp.zeros_like(state_ref)

    p = p_ref[...].reshape(rows, RWKV_IN)
    row = lax.broadcasted_iota(jnp.int32, (rows, 1), 0)
    shifted = pltpu.roll(p, 1, axis=0)
    for b in range(nb):
        shifted = jnp.where(row == b * per_seq, prev_ref[b:b + 1, :], shifted)
    for b in range(nb):
        prev_ref[b:b + 1, :] = p[(b + 1) * per_seq - 1:(b + 1) * per_seq, :]
    p = p + (shifted - p) * mu_ref[...]

    r = p[:, :RWKV_WIDTH]
    k = p[:, RWKV_WIDTH:2 * RWKV_WIDTH]
    v = p[:, 2 * RWKV_WIDTH:w_at]
    w_lo = p[:, w_at:w_at + 64]
    a_lo = p[:, w_at + 64:w_at + 128]
    g_lo = p[:, w_at + 128:]

    lora = lambda x, w_ref: _mm(_t(x), tuple(w_ref[i] for i in range(RWKV_TERMS)))
    w = -jax.nn.softplus(-(w0_ref[...] + lora(jnp.tanh(w_lo), wup_ref))) - 0.5
    log_decay = -jnp.exp(w)
    a = jax.nn.sigmoid(a0_ref[...] + lora(a_lo, aup_ref))
    gate = lora(jax.nn.sigmoid(g_lo), gup_ref)

    kk = k * kk_ref[...]
    kk = kk / jnp.maximum(jnp.sqrt(_head_sums(kk * kk)), 1e-12)
    k = k * (1.0 + (a - 1.0) * ka_ref[...])

    s_i = lax.broadcasted_iota(jnp.int32, (rows, rows), 0)
    s_j = lax.broadcasted_iota(jnp.int32, (rows, rows), 1)
    causal = ((s_i // n == s_j // n) & (s_i >= s_j)).astype(BF16)
    cum = _mm((causal,), _terms(log_decay, 3))
    cum_end = cum[n - 1:n, :]
    for it in range(1, items):
        cum_end = jnp.where(row >= it * n, cum[(it + 1) * n - 1:(it + 1) * n, :], cum_end)
    inv_g = jnp.exp(-cum)
    to_end = jnp.exp(cum_end - cum)
    beta = kk * a
    r_bar = r * jnp.exp(cum)
    a_bar = -kk * jnp.exp(cum - log_decay)
    b_til = beta * inv_g
    k_til = k * inv_g
    b_end = beta * to_end
    k_end = k * to_end
    g_end = jnp.exp(cum_end)

    lane = lax.broadcasted_iota(jnp.int32, (1, LANES), 1)
    q_i = lax.broadcasted_iota(jnp.int32, (n, LANES), 0)
    q_j = lax.broadcasted_iota(jnp.int32, (n, LANES), 1) % n
    lower = q_i >= q_j
    strictly_lower = q_i > q_j
    eye = (lax.broadcasted_iota(jnp.int32, (n, n), 0) == lax.broadcasted_iota(jnp.int32, (n, n), 1)).astype(F32)
    c_i = lax.broadcasted_iota(jnp.int32, (LANES, LANES), 0)
    c_j = lax.broadcasted_iota(jnp.int32, (LANES, LANES), 1)
    same_head = c_i // HEAD_DIM == c_j // HEAD_DIM
    diagonal = c_i == c_j
    zeros = jnp.zeros((n, LANES), F32)

    pair_ids = [(it, j) for it in range(items) for j in range(pairs)]
    head_ids = [(pi, h) for pi in range(len(pair_ids)) for h in range(HEADS_PER_VREG)]
    heads_of = lambda pi: [hi for hi, (p_i, _) in enumerate(head_ids) if p_i == pi]
    cut = lambda x, pi: x[pair_ids[pi][0] * n:(pair_ids[pi][0] + 1) * n,
                          pair_ids[pi][1] * LANES:(pair_ids[pi][1] + 1) * LANES]
    rb = [cut(r_bar, pi) for pi in range(len(pair_ids))]
    be = [cut(b_end, pi) for pi in range(len(pair_ids))]
    vv = [cut(v, pi) for pi in range(len(pair_ids))]
    bk = [_t(jnp.concatenate([cut(b_til, pi), cut(k_til, pi)], axis=0)) for pi in range(len(pair_ids))]
    mine = [lane // HEAD_DIM == h for _, h in head_ids]
    abm = [jnp.where(mine[hi], cut(a_bar, pi), 0.0) for hi, (pi, _) in enumerate(head_ids)]
    v0 = [_t(jnp.concatenate([zeros, jnp.where(mine[hi], vv[pi], 0.0)], axis=0)) for hi, (pi, _) in enumerate(head_ids)]
    m = [_mm(_t(jnp.concatenate([abm[hi], jnp.where(mine[hi], rb[pi], 0.0)], axis=0)), bk[pi], "nt")
         for hi, (pi, _) in enumerate(head_ids)]
    top = [jnp.where(strictly_lower, x[:n], 0.0) for x in m]
    bot = [jnp.where(lower, x[n:], 0.0) for x in m]
    y = [_mm(_t(top[hi]), v0[hi]) for hi in range(len(head_ids))]
    power = [x[:, :n] for x in top]
    inv = [eye + x for x in power]
    for _ in range(int(math.log2(n)) - 1):
        pt = [_t(x) for x in power]
        power = [_mm(x, x) for x in pt]
        inv = [x + _mm(_t(x), _t(pw)) for x, pw in zip(inv, power)]
    wu = [_mm(_t(inv[hi]), _t(jnp.concatenate([abm[hi], y[hi]], axis=1))) for hi in range(len(head_ids))]
    v_low = [tuple(jnp.concatenate([jnp.zeros((n, LANES), BF16), t[n:]], axis=1) for t in v0[hi])
             for hi in range(len(head_ids))]
    ro = [_mm(_t(bot[hi]), tuple(jnp.concatenate([a, b2], axis=0) for a, b2 in zip(_t(wu[hi]), v_low[hi])))
          for hi in range(len(head_ids))]
    both = lambda xs, pi, sl: sum(xs[hi][:, sl] for hi in heads_of(pi))
    first, second = slice(0, LANES), slice(LANES, 2 * LANES)
    g_t = [jnp.where(same_head, _mm(_t(be[pi]), _t(both(wu, pi, first)), "tn"), 0.0)
           + jnp.where(diagonal, cut(g_end, pi)[:1], 0.0) for pi in range(len(pair_ids))]
    s2_t = [_mm(_t(jnp.concatenate([be[pi], cut(k_end, pi)], axis=0)),
                _t(jnp.concatenate([both(wu, pi, second), vv[pi]], axis=0)), "tn") for pi in range(len(pair_ids))]
    r2 = [_t(rb[pi] + both(ro, pi, first)) for pi in range(len(pair_ids))]
    out = [None] * len(pair_ids)
    per_step = items // nb
    for b in range(nb):
        for j in range(pairs):
            state = state_ref[b * pairs + j]
            for c in range(per_step):
                pi = (b * per_step + c) * pairs + j
                out[pi] = _mm(r2[pi], _t(state)) + both(ro, pi, second)
                state = _mm(_t(g_t[pi]), _t(state)) + jnp.where(same_head, s2_t[pi], 0.0)
            state_ref[b * pairs + j] = state
    o = jnp.concatenate([jnp.concatenate(out[it * pairs:(it + 1) * pairs], axis=1) for it in range(items)], axis=0)

    inv_n = 1.0 / HEAD_DIM
    cen = o - _head_sums(o) * inv_n
    var = _head_sums(cen * cen) * inv_n
    o = cen * lax.rsqrt(var + RWKV_GN_EPS) * lnw_ref[...] + lnb_ref[...]
    o = o + _head_sums(r * k * rk_ref[...]) * v
    o_ref[...] = (o * gate).reshape(nb, per_seq, RWKV_WIDTH)


def _rwkv_mix(p, mu, w0, w_up, a0, a_up, g_up, k_k, k_a, r_k, ln_w, ln_b):
    b, s, _ = p.shape
    per_step = RWKV_CHUNK * RWKV_STEP_CHUNKS
    vec = lambda t: t.astype(F32).reshape(1, -1)
    split = lambda t: jnp.stack(_terms(t.astype(F32), RWKV_TERMS))
    consts = [vec(mu), vec(w0), split(w_up), vec(a0), split(a_up), split(g_up), vec(k_k), vec(k_a), vec(r_k),
              vec(ln_w), vec(ln_b)]
    return pl.pallas_call(
        _rwkv_kernel,
        name="rwkv7",
        grid=(s // per_step,),
        in_specs=[pl.BlockSpec((b, per_step, RWKV_IN), lambda c: (0, c, 0))]
        + [_const_spec(t.shape) for t in consts],
        out_specs=pl.BlockSpec((b, per_step, RWKV_WIDTH), lambda c: (0, c, 0)),
        out_shape=jax.ShapeDtypeStruct((b, s, RWKV_WIDTH), F32),
        scratch_shapes=[pltpu.VMEM((b, RWKV_IN), F32),
                        pltpu.VMEM((b * RWKV_WIDTH // LANES, LANES, LANES), F32)],
        compiler_params=_params("arbitrary"),
    )(p, *consts)


def _s5_prep_kernel(are_ref, aim_ref, lstep_ref, bre_ref, bim_ref, bmat_ref, pw_ref):
    lam_re = jnp.minimum(are_ref[...], -1e-4)
    lam_im = aim_ref[...]
    step = jnp.exp(lstep_ref[...])
    z_re, z_im = lam_re * step, lam_im * step
    mag = jnp.exp(z_re)
    num_re, num_im = mag * jnp.cos(z_im) - 1.0, mag * jnp.sin(z_im)
    den = lam_re * lam_re + lam_im * lam_im
    coef_re = (num_re * lam_re + num_im * lam_im) / den
    coef_im = (num_im * lam_re - num_re * lam_im) / den
    bre, bim = bre_ref[...], bim_ref[...]
    bmat_ref[:, :S5_LANES] = (coef_re * bre - coef_im * bim).astype(bmat_ref.dtype)
    bmat_ref[:, S5_LANES:] = (coef_re * bim + coef_im * bre).astype(bmat_ref.dtype)
    steps = (lax.broadcasted_iota(jnp.int32, (SUBLANES, 1), 0) + 1).astype(F32)
    pmag = jnp.exp(steps * z_re)
    pw_ref[:, :S5_LANES] = pmag * jnp.cos(steps * z_im)
    pw_ref[:, S5_LANES:] = pmag * jnp.sin(steps * z_im)


def _s5_kernel(u_ref, bmat_ref, pw_ref, cre_ref, cim_ref, d_ref, wglu_ref, bglu_ref, gain_ref, o_ref, carry_ref):
    n = S5_CHUNK

    @pl.when(pl.program_id(1) == 0)
    def _():
        carry_ref[...] = jnp.zeros_like(carry_ref)

    u = u_ref[0]
    bu = jnp.dot(u.astype(BF16), bmat_ref[...], preferred_element_type=F32)
    x_re, x_im = bu[:, :S5_LANES], bu[:, S5_LANES:]
    sub = lax.broadcasted_iota(jnp.int32, (n, 1), 0) % SUBLANES
    dist = 1
    while dist < SUBLANES:
        l_re = pw_ref[dist - 1:dist, :S5_LANES]
        l_im = pw_ref[dist - 1:dist, S5_LANES:]
        keep = sub >= dist
        s_re = jnp.where(keep, pltpu.roll(x_re, dist, axis=0), 0.0)
        s_im = jnp.where(keep, pltpu.roll(x_im, dist, axis=0), 0.0)
        x_re, x_im = x_re + l_re * s_re - l_im * s_im, x_im + l_re * s_im + l_im * s_re
        dist *= 2
    c_re, c_im = carry_ref[:, :S5_LANES], carry_ref[:, S5_LANES:]
    p_re, p_im = pw_ref[:, :S5_LANES], pw_ref[:, S5_LANES:]
    groups_re, groups_im = [], []
    for g in range(n // SUBLANES):
        g_re, g_im = x_re[g * SUBLANES:(g + 1) * SUBLANES], x_im[g * SUBLANES:(g + 1) * SUBLANES]
        g_re, g_im = g_re + p_re * c_re - p_im * c_im, g_im + p_re * c_im + p_im * c_re
        c_re, c_im = g_re[SUBLANES - 1:], g_im[SUBLANES - 1:]
        groups_re.append(g_re)
        groups_im.append(g_im)
    carry_ref[:, :S5_LANES] = c_re
    carry_ref[:, S5_LANES:] = c_im
    x_re, x_im = jnp.concatenate(groups_re, axis=0), jnp.concatenate(groups_im, axis=0)

    y = (jnp.dot(x_re.astype(BF16), cre_ref[...], preferred_element_type=F32)
         - jnp.dot(x_im.astype(BF16), cim_ref[...], preferred_element_type=F32)) + d_ref[...] * u
    y = jax.nn.gelu(y)
    y = y * jax.nn.sigmoid(jnp.dot(y.astype(BF16), wglu_ref[...], preferred_element_type=F32) + bglu_ref[...])
    o_ref[0] = _rms(y, gain_ref[...])


def _block_diag(t):
    g, rows, cols = t.shape
    return jnp.einsum('grc,gh->grhc', t, jnp.eye(g, dtype=t.dtype)).reshape(g * rows, g * cols)


def _s5_mix(u, a_re, a_im, log_step, b_re, b_im, c_re, c_im, d, w_glu, b_glu, gain):
    b, s, width = u.shape
    lane = lambda t: t.astype(F32).reshape(1, S5_LANES)
    to_cp = lambda t: _block_diag(jnp.swapaxes(t.astype(F32), 1, 2))
    bmat, pw = pl.pallas_call(
        _s5_prep_kernel,
        name="s5_discretise",
        out_shape=[jax.ShapeDtypeStruct((width, 2 * S5_LANES), BF16),
                   jax.ShapeDtypeStruct((SUBLANES, 2 * S5_LANES), F32)],
        compiler_params=pltpu.CompilerParams(vmem_limit_bytes=VMEM_LIMIT),
    )(lane(a_re), lane(a_im), lane(jnp.repeat(log_step[:, None], S5_STATE, axis=1)), to_cp(b_re), to_cp(b_im))
    cre = _block_diag(jnp.swapaxes(c_re, 1, 2)).astype(BF16)
    cim = _block_diag(jnp.swapaxes(c_im, 1, 2)).astype(BF16)
    vec = lambda t: t.astype(F32).reshape(1, width)
    consts = [bmat, pw, cre, cim, vec(d), w_glu.astype(BF16), vec(b_glu), vec(gain)]
    return pl.pallas_call(
        _s5_kernel,
        name="s5_scan",
        grid=(b, s // S5_CHUNK),
        in_specs=[pl.BlockSpec((1, S5_CHUNK, width), lambda i, c: (i, c, 0))]
        + [_const_spec(t.shape) for t in consts],
        out_specs=pl.BlockSpec((1, S5_CHUNK, width), lambda i, c: (i, c, 0)),
        out_shape=jax.ShapeDtypeStruct((b, s, width), F32),
        scratch_shapes=[pltpu.VMEM((1, 2 * S5_LANES), F32)],
        compiler_params=_params("parallel", "arbitrary"),
    )(u, *consts)


def _attn_kernel(q_ref, *refs):
    k_refs = refs[:ATTN_KEY_BLOCKS]
    v_refs = refs[ATTN_KEY_BLOCKS:2 * ATTN_KEY_BLOCKS]
    bias_ref, gain_ref, o_ref = refs[2 * ATTN_KEY_BLOCKS:]
    n_keys = ATTN_KEY_BLOCKS * ATTN_TILE
    q = q_ref[0] * (HEAD_DIM ** -0.5)
    k = jnp.concatenate([kr[0] for kr in k_refs], axis=0)
    v = jnp.concatenate([vr[0] for vr in v_refs], axis=0)
    key_row = (pl.program_id(1) - (ATTN_KEY_BLOCKS - 1)) * ATTN_TILE + lax.broadcasted_iota(jnp.int32, (1, n_keys), 1)
    neg = jnp.finfo(F32).min
    masked = (bias_ref[0] == neg) | (key_row < 0)
    heads = [slice(h * HEAD_DIM, (h + 1) * HEAD_DIM) for h in range(ATTN_HEADS)]
    s = [lax.dot_general(q[:, sl], k[:, sl], _CONTRACT["nt"], preferred_element_type=F32) for sl in heads]
    s = [jnp.where(masked, neg, x + bias_ref[h]) for h, x in enumerate(s)]
    e = [jnp.exp(x - jnp.max(x, axis=-1, keepdims=True)) for x in s]
    pv = [jnp.dot(x.astype(BF16), v[:, sl], preferred_element_type=F32) for x, sl in zip(e, heads)]
    outs = [x / jnp.sum(y, axis=-1, keepdims=True) for x, y in zip(pv, e)]
    o_ref[0] = _rms(jnp.concatenate(outs, axis=1), gain_ref[...])


def _attn_bias_table(rel_bias):
    n_keys = ATTN_KEY_BLOCKS * ATTN_TILE
    left = ATTN_LEFT * ATTN_CHUNK
    period = n_keys + ATTN_TILE
    rb = rel_bias.astype(F32)
    heads = rb.shape[0]
    far = lambda width: jnp.broadcast_to(rb[:, 2 * MAX_REL:], (heads, width))
    g = jnp.concatenate([far(left - MAX_REL), rb[:, 1:][:, ::-1], far(period - left - MAX_REL)], axis=1)
    toeplitz = jnp.tile(g, (1, ATTN_TILE))[:, :ATTN_TILE * (period - 1)].reshape(heads, ATTN_TILE, period - 1)
    q = np.arange(ATTN_TILE)[:, None]
    j = np.arange(n_keys)[None, :]
    in_band = (j // ATTN_CHUNK >= q // ATTN_CHUNK) & (j // ATTN_CHUNK <= q // ATTN_CHUNK + ATTN_LEFT)
    return jnp.where(in_band[None], toeplitz[:, :, :n_keys], jnp.finfo(F32).min)


def _chunk_attention(qkv, rel_bias, gain):
    b, s, _ = qkv.shape
    bias = _attn_bias_table(rel_bias)
    back = ATTN_KEY_BLOCKS - 1
    key_specs = lambda col: [pl.BlockSpec((1, ATTN_TILE, ATTN_WIDTH),
                                          lambda i, c, j=j: (i, jnp.maximum(c + j - back, 0), col))
                             for j in range(ATTN_KEY_BLOCKS)]
    return pl.pallas_call(
        _attn_kernel,
        name="chunk_attention",
        grid=(b, s // ATTN_TILE),
        in_specs=[pl.BlockSpec((1, ATTN_TILE, ATTN_WIDTH), lambda i, c: (i, c, 0))] + key_specs(1) + key_specs(2)
        + [_const_spec(bias.shape), _const_spec((1, ATTN_WIDTH))],
        out_specs=pl.BlockSpec((1, ATTN_TILE, ATTN_WIDTH), lambda i, c: (i, c, 0)),
        out_shape=jax.ShapeDtypeStruct((b, s, ATTN_WIDTH), F32),
        compiler_params=_params("parallel", "parallel"),
    )(*([qkv] * (1 + 2 * ATTN_KEY_BLOCKS)), bias, gain.astype(F32).reshape(1, ATTN_WIDTH))


def _outproj_kernel(route, x_ref, orw_ref, os5_ref, oat_ref, w_ref, gpost_ref, gpre_ref, *refs):
    mixed = jnp.concatenate([orw_ref[...], os5_ref[...], oat_ref[...]], axis=1).astype(BF16)
    y = jnp.dot(mixed, w_ref[...], preferred_element_type=F32)
    x = x_ref[...] + _rms(y, gpost_ref[...])
    h = _rms(x, gpre_ref[...])
    if not route:
        x_out, h_out = refs
    else:
        router_ref, x_out, h_out, route_out = refs
        logits = _mm(_terms(h, 2), _terms(router_ref[...], 2))
        lane = lax.broadcasted_iota(jnp.int32, logits.shape, 1)
        neg = jnp.finfo(F32).min
        logits = jnp.where(lane < N_EXPERTS, logits, neg)
        top1 = jnp.max(logits, axis=-1, keepdims=True)
        idx1 = jnp.min(jnp.where(logits == top1, lane, LANES), axis=-1, keepdims=True)
        rest = jnp.where(lane == idx1, neg, logits)
        top2 = jnp.max(rest, axis=-1, keepdims=True)
        idx2 = jnp.min(jnp.where(rest == top2, lane, LANES), axis=-1, keepdims=True)
        e2 = jnp.exp(top2 - top1)
        route_out[...] = (jnp.where(lane == 0, idx1.astype(F32), 0.0) + jnp.where(lane == 1, idx2.astype(F32), 0.0)
                          + jnp.where(lane == 2, 1.0 / (1.0 + e2), 0.0) + jnp.where(lane == 3, e2 / (1.0 + e2), 0.0))
    x_out[...] = x
    h_out[...] = h.astype(h_out.dtype)


def _out_projection(x, o_rwkv, o_s5, o_attn, w_out, gain_post, gain_pre, router):
    t, d = x.shape
    row = lambda n: pl.BlockSpec((ROW_TILE, n), lambda i: (i, 0))
    vec = lambda g: g.astype(F32).reshape(1, d)
    ins = [x, o_rwkv, o_s5, o_attn, w_out.astype(BF16), vec(gain_post), vec(gain_pre)]
    in_specs = [row(d), row(o_rwkv.shape[1]), row(o_s5.shape[1]), row(o_attn.shape[1]),
                _const_spec(w_out.shape), _const_spec((1, d)), _const_spec((1, d))]
    out_specs = [row(d), row(d)]
    out_shape = [jax.ShapeDtypeStruct((t, d), F32), jax.ShapeDtypeStruct((t, d), BF16 if router is None else F32)]
    if router is not None:
        ins.append(jnp.pad(router.astype(F32), ((0, 0), (0, LANES - N_EXPERTS))))
        in_specs.append(_const_spec((d, LANES)))
        out_specs.append(row(LANES))
        out_shape.append(jax.ShapeDtypeStruct((t, LANES), F32))
    return pl.pallas_call(
        functools.partial(_outproj_kernel, router is not None),
        name="out_projection",
        grid=(t // ROW_TILE,),
        in_specs=in_specs, out_specs=out_specs, out_shape=out_shape,
        compiler_params=_params("parallel"),
    )(*ins)


def _ffn_kernel(x_ref, h_ref, wg_ref, wu_ref, wd_ref, gain_ref, o_ref, acc_ref):
    f = pl.program_id(1)

    @pl.when(f == 0)
    def _():
        acc_ref[...] = jnp.zeros_like(acc_ref)

    h = h_ref[...]
    act = jax.nn.silu(jnp.dot(h, wg_ref[...], preferred_element_type=F32)) * jnp.dot(h, wu_ref[...], preferred_element_type=F32)
    acc_ref[...] += jnp.dot(act.astype(BF16), wd_ref[...], preferred_element_type=F32)

    @pl.when(f == pl.num_programs(1) - 1)
    def _():
        o_ref[...] = x_ref[...] + _rms(acc_ref[...], gain_ref[...])


def _ffn(x, h, w_gate, w_up, w_down, gain):
    t, d = x.shape
    d_ff = w_gate.shape[1]
    row = lambda n: pl.BlockSpec((ROW_TILE, n), lambda i, f: (i, 0))
    return pl.pallas_call(
        _ffn_kernel,
        name="swiglu_ffn",
        grid=(t // ROW_TILE, d_ff // FFN_TILE),
        in_specs=[row(d), row(d),
                  pl.BlockSpec((d, FFN_TILE), lambda i, f: (0, f)),
                  pl.BlockSpec((d, FFN_TILE), lambda i, f: (0, f)),
                  pl.BlockSpec((FFN_TILE, d), lambda i, f: (f, 0)),
                  pl.BlockSpec((1, d), lambda i, f: (0, 0))],
        out_specs=row(d),
        out_shape=jax.ShapeDtypeStruct((t, d), F32),
        scratch_shapes=[pltpu.VMEM((ROW_TILE, d), F32)],
        compiler_params=_params("parallel", "arbitrary"),
    )(x, h, w_gate.astype(BF16), w_up.astype(BF16), w_down.astype(BF16), gain.astype(F32).reshape(1, d))


MOE_TILE = 512


def _moe_plan(route, n_experts):
    t = route.shape[0]
    n = 2 * t
    tm = MOE_TILE
    expert = route[:, :2].astype(jnp.int32).reshape(n)
    onehot = (expert[:, None] == jnp.arange(n_experts, dtype=jnp.int32)[None, :]).astype(jnp.int32)
    rank = jnp.sum(jnp.cumsum(onehot, axis=0) * onehot, axis=1) - 1
    counts = jnp.sum(onehot, axis=0)
    padded = (counts + tm - 1) // tm * tm
    ends = jnp.cumsum(padded)
    slot = jnp.sum((ends - padded)[None, :] * onehot, axis=1) + rank
    n_tiles = n // tm + n_experts
    slots = n_tiles * tm
    ids = jnp.arange(n, dtype=jnp.int32)
    every = jnp.arange(slots, dtype=jnp.int32)
    dump = n + every % tm + tm * ((every // tm) % 2)
    dst = dump.at[slot].set((ids % 2) * t + ids // 2, unique_indices=True)
    tok = jnp.where(dst < n, dst % t, 0)
    tile_start = jnp.arange(n_tiles, dtype=jnp.int32) * tm
    tile_expert = jnp.minimum(jnp.sum((tile_start[:, None] >= ends[None, :]).astype(jnp.int32), axis=1), n_experts - 1)
    tile_valid = (tile_start < ends[-1]).astype(jnp.int32)
    dst = dst.reshape(n_tiles, 1, tm)
    before_first = (n + tm + jnp.arange(tm, dtype=jnp.int32)).reshape(1, 1, tm)
    return tok.reshape(n_tiles, 1, tm), dst, jnp.concatenate([before_first, dst[:-1]], axis=0), tile_expert, tile_valid


def _moe_ffn_kernel(te_ref, tv_ref, tok0_ref, tokn_ref, dstp_ref, dst_ref, h_hbm, wg_ref, wu_ref, wd_ref,
                    out_hbm, xbuf, xb, ybuf, acc_ref, gsem, ssem):
    t, f = pl.program_id(0), pl.program_id(1)
    n_t, n_f = pl.num_programs(0), pl.num_programs(1)
    tm = MOE_TILE
    slot = t % 2
    valid = tv_ref[t] == 1
    prev_valid = tv_ref[jnp.maximum(t - 1, 0)] == 1
    last = t == n_t - 1

    gather_row = lambda tok_ref, s, i: pltpu.make_async_copy(
        h_hbm.at[pl.ds(tok_ref[0, 0, i], 1), :], xbuf.at[s, pl.ds(i, 1), :], gsem.at[s])
    scatter_row = lambda d_ref, s, i: pltpu.make_async_copy(
        ybuf.at[s, pl.ds(i, 1), :], out_hbm.at[pl.ds(d_ref[0, 0, i], 1), :], ssem.at[s])
    wait_gather = lambda s: pltpu.make_async_copy(h_hbm.at[pl.ds(0, tm), :], xbuf.at[s], gsem.at[s]).wait()
    wait_scatter = lambda s: pltpu.make_async_copy(ybuf.at[s], out_hbm.at[pl.ds(0, tm), :], ssem.at[s]).wait()

    def looped(copy, idx_ref, s):
        def body(i, carry):
            copy(idx_ref, s, i).start()
            return carry
        lax.fori_loop(0, tm, body, 0, unroll=8)

    def compute():
        x = xb[...]
        act = jax.nn.silu(jnp.dot(x, wg_ref[0], preferred_element_type=F32)) * jnp.dot(x, wu_ref[0], preferred_element_type=F32)
        return jnp.dot(act.astype(BF16), wd_ref[0], preferred_element_type=F32)

    @pl.when((f == 0) & (t == 0))
    def _():
        looped(gather_row, tok0_ref, 0)
        ybuf[1] = jnp.zeros(ybuf.shape[1:], F32)
        for bank in range(2):
            fill = pltpu.make_async_copy(ybuf.at[1], out_hbm.at[pl.ds(out_hbm.shape[0] - (bank + 1) * tm, tm), :],
                                         ssem.at[1])
            fill.start()
            fill.wait()

    @pl.when((f == 0) & ((t == 0) | prev_valid))
    def _():
        wait_gather(slot)

    @pl.when((f == 0) & valid)
    def _():
        xb[...] = xbuf[slot].astype(BF16)
        for i in range(tm):
            gather_row(tokn_ref, 1 - slot, i).start()
        acc_ref[...] = compute()

    @pl.when((f == n_f - 1) & ((t == 1) | ((t >= 2) & (tv_ref[jnp.maximum(t - 2, 0)] == 1))))
    def _():
        wait_scatter(slot)

    @pl.when((f == n_f - 1) & valid)
    def _():
        for i in range(tm):
            scatter_row(dstp_ref, 1 - slot, i).start()
        ybuf[slot] = acc_ref[...] + compute()

    @pl.when((f == n_f - 1) & jnp.logical_not(valid) & (t >= 1) & prev_valid)
    def _():
        looped(scatter_row, dstp_ref, 1 - slot)

    @pl.when((f == n_f - 1) & last)
    def _():
        @pl.when(valid)
        def _():
            looped(scatter_row, dst_ref, slot)
            wait_scatter(slot)
            wait_gather(1 - slot)

        @pl.when(prev_valid)
        def _():
            wait_scatter(1 - slot)


def _moe_combine_kernel(x_ref, y1_ref, y2_ref, route_ref, gain_ref, o_ref):
    y = route_ref[:, 2:3] * y1_ref[...] + route_ref[:, 3:4] * y2_ref[...]
    o_ref[...] = x_ref[...] + _rms(y, gain_ref[...])


def _moe(x, h, route, w_gate, w_up, w_down, gain):
    t, d = x.shape
    n_e, _, d_ff = w_gate.shape
    tm = MOE_TILE
    assert d_ff == 2 * FFN_TILE
    tok, dst, dst_prev, tile_expert, tile_valid = _moe_plan(route, n_e)
    n_tiles = tok.shape[0]
    idx_spec = lambda shift: pl.BlockSpec(
        (1, 1, tm), lambda i, f, te, tv: (jnp.clip(i + shift, 0, n_tiles - 1), 0, 0), memory_space=pltpu.SMEM)
    grid_spec = pltpu.PrefetchScalarGridSpec(
        num_scalar_prefetch=2,
        grid=(n_tiles, d_ff // FFN_TILE),
        in_specs=[idx_spec(0), idx_spec(1), idx_spec(0), idx_spec(0),
                  pl.BlockSpec(memory_space=pl.ANY),
                  pl.BlockSpec((1, d, FFN_TILE), lambda i, f, te, tv: (te[i], 0, f)),
                  pl.BlockSpec((1, d, FFN_TILE), lambda i, f, te, tv: (te[i], 0, f)),
                  pl.BlockSpec((1, FFN_TILE, d), lambda i, f, te, tv: (te[i], f, 0))],
        out_specs=pl.BlockSpec(memory_space=pl.ANY),
        scratch_shapes=[pltpu.VMEM((2, tm, d), F32), pltpu.VMEM((tm, d), BF16), pltpu.VMEM((2, tm, d), F32),
                        pltpu.VMEM((tm, d), F32), pltpu.SemaphoreType.DMA((2,)), pltpu.SemaphoreType.DMA((2,))],
    )
    routed = pl.pallas_call(
        _moe_ffn_kernel,
        name="moe_experts",
        grid_spec=grid_spec,
        out_shape=jax.ShapeDtypeStruct((2 * t + 2 * tm, d), F32),
        compiler_params=_params("arbitrary", "arbitrary"),
    )(tile_expert, tile_valid, tok, tok, dst_prev, dst, h, w_gate.astype(BF16), w_up.astype(BF16), w_down.astype(BF16))
    row = lambda n: pl.BlockSpec((ROW_TILE, n), lambda i: (i, 0))
    second = pl.BlockSpec((ROW_TILE, d), lambda i: (i + t // ROW_TILE, 0))
    return pl.pallas_call(
        _moe_combine_kernel,
        name="moe_combine",
        grid=(t // ROW_TILE,),
        in_specs=[row(d), row(d), second, row(LANES), _const_spec((1, d))],
        out_specs=row(d),
        out_shape=jax.ShapeDtypeStruct((t, d), F32),
        compiler_params=_params("parallel"),
    )(x, routed, routed, route, gain.astype(F32).reshape(1, d))


def kernel(x, norm_mix_pre, norm_mix_post, norm_ffn_pre, norm_ffn_post, w_in, w_out, rwkv_mu, rwkv_w0, rwkv_w_up, rwkv_a0, rwkv_a_up, rwkv_g_up, rwkv_k_k, rwkv_k_a, rwkv_r_k, rwkv_ln_w, rwkv_ln_b, s5_a_re, s5_a_im, s5_log_step, s5_b_re, s5_b_im, s5_c_re, s5_c_im, s5_d, s5_w_glu, s5_b_glu, s5_norm, attn_rel_bias, attn_norm, ffn_w_gate, ffn_w_up, ffn_w_down, moe_router, moe_w_gate, moe_w_up, moe_w_down):
    b, s, d = x.shape
    depth = w_in.shape[0]
    xt = x.reshape(b * s, d)
    for layer in range(depth):
        p, u, qkv = _in_projection(xt, norm_mix_pre[layer], w_in[layer])
        o_rwkv = _rwkv_mix(p.reshape(b, s, -1), rwkv_mu[layer], rwkv_w0[layer], rwkv_w_up[layer], rwkv_a0[layer],
                           rwkv_a_up[layer], rwkv_g_up[layer], rwkv_k_k[layer], rwkv_k_a[layer], rwkv_r_k[layer],
                           rwkv_ln_w[layer], rwkv_ln_b[layer])
        o_s5 = _s5_mix(u.reshape(b, s, -1), s5_a_re[layer], s5_a_im[layer], s5_log_step[layer], s5_b_re[layer],
                       s5_b_im[layer], s5_c_re[layer], s5_c_im[layer], s5_d[layer], s5_w_glu[layer],
                       s5_b_glu[layer], s5_norm[layer])
        o_attn = _chunk_attention(qkv.reshape(b, s, -1), attn_rel_bias[layer], attn_norm[layer])
        i = layer // 2
        moe = layer % 2 == 1
        outs = _out_projection(xt, o_rwkv.reshape(b * s, -1), o_s5.reshape(b * s, -1), o_attn.reshape(b * s, -1),
                               w_out[layer], norm_mix_post[layer], norm_ffn_pre[layer],
                               moe_router[i] if moe else None)
        if moe:
            xt, h, route = outs
            xt = _moe(xt, h, route, moe_w_gate[i], moe_w_up[i], moe_w_down[i], norm_ffn_post[layer])
        else:
            xt, h = outs
            xt = _ffn(xt, h, ffn_w_gate[i], ffn_w_up[i], ffn_w_down[i], norm_ffn_post[layer])
    return xt.reshape(b, s, d)
```

```python
import functools
import math

import jax
import jax.numpy as jnp
import numpy as np
from jax import lax
from jax.experimental import pallas as pl
from jax.experimental.pallas import tpu as pltpu

F32 = jnp.float32
BF16 = jnp.bfloat16

RMS_EPS = 1e-6
RWKV_GN_EPS = 64e-5
HEAD_DIM = 64
RWKV_HEADS = 6
RWKV_WIDTH = 384
RWKV_IN = 1408
S5_WIDTH = 256
S5_GROUPS = 16
S5_STATE = 64
S5_LANES = S5_GROUPS * S5_STATE
ATTN_HEADS = 6
ATTN_WIDTH = 384
ATTN_CHUNK = 64
ATTN_LEFT = 8
MAX_REL = 128
N_EXPERTS = 8
LANES = 128

RWKV_CHUNK = 64
RWKV_STEP_CHUNKS = 2
S5_CHUNK = 256
SUBLANES = 8
ATTN_TILE = 2 * ATTN_CHUNK
ATTN_KEY_BLOCKS = (ATTN_LEFT * ATTN_CHUNK) // ATTN_TILE + 1
ROW_TILE = 512
FFN_TILE = 1408
VMEM_LIMIT = 48 * 1024 * 1024
MOE_VMEM_LIMIT = 58 * 1024 * 1024

_HI = lax.Precision.HIGHEST
_CONTRACT = {"nn": (((1,), (0,)), ((), ())), "nt": (((1,), (1,)), ((), ())), "tn": (((0,), (0,)), ((), ()))}


def _hdot(a, b):
    return jnp.dot(a, b, precision=_HI, preferred_element_type=F32)


def _terms(x, n):
    out = []
    for i in range(n):
        t = x.astype(BF16)
        out.append(t)
        if i + 1 < n:
            x = x - t.astype(F32)
    return tuple(out)


def _mm(a, b, kind="nn"):
    order = max(len(a), len(b))
    acc = None
    for i, ai in enumerate(a):
        for j, bj in enumerate(b):
            if i + j < order:
                d = lax.dot_general(ai, bj, _CONTRACT[kind], preferred_element_type=F32)
                acc = d if acc is None else acc + d
    return acc


def _rms(x, gain):
    return x * lax.rsqrt(jnp.mean(x * x, axis=-1, keepdims=True) + RMS_EPS) * gain


def _params(*semantics):
    return pltpu.CompilerParams(dimension_semantics=semantics, vmem_limit_bytes=VMEM_LIMIT)


def _const_spec(shape):
    zeros = (0,) * len(shape)
    return pl.BlockSpec(shape, lambda *_: zeros)


def _inproj_kernel(x_ref, gain_ref, w_ref, p_ref, u_ref, qkv_ref):
    h = _rms(x_ref[...], gain_ref[...]).astype(BF16)
    s5_at, qkv_at = RWKV_IN, RWKV_IN + S5_WIDTH
    p_ref[...] = jnp.dot(h, w_ref[:, :s5_at], preferred_element_type=F32)
    u_ref[...] = jnp.dot(h, w_ref[:, s5_at:qkv_at], preferred_element_type=F32)
    qkv_ref[...] = jnp.dot(h, w_ref[:, qkv_at:], preferred_element_type=F32).astype(BF16)


def _in_projection(x, gain, w_in):
    t, d = x.shape
    n_in = w_in.shape[1]
    n_qkv = n_in - RWKV_IN - S5_WIDTH
    row = lambda n: pl.BlockSpec((ROW_TILE, n), lambda i: (i, 0))
    return pl.pallas_call(
        _inproj_kernel,
        name="in_projection",
        grid=(t // ROW_TILE,),
        in_specs=[row(d), _const_spec((1, d)), _const_spec((d, n_in))],
        out_specs=[row(RWKV_IN), row(S5_WIDTH), row(n_qkv)],
        out_shape=[jax.ShapeDtypeStruct((t, RWKV_IN), F32), jax.ShapeDtypeStruct((t, S5_WIDTH), F32),
                   jax.ShapeDtypeStruct((t, n_qkv), BF16)],
        compiler_params=_params("parallel"),
    )(x, gain.reshape(1, d), w_in.astype(BF16))


RWKV_TERMS = 1
HEADS_PER_VREG = LANES // HEAD_DIM


def _t(x):
    return _terms(x, RWKV_TERMS)


def _head_sums(x):
    first = lax.broadcasted_iota(jnp.int32, (1, LANES), 1) < HEAD_DIM
    outs = []
    for j in range(x.shape[1] // LANES):
        xp = x[:, j * LANES:(j + 1) * LANES]
        s0 = jnp.sum(jnp.where(first, xp, 0.0), axis=-1, keepdims=True)
        s1 = jnp.sum(jnp.where(first, 0.0, xp), axis=-1, keepdims=True)
        outs.append(jnp.where(first, s0, s1))
    return jnp.concatenate(outs, axis=1)


def _rwkv_kernel(p_ref, mu_ref, w0_ref, wup_ref, a0_ref, aup_ref, gup_ref, kk_ref, ka_ref, rk_ref,
                 lnw_ref, lnb_ref, o_ref, prev_ref, state_ref):
    nb, per_seq = p_ref.shape[0], p_ref.shape[1]
    n = RWKV_CHUNK
    rows = nb * per_seq
    items = rows // n
    w_at = 3 * RWKV_WIDTH
    pairs = RWKV_WIDTH // LANES

    @pl.when(pl.program_id(0) == 0)
    def _():
        prev_ref[...] = jnp.zeros_like(prev_ref)
        state_ref[...] = jnp.zeros_like(state_ref)

    p = p_ref[...].reshape(rows, RWKV_IN)
    row = lax.broadcasted_iota(jnp.int32, (rows, 1), 0)
    shifted = pltpu.roll(p, 1, axis=0)
    for b in range(nb):
        shifted = jnp.where(row == b * per_seq, prev_ref[b:b + 1, :], shifted)
    for b in range(nb):
        prev_ref[b:b + 1, :] = p[(b + 1) * per_seq - 1:(b + 1) * per_seq, :]
    p = p + (shifted - p) * mu_ref[...]

    r = p[:, :RWKV_WIDTH]
    k = p[:, RWKV_WIDTH:2 * RWKV_WIDTH]
    v = p[:, 2 * RWKV_WIDTH:w_at]
    w_lo = p[:, w_at:w_at + 64]
    a_lo = p[:, w_at + 64:w_at + 128]
    g_lo = p[:, w_at + 128:]

    lora = lambda x, w_ref: _mm(_t(x), tuple(w_ref[i] for i in range(RWKV_TERMS)))
    w = -jax.nn.softplus(-(w0_ref[...] + lora(jnp.tanh(w_lo), wup_ref))) - 0.5
    log_decay = -jnp.exp(w)
    a = jax.nn.sigmoid(a0_ref[...] + lora(a_lo, aup_ref))
    gate = lora(jax.nn.sigmoid(g_lo), gup_ref)

    kk = k * kk_ref[...]
    kk = kk / jnp.maximum(jnp.sqrt(_head_sums(kk * kk)), 1e-12)
    k = k * (1.0 + (a - 1.0) * ka_ref[...])

    s_i = lax.broadcasted_iota(jnp.int32, (rows, rows), 0)
    s_j = lax.broadcasted_iota(jnp.int32, (rows, rows), 1)
    causal = ((s_i // n == s_j // n) & (s_i >= s_j)).astype(BF16)
    cum = _mm((causal,), _terms(log_decay, 3))
    cum_end = cum[n - 1:n, :]
    for it in range(1, items):
        cum_end = jnp.where(row >= it * n, cum[(it + 1) * n - 1:(it + 1) * n, :], cum_end)
    inv_g = jnp.exp(-cum)
    to_end = jnp.exp(cum_end - cum)
    beta = kk * a
    r_bar = r * jnp.exp(cum)
    a_bar = -kk * jnp.exp(cum - log_decay)
    b_til = beta * inv_g
    k_til = k * inv_g
    b_end = beta * to_end
    k_end = k * to_end
    g_end = jnp.exp(cum_end)

    lane = lax.broadcasted_iota(jnp.int32, (1, LANES), 1)
    q_i = lax.broadcasted_iota(jnp.int32, (n, LANES), 0)
    q_j = lax.broadcasted_iota(jnp.int32, (n, LANES), 1) % n
    lower = q_i >= q_j
    strictly_lower = q_i > q_j
    eye = (lax.broadcasted_iota(jnp.int32, (n, n), 0) == lax.broadcasted_iota(jnp.int32, (n, n), 1)).astype(F32)
    c_i = lax.broadcasted_iota(jnp.int32, (LANES, LANES), 0)
    c_j = lax.broadcasted_iota(jnp.int32, (LANES, LANES), 1)
    same_head = c_i // HEAD_DIM == c_j // HEAD_DIM
    diagonal = c_i == c_j
    zeros = jnp.zeros((n, LANES), F32)

    pair_ids = [(it, j) for it in range(items) for j in range(pairs)]
    head_ids = [(pi, h) for pi in range(len(pair_ids)) for h in range(HEADS_PER_VREG)]
    heads_of = lambda pi: [hi for hi, (p_i, _) in enumerate(head_ids) if p_i == pi]
    cut = lambda x, pi: x[pair_ids[pi][0] * n:(pair_ids[pi][0] + 1) * n,
                          pair_ids[pi][1] * LANES:(pair_ids[pi][1] + 1) * LANES]
    rb = [cut(r_bar, pi) for pi in range(len(pair_ids))]
    be = [cut(b_end, pi) for pi in range(len(pair_ids))]
    vv = [cut(v, pi) for pi in range(len(pair_ids))]
    bk = [_t(jnp.concatenate([cut(b_til, pi), cut(k_til, pi)], axis=0)) for pi in range(len(pair_ids))]
    mine = [lane // HEAD_DIM == h for _, h in head_ids]
    abm = [jnp.where(mine[hi], cut(a_bar, pi), 0.0) for hi, (pi, _) in enumerate(head_ids)]
    v0 = [_t(jnp.concatenate([zeros, jnp.where(mine[hi], vv[pi], 0.0)], axis=0)) for hi, (pi, _) in enumerate(head_ids)]
    m = [_mm(_t(jnp.concatenate([abm[hi], jnp.where(mine[hi], rb[pi], 0.0)], axis=0)), bk[pi], "nt")
         for hi, (pi, _) in enumerate(head_ids)]
    top = [jnp.where(strictly_lower, x[:n], 0.0) for x in m]
    bot = [jnp.where(lower, x[n:], 0.0) for x in m]
    y = [_mm(_t(top[hi]), v0[hi]) for hi in range(len(head_ids))]
    power = [x[:, :n] for x in top]
    inv = [eye + x for x in power]
    for _ in range(int(math.log2(n)) - 1):
        pt = [_t(x) for x in power]
        power = [_mm(x, x) for x in pt]
        inv = [x + _mm(_t(x), _t(pw)) for x, pw in zip(inv, power)]
    wu = [_mm(_t(inv[hi]), _t(jnp.concatenate([abm[hi], y[hi]], axis=1))) for hi in range(len(head_ids))]
    v_low = [tuple(jnp.concatenate([jnp.zeros((n, LANES), BF16), t[n:]], axis=1) for t in v0[hi])
             for hi in range(len(head_ids))]
    ro = [_mm(_t(bot[hi]), tuple(jnp.concatenate([a, b2], axis=0) for a, b2 in zip(_t(wu[hi]), v_low[hi])))
          for hi in range(len(head_ids))]
    both = lambda xs, pi, sl: sum(xs[hi][:, sl] for hi in heads_of(pi))
    first, second = slice(0, LANES), slice(LANES, 2 * LANES)
    g_t = [jnp.where(same_head, _mm(_t(be[pi]), _t(both(wu, pi, first)), "tn"), 0.0)
           + jnp.where(diagonal, cut(g_end, pi)[:1], 0.0) for pi in range(len(pair_ids))]
    s2_t = [_mm(_t(jnp.concatenate([be[pi], cut(k_end, pi)], axis=0)),
                _t(jnp.concatenate([both(wu, pi, second), vv[pi]], axis=0)), "tn") for pi in range(len(pair_ids))]
    r2 = [_t(rb[pi] + both(ro, pi, first)) for pi in range(len(pair_ids))]
    out = [None] * len(pair_ids)
    per_step = items // nb
    for b in range(nb):
        for j in range(pairs):
            state = state_ref[b * pairs + j]
            for c in range(per_step):
                pi = (b * per_step + c) * pairs + j
                out[pi] = _mm(r2[pi], _t(state)) + both(ro, pi, second)
                state = _mm(_t(g_t[pi]), _t(state)) + jnp.where(same_head, s2_t[pi], 0.0)
            state_ref[b * pairs + j] = state
    o = jnp.concatenate([jnp.concatenate(out[it * pairs:(it + 1) * pairs], axis=1) for it in range(items)], axis=0)

    inv_n = 1.0 / HEAD_DIM
    cen = o - _head_sums(o) * inv_n
    var = _head_sums(cen * cen) * inv_n
    o = cen * lax.rsqrt(var + RWKV_GN_EPS) * lnw_ref[...] + lnb_ref[...]
    o = o + _head_sums(r * k * rk_ref[...]) * v
    o_ref[...] = (o * gate).reshape(nb, per_seq, RWKV_WIDTH)


def _rwkv_mix(p, mu, w0, w_up, a0, a_up, g_up, k_k, k_a, r_k, ln_w, ln_b):
    b, s, _ = p.shape
    per_step = RWKV_CHUNK * RWKV_STEP_CHUNKS
    vec = lambda t: t.astype(F32).reshape(1, -1)
    split = lambda t: jnp.stack(_terms(t.astype(F32), RWKV_TERMS))
    consts = [vec(mu), vec(w0), split(w_up), vec(a0), split(a_up), split(g_up), vec(k_k), vec(k_a), vec(r_k),
              vec(ln_w), vec(ln_b)]
    return pl.pallas_call(
        _rwkv_kernel,
        name="rwkv7",
        grid=(s // per_step,),
        in_specs=[pl.BlockSpec((b, per_step, RWKV_IN), lambda c: (0, c, 0))]
        + [_const_spec(t.shape) for t in consts],
        out_specs=pl.BlockSpec((b, per_step, RWKV_WIDTH), lambda c: (0, c, 0)),
        out_shape=jax.ShapeDtypeStruct((b, s, RWKV_WIDTH), F32),
        scratch_shapes=[pltpu.VMEM((b, RWKV_IN), F32),
                        pltpu.VMEM((b * RWKV_WIDTH // LANES, LANES, LANES), F32)],
        compiler_params=_params("arbitrary"),
    )(p, *consts)


def _s5_prep_kernel(are_ref, aim_ref, lstep_ref, bre_ref, bim_ref, bmat_ref, pw_ref):
    lam_re = jnp.minimum(are_ref[...], -1e-4)
    lam_im = aim_ref[...]
    step = jnp.exp(lstep_ref[...])
    z_re, z_im = lam_re * step, lam_im * step
    mag = jnp.exp(z_re)
    num_re, num_im = mag * jnp.cos(z_im) - 1.0, mag * jnp.sin(z_im)
    den = lam_re * lam_re + lam_im * lam_im
    coef_re = (num_re * lam_re + num_im * lam_im) / den
    coef_im = (num_im * lam_re - num_re * lam_im) / den
    bre, bim = bre_ref[...], bim_ref[...]
    bmat_ref[:, :S5_LANES] = (coef_re * bre - coef_im * bim).astype(bmat_ref.dtype)
    bmat_ref[:, S5_LANES:] = (coef_re * bim + coef_im * bre).astype(bmat_ref.dtype)
    steps = (lax.broadcasted_iota(jnp.int32, (SUBLANES, 1), 0) + 1).astype(F32)
    pmag = jnp.exp(steps * z_re)
    pw_ref[:, :S5_LANES] = pmag * jnp.cos(steps * z_im)
    pw_ref[:, S5_LANES:] = pmag * jnp.sin(steps * z_im)


def _s5_kernel(u_ref, bmat_ref, pw_ref, cre_ref, cim_ref, d_ref, wglu_ref, bglu_ref, gain_ref, o_ref, carry_ref):
    n = S5_CHUNK

    @pl.when(pl.program_id(1) == 0)
    def _():
        carry_ref[...] = jnp.zeros_like(carry_ref)

    u = u_ref[0]
    bu = jnp.dot(u.astype(BF16), bmat_ref[...], preferred_element_type=F32)
    x_re, x_im = bu[:, :S5_LANES], bu[:, S5_LANES:]
    sub = lax.broadcasted_iota(jnp.int32, (n, 1), 0) % SUBLANES
    dist = 1
    while dist < SUBLANES:
        l_re = pw_ref[dist - 1:dist, :S5_LANES]
        l_im = pw_ref[dist - 1:dist, S5_LANES:]
        keep = sub >= dist
        s_re = jnp.where(keep, pltpu.roll(x_re, dist, axis=0), 0.0)
        s_im = jnp.where(keep, pltpu.roll(x_im, dist, axis=0), 0.0)
        x_re, x_im = x_re + l_re * s_re - l_im * s_im, x_im + l_re * s_im + l_im * s_re
        dist *= 2
    c_re, c_im = carry_ref[:, :S5_LANES], carry_ref[:, S5_LANES:]
    p_re, p_im = pw_ref[:, :S5_LANES], pw_ref[:, S5_LANES:]
    groups_re, groups_im = [], []
    for g in range(n // SUBLANES):
        g_re, g_im = x_re[g * SUBLANES:(g + 1) * SUBLANES], x_im[g * SUBLANES:(g + 1) * SUBLANES]
        g_re, g_im = g_re + p_re * c_re - p_im * c_im, g_im + p_re * c_im + p_im * c_re
        c_re, c_im = g_re[SUBLANES - 1:], g_im[SUBLANES - 1:]
        groups_re.append(g_re)
        groups_im.append(g_im)
    carry_ref[:, :S5_LANES] = c_re
    carry_ref[:, S5_LANES:] = c_im
    x_re, x_im = jnp.concatenate(groups_re, axis=0), jnp.concatenate(groups_im, axis=0)

    y = (jnp.dot(x_re.astype(BF16), cre_ref[...], preferred_element_type=F32)
         - jnp.dot(x_im.astype(BF16), cim_ref[...], preferred_element_type=F32)) + d_ref[...] * u
    y = jax.nn.gelu(y)
    y = y * jax.nn.sigmoid(jnp.dot(y.astype(BF16), wglu_ref[...], preferred_element_type=F32) + bglu_ref[...])
    o_ref[0] = _rms(y, gain_ref[...])


def _block_diag(t):
    g, rows, cols = t.shape
    return jnp.einsum('grc,gh->grhc', t, jnp.eye(g, dtype=t.dtype)).reshape(g * rows, g * cols)


def _s5_mix(u, a_re, a_im, log_step, b_re, b_im, c_re, c_im, d, w_glu, b_glu, gain):
    b, s, width = u.shape
    lane = lambda t: t.astype(F32).reshape(1, S5_LANES)
    to_cp = lambda t: _block_diag(jnp.swapaxes(t.astype(F32), 1, 2))
    bmat, pw = pl.pallas_call(
        _s5_prep_kernel,
        name="s5_discretise",
        out_shape=[jax.ShapeDtypeStruct((width, 2 * S5_LANES), BF16),
                   jax.ShapeDtypeStruct((SUBLANES, 2 * S5_LANES), F32)],
        compiler_params=pltpu.CompilerParams(vmem_limit_bytes=VMEM_LIMIT),
    )(lane(a_re), lane(a_im), lane(jnp.repeat(log_step[:, None], S5_STATE, axis=1)), to_cp(b_re), to_cp(b_im))
    cre = _block_diag(jnp.swapaxes(c_re, 1, 2)).astype(BF16)
    cim = _block_diag(jnp.swapaxes(c_im, 1, 2)).astype(BF16)
    vec = lambda t: t.astype(F32).reshape(1, width)
    consts = [bmat, pw, cre, cim, vec(d), w_glu.astype(BF16), vec(b_glu), vec(gain)]
    return pl.pallas_call(
        _s5_kernel,
        name="s5_scan",
        grid=(b, s // S5_CHUNK),
        in_specs=[pl.BlockSpec((1, S5_CHUNK, width), lambda i, c: (i, c, 0))]
        + [_const_spec(t.shape) for t in consts],
        out_specs=pl.BlockSpec((1, S5_CHUNK, width), lambda i, c: (i, c, 0)),
        out_shape=jax.ShapeDtypeStruct((b, s, width), F32),
        scratch_shapes=[pltpu.VMEM((1, 2 * S5_LANES), F32)],
        compiler_params=_params("parallel", "arbitrary"),
    )(u, *consts)


def _attn_kernel(q_ref, *refs):
    k_refs = refs[:ATTN_KEY_BLOCKS]
    v_refs = refs[ATTN_KEY_BLOCKS:2 * ATTN_KEY_BLOCKS]
    bias_ref, gain_ref, o_ref = refs[2 * ATTN_KEY_BLOCKS:]
    n_keys = ATTN_KEY_BLOCKS * ATTN_TILE
    q = q_ref[0] * (HEAD_DIM ** -0.5)
    k = jnp.concatenate([kr[0] for kr in k_refs], axis=0)
    v = jnp.concatenate([vr[0] for vr in v_refs], axis=0)
    key_row = (pl.program_id(1) - (ATTN_KEY_BLOCKS - 1)) * ATTN_TILE + lax.broadcasted_iota(jnp.int32, (1, n_keys), 1)
    neg = jnp.finfo(F32).min
    masked = (bias_ref[0] == neg) | (key_row < 0)
    heads = [slice(h * HEAD_DIM, (h + 1) * HEAD_DIM) for h in range(ATTN_HEADS)]
    s = [lax.dot_general(q[:, sl], k[:, sl], _CONTRACT["nt"], preferred_element_type=F32) for sl in heads]
    s = [jnp.where(masked, neg, x + bias_ref[h]) for h, x in enumerate(s)]
    e = [jnp.exp(x - jnp.max(x, axis=-1, keepdims=True)) for x in s]
    pv = [jnp.dot(x.astype(BF16), v[:, sl], preferred_element_type=F32) for x, sl in zip(e, heads)]
    outs = [x / jnp.sum(y, axis=-1, keepdims=True) for x, y in zip(pv, e)]
    o_ref[0] = _rms(jnp.concatenate(outs, axis=1), gain_ref[...])


def _attn_bias_table(rel_bias):
    n_keys = ATTN_KEY_BLOCKS * ATTN_TILE
    left = ATTN_LEFT * ATTN_CHUNK
    period = n_keys + ATTN_TILE
    rb = rel_bias.astype(F32)
    heads = rb.shape[0]
    far = lambda width: jnp.broadcast_to(rb[:, 2 * MAX_REL:], (heads, width))
    g = jnp.concatenate([far(left - MAX_REL), rb[:, 1:][:, ::-1], far(period - left - MAX_REL)], axis=1)
    toeplitz = jnp.tile(g, (1, ATTN_TILE))[:, :ATTN_TILE * (period - 1)].reshape(heads, ATTN_TILE, period - 1)
    q = np.arange(ATTN_TILE)[:, None]
    j = np.arange(n_keys)[None, :]
    in_band = (j // ATTN_CHUNK >= q // ATTN_CHUNK) & (j // ATTN_CHUNK <= q // ATTN_CHUNK + ATTN_LEFT)
    return jnp.where(in_band[None], toeplitz[:, :, :n_keys], jnp.finfo(F32).min)


def _chunk_attention(qkv, rel_bias, gain):
    b, s, _ = qkv.shape
    bias = _attn_bias_table(rel_bias)
    back = ATTN_KEY_BLOCKS - 1
    key_specs = lambda col: [pl.BlockSpec((1, ATTN_TILE, ATTN_WIDTH),
                                          lambda i, c, j=j: (i, jnp.maximum(c + j - back, 0), col))
                             for j in range(ATTN_KEY_BLOCKS)]
    return pl.pallas_call(
        _attn_kernel,
        name="chunk_attention",
        grid=(b, s // ATTN_TILE),
        in_specs=[pl.BlockSpec((1, ATTN_TILE, ATTN_WIDTH), lambda i, c: (i, c, 0))] + key_specs(1) + key_specs(2)
        + [_const_spec(bias.shape), _const_spec((1, ATTN_WIDTH))],
        out_specs=pl.BlockSpec((1, ATTN_TILE, ATTN_WIDTH), lambda i, c: (i, c, 0)),
        out_shape=jax.ShapeDtypeStruct((b, s, ATTN_WIDTH), F32),
        compiler_params=_params("parallel", "parallel"),
    )(*([qkv] * (1 + 2 * ATTN_KEY_BLOCKS)), bias, gain.astype(F32).reshape(1, ATTN_WIDTH))


def _outproj_kernel(route, x_ref, orw_ref, os5_ref, oat_ref, w_ref, gpost_ref, gpre_ref, *refs):
    mixed = jnp.concatenate([orw_ref[...], os5_ref[...], oat_ref[...]], axis=1).astype(BF16)
    y = jnp.dot(mixed, w_ref[...], preferred_element_type=F32)
    x = x_ref[...] + _rms(y, gpost_ref[...])
    h = _rms(x, gpre_ref[...])
    if not route:
        x_out, h_out = refs
    else:
        router_ref, x_out, h_out, route_out = refs
        logits = _mm(_terms(h, 2), _terms(router_ref[...], 2))
        lane = lax.broadcasted_iota(jnp.int32, logits.shape, 1)
        neg = jnp.finfo(F32).min
        logits = jnp.where(lane < N_EXPERTS, logits, neg)
        top1 = jnp.max(logits, axis=-1, keepdims=True)
        idx1 = jnp.min(jnp.where(logits == top1, lane, LANES), axis=-1, keepdims=True)
        rest = jnp.where(lane == idx1, neg, logits)
        top2 = jnp.max(rest, axis=-1, keepdims=True)
        idx2 = jnp.min(jnp.where(rest == top2, lane, LANES), axis=-1, keepdims=True)
        e2 = jnp.exp(top2 - top1)
        route_out[...] = (jnp.where(lane == 0, idx1.astype(F32), 0.0) + jnp.where(lane == 1, idx2.astype(F32), 0.0)
                          + jnp.where(lane == 2, 1.0 / (1.0 + e2), 0.0) + jnp.where(lane == 3, e2 / (1.0 + e2), 0.0))
    x_out[...] = x
    h_out[...] = h.astype(h_out.dtype)


def _out_projection(x, o_rwkv, o_s5, o_attn, w_out, gain_post, gain_pre, router):
    t, d = x.shape
    row = lambda n: pl.BlockSpec((ROW_TILE, n), lambda i: (i, 0))
    vec = lambda g: g.astype(F32).reshape(1, d)
    ins = [x, o_rwkv, o_s5, o_attn, w_out.astype(BF16), vec(gain_post), vec(gain_pre)]
    in_specs = [row(d), row(o_rwkv.shape[1]), row(o_s5.shape[1]), row(o_attn.shape[1]),
                _const_spec(w_out.shape), _const_spec((1, d)), _const_spec((1, d))]
    out_specs = [row(d), row(d)]
    out_shape = [jax.ShapeDtypeStruct((t, d), F32), jax.ShapeDtypeStruct((t, d), BF16 if router is None else F32)]
    if router is not None:
        ins.append(jnp.pad(router.astype(F32), ((0, 0), (0, LANES - N_EXPERTS))))
        in_specs.append(_const_spec((d, LANES)))
        out_specs.append(row(LANES))
        out_shape.append(jax.ShapeDtypeStruct((t, LANES), F32))
    return pl.pallas_call(
        functools.partial(_outproj_kernel, router is not None),
        name="out_projection",
        grid=(t // ROW_TILE,),
        in_specs=in_specs, out_specs=out_specs, out_shape=out_shape,
        compiler_params=_params("parallel"),
    )(*ins)


def _ffn_kernel(x_ref, h_ref, wg_ref, wu_ref, wd_ref, gain_ref, o_ref):
    h = h_ref[...]
    y = None
    for f in range(wg_ref.shape[1] // FFN_TILE):
        cols = slice(f * FFN_TILE, (f + 1) * FFN_TILE)
        act = (jax.nn.silu(jnp.dot(h, wg_ref[:, cols], preferred_element_type=F32))
               * jnp.dot(h, wu_ref[:, cols], preferred_element_type=F32))
        part = jnp.dot(act.astype(BF16), wd_ref[cols, :], preferred_element_type=F32)
        y = part if y is None else y + part
    o_ref[...] = x_ref[...] + _rms(y, gain_ref[...])


def _ffn(x, h, w_gate, w_up, w_down, gain):
    t, d = x.shape
    d_ff = w_gate.shape[1]
    row = lambda n: pl.BlockSpec((ROW_TILE, n), lambda i: (i, 0))
    resident = lambda shape: pl.BlockSpec(shape, lambda i: (0, 0), pipeline_mode=pl.Buffered(1))
    return pl.pallas_call(
        _ffn_kernel,
        name="swiglu_ffn",
        grid=(t // ROW_TILE,),
        in_specs=[row(d), row(d), resident((d, d_ff)), resident((d, d_ff)), resident((d_ff, d)), _const_spec((1, d))],
        out_specs=row(d),
        out_shape=jax.ShapeDtypeStruct((t, d), F32),
        compiler_params=_params("parallel"),
    )(x, h, w_gate.astype(BF16), w_up.astype(BF16), w_down.astype(BF16), gain.astype(F32).reshape(1, d))


MOE_TILE = 512


def _moe_plan(route, n_experts):
    t = route.shape[0]
    n = 2 * t
    tm = MOE_TILE
    expert = route[:, :2].astype(jnp.int32).reshape(n)
    onehot = (expert[:, None] == jnp.arange(n_experts, dtype=jnp.int32)[None, :]).astype(jnp.int32)
    rank = jnp.sum(jnp.cumsum(onehot, axis=0) * onehot, axis=1) - 1
    counts = jnp.sum(onehot, axis=0)
    padded = (counts + tm - 1) // tm * tm
    ends = jnp.cumsum(padded)
    slot = jnp.sum((ends - padded)[None, :] * onehot, axis=1) + rank
    n_tiles = n // tm + n_experts
    slots = n_tiles * tm
    ids = jnp.arange(n, dtype=jnp.int32)
    every = jnp.arange(slots, dtype=jnp.int32)
    dump = n + every % tm + tm * ((every // tm) % 2)
    dst = dump.at[slot].set((ids % 2) * t + ids // 2, unique_indices=True)
    tok = jnp.where(dst < n, dst % t, 0)
    tile_start = jnp.arange(n_tiles, dtype=jnp.int32) * tm
    tile_expert = jnp.minimum(jnp.sum((tile_start[:, None] >= ends[None, :]).astype(jnp.int32), axis=1), n_experts - 1)
    tile_valid = (tile_start < ends[-1]).astype(jnp.int32)
    dst = dst.reshape(n_tiles, 1, tm)
    before_first = (n + tm + jnp.arange(tm, dtype=jnp.int32)).reshape(1, 1, tm)
    return tok.reshape(n_tiles, 1, tm), dst, jnp.concatenate([before_first, dst[:-1]], axis=0), tile_expert, tile_valid


def _moe_ffn_kernel(te_ref, tv_ref, tok0_ref, tokn_ref, dstp_ref, dst_ref, h_hbm, wg_ref, wu_ref, wd_ref,
                    out_hbm, xbuf, xb, ybuf, gsem, ssem):
    t = pl.program_id(0)
    n_t = pl.num_programs(0)
    tm = MOE_TILE
    slot = t % 2
    valid = tv_ref[t] == 1
    prev_valid = tv_ref[jnp.maximum(t - 1, 0)] == 1

    gather_row = lambda tok_ref, s, i: pltpu.make_async_copy(
        h_hbm.at[pl.ds(tok_ref[0, 0, i], 1), :], xbuf.at[s, pl.ds(i, 1), :], gsem.at[s])
    scatter_row = lambda d_ref, i: pltpu.make_async_copy(
        ybuf.at[pl.ds(i, 1), :], out_hbm.at[pl.ds(d_ref[0, 0, i], 1), :], ssem.at[0])
    wait_gather = lambda s: pltpu.make_async_copy(h_hbm.at[pl.ds(0, tm), :], xbuf.at[s], gsem.at[s]).wait()
    wait_scatter = lambda: pltpu.make_async_copy(ybuf, out_hbm.at[pl.ds(0, tm), :], ssem.at[0]).wait()

    def looped(start_row):
        def body(i, carry):
            start_row(i)
            return carry
        lax.fori_loop(0, tm, body, 0, unroll=8)

    @pl.when(t == 0)
    def _():
        looped(lambda i: gather_row(tok0_ref, 0, i).start())
        ybuf[...] = jnp.zeros_like(ybuf)
        for bank in range(2):
            fill = pltpu.make_async_copy(ybuf, out_hbm.at[pl.ds(out_hbm.shape[0] - (bank + 1) * tm, tm), :], ssem.at[0])
            fill.start()
            fill.wait()

    @pl.when((t == 0) | prev_valid)
    def _():
        wait_gather(slot)

    @pl.when(valid)
    def _():
        xb[...] = xbuf[slot].astype(BF16)
        for i in range(tm):
            gather_row(tokn_ref, 1 - slot, i).start()
        for i in range(tm):
            scatter_row(dstp_ref, i).start()
        x = xb[...]
        y = None
        for f in range(wg_ref.shape[2] // FFN_TILE):
            cols = slice(f * FFN_TILE, (f + 1) * FFN_TILE)
            act = (jax.nn.silu(jnp.dot(x, wg_ref[0, :, cols], preferred_element_type=F32))
                   * jnp.dot(x, wu_ref[0, :, cols], preferred_element_type=F32))
            part = jnp.dot(act.astype(BF16), wd_ref[0, cols, :], preferred_element_type=F32)
            y = part if y is None else y + part
        wait_scatter()
        ybuf[...] = y

    @pl.when(jnp.logical_not(valid) & (t >= 1) & prev_valid)
    def _():
        looped(lambda i: scatter_row(dstp_ref, i).start())
        wait_scatter()

    @pl.when((t == n_t - 1) & valid)
    def _():
        looped(lambda i: scatter_row(dst_ref, i).start())
        wait_scatter()
        wait_gather(1 - slot)


def _moe_combine_kernel(x_ref, y1_ref, y2_ref, route_ref, gain_ref, o_ref):
    y = route_ref[:, 2:3] * y1_ref[...] + route_ref[:, 3:4] * y2_ref[...]
    o_ref[...] = x_ref[...] + _rms(y, gain_ref[...])


def _moe(x, h, route, w_gate, w_up, w_down, gain):
    t, d = x.shape
    n_e, _, d_ff = w_gate.shape
    tm = MOE_TILE
    tok, dst, dst_prev, tile_expert, tile_valid = _moe_plan(route, n_e)
    n_tiles = tok.shape[0]
    idx_spec = lambda shift: pl.BlockSpec(
        (1, 1, tm), lambda i, te, tv: (jnp.clip(i + shift, 0, n_tiles - 1), 0, 0), memory_space=pltpu.SMEM)
    grid_spec = pltpu.PrefetchScalarGridSpec(
        num_scalar_prefetch=2,
        grid=(n_tiles,),
        in_specs=[idx_spec(0), idx_spec(1), idx_spec(0), idx_spec(0),
                  pl.BlockSpec(memory_space=pl.ANY),
                  pl.BlockSpec((1, d, d_ff), lambda i, te, tv: (te[i], 0, 0)),
                  pl.BlockSpec((1, d, d_ff), lambda i, te, tv: (te[i], 0, 0)),
                  pl.BlockSpec((1, d_ff, d), lambda i, te, tv: (te[i], 0, 0))],
        out_specs=pl.BlockSpec(memory_space=pl.ANY),
        scratch_shapes=[pltpu.VMEM((2, tm, d), F32), pltpu.VMEM((tm, d), BF16), pltpu.VMEM((tm, d), F32),
                        pltpu.SemaphoreType.DMA((2,)), pltpu.SemaphoreType.DMA((1,))],
    )
    routed = pl.pallas_call(
        _moe_ffn_kernel,
        name="moe_experts",
        grid_spec=grid_spec,
        out_shape=jax.ShapeDtypeStruct((2 * t + 2 * tm, d), F32),
        compiler_params=pltpu.CompilerParams(dimension_semantics=("arbitrary",), vmem_limit_bytes=MOE_VMEM_LIMIT),
    )(tile_expert, tile_valid, tok, tok, dst_prev, dst, h, w_gate.astype(BF16), w_up.astype(BF16), w_down.astype(BF16))
    row = lambda n: pl.BlockSpec((ROW_TILE, n), lambda i: (i, 0))
    second = pl.BlockSpec((ROW_TILE, d), lambda i: (i + t // ROW_TILE, 0))
    return pl.pallas_call(
        _moe_combine_kernel,
        name="moe_combine",
        grid=(t // ROW_TILE,),
        in_specs=[row(d), row(d), second, row(LANES), _const_spec((1, d))],
        out_specs=row(d),
        out_shape=jax.ShapeDtypeStruct((t, d), F32),
        compiler_params=_params("parallel"),
    )(x, routed, routed, route, gain.astype(F32).reshape(1, d))


def kernel(x, norm_mix_pre, norm_mix_post, norm_ffn_pre, norm_ffn_post, w_in, w_out, rwkv_mu, rwkv_w0, rwkv_w_up, rwkv_a0, rwkv_a_up, rwkv_g_up, rwkv_k_k, rwkv_k_a, rwkv_r_k, rwkv_ln_w, rwkv_ln_b, s5_a_re, s5_a_im, s5_log_step, s5_b_re, s5_b_im, s5_c_re, s5_c_im, s5_d, s5_w_glu, s5_b_glu, s5_norm, attn_rel_bias, attn_norm, ffn_w_gate, ffn_w_up, ffn_w_down, moe_router, moe_w_gate, moe_w_up, moe_w_down):
    b, s, d = x.shape
    depth = w_in.shape[0]
    xt = x.reshape(b * s, d)
    for layer in range(depth):
        p, u, qkv = _in_projection(xt, norm_mix_pre[layer], w_in[layer])
        o_rwkv = _rwkv_mix(p.reshape(b, s, -1), rwkv_mu[layer], rwkv_w0[layer], rwkv_w_up[layer], rwkv_a0[layer],
                           rwkv_a_up[layer], rwkv_g_up[layer], rwkv_k_k[layer], rwkv_k_a[layer], rwkv_r_k[layer],
                           rwkv_ln_w[layer], rwkv_ln_b[layer])
        o_s5 = _s5_mix(u.reshape(b, s, -1), s5_a_re[layer], s5_a_im[layer], s5_log_step[layer], s5_b_re[layer],
                       s5_b_im[layer], s5_c_re[layer], s5_c_im[layer], s5_d[layer], s5_w_glu[layer],
                       s5_b_glu[layer], s5_norm[layer])
        o_attn = _chunk_attention(qkv.reshape(b, s, -1), attn_rel_bias[layer], attn_norm[layer])
        i = layer // 2
        moe = layer % 2 == 1
        outs = _out_projection(xt, o_rwkv.reshape(b * s, -1), o_s5.reshape(b * s, -1), o_attn.reshape(b * s, -1),
                               w_out[layer], norm_mix_post[layer], norm_ffn_pre[layer],
                               moe_router[i] if moe else None)
        if moe:
            xt, h, route = outs
            xt = _moe(xt, h, route, moe_w_gate[i], moe_w_up[i], moe_w_down[i], norm_ffn_post[layer])
        else:
            xt, h = outs
            xt = _ffn(xt, h, ffn_w_gate[i], ffn_w_up[i], ffn_w_down[i], norm_ffn_post[layer])
    return xt.reshape(b, s, d)
```

```python
import functools
import math

import jax
import jax.numpy as jnp
import numpy as np
from jax import lax
from jax.experimental import pallas as pl
from jax.experimental.pallas import tpu as pltpu

F32 = jnp.float32
BF16 = jnp.bfloat16

RMS_EPS = 1e-6
RWKV_GN_EPS = 64e-5
HEAD_DIM = 64
RWKV_HEADS = 6
RWKV_WIDTH = 384
RWKV_IN = 1408
S5_WIDTH = 256
S5_GROUPS = 16
S5_STATE = 64
S5_LANES = S5_GROUPS * S5_STATE
ATTN_HEADS = 6
ATTN_WIDTH = 384
ATTN_CHUNK = 64
ATTN_LEFT = 8
MAX_REL = 128
N_EXPERTS = 8
LANES = 128

RWKV_CHUNK = 64
RWKV_STEP_CHUNKS = 2
S5_CHUNK = 256
SUBLANES = 8
ATTN_TILE = 2 * ATTN_CHUNK
ATTN_KEY_BLOCKS = (ATTN_LEFT * ATTN_CHUNK) // ATTN_TILE + 1
ATTN_STEP_TILES = 2
ROW_TILE = 512
FFN_TILE = 1408
VMEM_LIMIT = 48 * 1024 * 1024

_HI = lax.Precision.HIGHEST
_CONTRACT = {"nn": (((1,), (0,)), ((), ())), "nt": (((1,), (1,)), ((), ())), "tn": (((0,), (0,)), ((), ()))}


def _hdot(a, b):
    return jnp.dot(a, b, precision=_HI, preferred_element_type=F32)


def _terms(x, n):
    out = []
    for i in range(n):
        t = x.astype(BF16)
        out.append(t)
        if i + 1 < n:
            x = x - t.astype(F32)
    return tuple(out)


def _mm(a, b, kind="nn"):
    order = max(len(a), len(b))
    acc = None
    for i, ai in enumerate(a):
        for j, bj in enumerate(b):
            if i + j < order:
                d = lax.dot_general(ai, bj, _CONTRACT[kind], preferred_element_type=F32)
                acc = d if acc is None else acc + d
    return acc


def _rms(x, gain):
    return x * lax.rsqrt(jnp.mean(x * x, axis=-1, keepdims=True) + RMS_EPS) * gain


def _params(*semantics):
    return pltpu.CompilerParams(dimension_semantics=semantics, vmem_limit_bytes=VMEM_LIMIT)


def _const_spec(shape):
    zeros = (0,) * len(shape)
    return pl.BlockSpec(shape, lambda *_: zeros)


def _inproj_kernel(x_ref, gain_ref, w_ref, p_ref, u_ref, qkv_ref):
    h = _rms(x_ref[...], gain_ref[...]).astype(BF16)
    s5_at, qkv_at = RWKV_IN, RWKV_IN + S5_WIDTH
    p_ref[...] = jnp.dot(h, w_ref[:, :s5_at], preferred_element_type=F32)
    u_ref[...] = jnp.dot(h, w_ref[:, s5_at:qkv_at], preferred_element_type=F32)
    qkv_ref[...] = jnp.dot(h, w_ref[:, qkv_at:], preferred_element_type=F32).astype(BF16)


def _in_projection(x, gain, w_in):
    t, d = x.shape
    n_in = w_in.shape[1]
    n_qkv = n_in - RWKV_IN - S5_WIDTH
    row = lambda n: pl.BlockSpec((ROW_TILE, n), lambda i: (i, 0))
    return pl.pallas_call(
        _inproj_kernel,
        name="in_projection",
        grid=(t // ROW_TILE,),
        in_specs=[row(d), _const_spec((1, d)), _const_spec((d, n_in))],
        out_specs=[row(RWKV_IN), row(S5_WIDTH), row(n_qkv)],
        out_shape=[jax.ShapeDtypeStruct((t, RWKV_IN), F32), jax.ShapeDtypeStruct((t, S5_WIDTH), F32),
                   jax.ShapeDtypeStruct((t, n_qkv), BF16)],
        compiler_params=_params("parallel"),
    )(x, gain.reshape(1, d), w_in.astype(BF16))


RWKV_TERMS = 1
HEADS_PER_VREG = LANES // HEAD_DIM


def _t(x):
    return _terms(x, RWKV_TERMS)


def _head_sums(x):
    first = lax.broadcasted_iota(jnp.int32, (1, LANES), 1) < HEAD_DIM
    outs = []
    for j in range(x.shape[1] // LANES):
        xp = x[:, j * LANES:(j + 1) * LANES]
        s0 = jnp.sum(jnp.where(first, xp, 0.0), axis=-1, keepdims=True)
        s1 = jnp.sum(jnp.where(first, 0.0, xp), axis=-1, keepdims=True)
        outs.append(jnp.where(first, s0, s1))
    return jnp.concatenate(outs, axis=1)


def _rwkv_kernel(p_ref, mu_ref, w0_ref, wup_ref, a0_ref, aup_ref, gup_ref, kk_ref, ka_ref, rk_ref,
                 lnw_ref, lnb_ref, o_ref, prev_ref, state_ref):
    nb, per_seq = p_ref.shape[0], p_ref.shape[1]
    n = RWKV_CHUNK
    rows = nb * per_seq
    items = rows // n
    w_at = 3 * RWKV_WIDTH
    pairs = RWKV_WIDTH // LANES

    @pl.when(pl.program_id(0) == 0)
    def _():
        prev_ref[...] = jnp.zeros_like(prev_ref)
        state_ref[...] = jnp.zeros_like(state_ref)

    p = p_ref[...].reshape(rows, RWKV_IN)
    row = lax.broadcasted_iota(jnp.int32, (rows, 1), 0)
    shifted = pltpu.roll(p, 1, axis=0)
    for b in range(nb):
        shifted = jnp.where(row == b * per_seq, prev_ref[b:b + 1, :], shifted)
    for b in range(nb):
        prev_ref[b:b + 1, :] = p[(b + 1) * per_seq - 1:(b + 1) * per_seq, :]
    p = p + (shifted - p) * mu_ref[...]

    r = p[:, :RWKV_WIDTH]
    k = p[:, RWKV_WIDTH:2 * RWKV_WIDTH]
    v = p[:, 2 * RWKV_WIDTH:w_at]
    w_lo = p[:, w_at:w_at + 64]
    a_lo = p[:, w_at + 64:w_at + 128]
    g_lo = p[:, w_at + 128:]

    lora = lambda x, w_ref: _mm(_t(x), tuple(w_ref[i] for i in range(RWKV_TERMS)))
    w = -jax.nn.softplus(-(w0_ref[...] + lora(jnp.tanh(w_lo), wup_ref))) - 0.5
    log_decay = -jnp.exp(w)
    a = jax.nn.sigmoid(a0_ref[...] + lora(a_lo, aup_ref))
    gate = lora(jax.nn.sigmoid(g_lo), gup_ref)

    kk = k * kk_ref[...]
    kk = kk / jnp.maximum(jnp.sqrt(_head_sums(kk * kk)), 1e-12)
    k = k * (1.0 + (a - 1.0) * ka_ref[...])

    s_i = lax.broadcasted_iota(jnp.int32, (rows, rows), 0)
    s_j = lax.broadcasted_iota(jnp.int32, (rows, rows), 1)
    causal = ((s_i // n == s_j // n) & (s_i >= s_j)).astype(BF16)
    cum = _mm((causal,), _terms(log_decay, 3))
    cum_end = cum[n - 1:n, :]
    for it in range(1, items):
        cum_end = jnp.where(row >= it * n, cum[(it + 1) * n - 1:(it + 1) * n, :], cum_end)
    inv_g = jnp.exp(-cum)
    to_end = jnp.exp(cum_end - cum)
    beta = kk * a
    r_bar = r * jnp.exp(cum)
    a_bar = -kk * jnp.exp(cum - log_decay)
    b_til = beta * inv_g
    k_til = k * inv_g
    b_end = beta * to_end
    k_end = k * to_end
    g_end = jnp.exp(cum_end)

    lane = lax.broadcasted_iota(jnp.int32, (1, LANES), 1)
    q_i = lax.broadcasted_iota(jnp.int32, (n, LANES), 0)
    q_j = lax.broadcasted_iota(jnp.int32, (n, LANES), 1) % n
    lower = q_i >= q_j
    strictly_lower = q_i > q_j
    eye = (lax.broadcasted_iota(jnp.int32, (n, n), 0) == lax.broadcasted_iota(jnp.int32, (n, n), 1)).astype(F32)
    c_i = lax.broadcasted_iota(jnp.int32, (LANES, LANES), 0)
    c_j = lax.broadcasted_iota(jnp.int32, (LANES, LANES), 1)
    same_head = c_i // HEAD_DIM == c_j // HEAD_DIM
    diagonal = c_i == c_j
    zeros = jnp.zeros((n, LANES), F32)

    pair_ids = [(it, j) for it in range(items) for j in range(pairs)]
    head_ids = [(pi, h) for pi in range(len(pair_ids)) for h in range(HEADS_PER_VREG)]
    heads_of = lambda pi: [hi for hi, (p_i, _) in enumerate(head_ids) if p_i == pi]
    cut = lambda x, pi: x[pair_ids[pi][0] * n:(pair_ids[pi][0] + 1) * n,
                          pair_ids[pi][1] * LANES:(pair_ids[pi][1] + 1) * LANES]
    rb = [cut(r_bar, pi) for pi in range(len(pair_ids))]
    be = [cut(b_end, pi) for pi in range(len(pair_ids))]
    vv = [cut(v, pi) for pi in range(len(pair_ids))]
    bk = [_t(jnp.concatenate([cut(b_til, pi), cut(k_til, pi)], axis=0)) for pi in range(len(pair_ids))]
    mine = [lane // HEAD_DIM == h for _, h in head_ids]
    abm = [jnp.where(mine[hi], cut(a_bar, pi), 0.0) for hi, (pi, _) in enumerate(head_ids)]
    v0 = [_t(jnp.concatenate([zeros, jnp.where(mine[hi], vv[pi], 0.0)], axis=0)) for hi, (pi, _) in enumerate(head_ids)]
    m = [_mm(_t(jnp.concatenate([abm[hi], jnp.where(mine[hi], rb[pi], 0.0)], axis=0)), bk[pi], "nt")
         for hi, (pi, _) in enumerate(head_ids)]
    top = [jnp.where(strictly_lower, x[:n], 0.0) for x in m]
    bot = [jnp.where(lower, x[n:], 0.0) for x in m]
    y = [_mm(_t(top[hi]), v0[hi]) for hi in range(len(head_ids))]
    power = [x[:, :n] for x in top]
    inv = [eye + x for x in power]
    for _ in range(int(math.log2(n)) - 1):
        pt = [_t(x) for x in power]
        power = [_mm(x, x) for x in pt]
        inv = [x + _mm(_t(x), _t(pw)) for x, pw in zip(inv, power)]
    wu = [_mm(_t(inv[hi]), _t(jnp.concatenate([abm[hi], y[hi]], axis=1))) for hi in range(len(head_ids))]
    v_low = [tuple(jnp.concatenate([jnp.zeros((n, LANES), BF16), t[n:]], axis=1) for t in v0[hi])
             for hi in range(len(head_ids))]
    ro = [_mm(_t(bot[hi]), tuple(jnp.concatenate([a, b2], axis=0) for a, b2 in zip(_t(wu[hi]), v_low[hi])))
          for hi in range(len(head_ids))]
    both = lambda xs, pi, sl: sum(xs[hi][:, sl] for hi in heads_of(pi))
    first, second = slice(0, LANES), slice(LANES, 2 * LANES)
    g_t = [jnp.where(same_head, _mm(_t(be[pi]), _t(both(wu, pi, first)), "tn"), 0.0)
           + jnp.where(diagonal, cut(g_end, pi)[:1], 0.0) for pi in range(len(pair_ids))]
    s2_t = [_mm(_t(jnp.concatenate([be[pi], cut(k_end, pi)], axis=0)),
                _t(jnp.concatenate([both(wu, pi, second), vv[pi]], axis=0)), "tn") for pi in range(len(pair_ids))]
    r2 = [_t(rb[pi] + both(ro, pi, first)) for pi in range(len(pair_ids))]
    out = [None] * len(pair_ids)
    per_step = items // nb
    for b in range(nb):
        for j in range(pairs):
            state = state_ref[b * pairs + j]
            for c in range(per_step):
                pi = (b * per_step + c) * pairs + j
                out[pi] = _mm(r2[pi], _t(state)) + both(ro, pi, second)
                state = _mm(_t(g_t[pi]), _t(state)) + jnp.where(same_head, s2_t[pi], 0.0)
            state_ref[b * pairs + j] = state
    o = jnp.concatenate([jnp.concatenate(out[it * pairs:(it + 1) * pairs], axis=1) for it in range(items)], axis=0)

    inv_n = 1.0 / HEAD_DIM
    cen = o - _head_sums(o) * inv_n
    var = _head_sums(cen * cen) * inv_n
    o = cen * lax.rsqrt(var + RWKV_GN_EPS) * lnw_ref[...] + lnb_ref[...]
    o = o + _head_sums(r * k * rk_ref[...]) * v
    o_ref[...] = (o * gate).reshape(nb, per_seq, RWKV_WIDTH)


def _rwkv_mix(p, mu, w0, w_up, a0, a_up, g_up, k_k, k_a, r_k, ln_w, ln_b):
    b, s, _ = p.shape
    per_step = RWKV_CHUNK * RWKV_STEP_CHUNKS
    vec = lambda t: t.astype(F32).reshape(1, -1)
    split = lambda t: jnp.stack(_terms(t.astype(F32), RWKV_TERMS))
    consts = [vec(mu), vec(w0), split(w_up), vec(a0), split(a_up), split(g_up), vec(k_k), vec(k_a), vec(r_k),
              vec(ln_w), vec(ln_b)]
    return pl.pallas_call(
        _rwkv_kernel,
        name="rwkv7",
        grid=(s // per_step,),
        in_specs=[pl.BlockSpec((b, per_step, RWKV_IN), lambda c: (0, c, 0))]
        + [_const_spec(t.shape) for t in consts],
        out_specs=pl.BlockSpec((b, per_step, RWKV_WIDTH), lambda c: (0, c, 0)),
        out_shape=jax.ShapeDtypeStruct((b, s, RWKV_WIDTH), F32),
        scratch_shapes=[pltpu.VMEM((b, RWKV_IN), F32),
                        pltpu.VMEM((b * RWKV_WIDTH // LANES, LANES, LANES), F32)],
        compiler_params=_params("arbitrary"),
    )(p, *consts)


def _s5_prep_kernel(are_ref, aim_ref, lstep_ref, bre_ref, bim_ref, bmat_ref, pw_ref, step_ref):
    lam_re = jnp.minimum(are_ref[...], -1e-4)
    lam_im = aim_ref[...]
    step = jnp.exp(lstep_ref[...])
    z_re, z_im = lam_re * step, lam_im * step
    mag = jnp.exp(z_re)
    num_re, num_im = mag * jnp.cos(z_im) - 1.0, mag * jnp.sin(z_im)
    den = lam_re * lam_re + lam_im * lam_im
    coef_re = (num_re * lam_re + num_im * lam_im) / den
    coef_im = (num_im * lam_re - num_re * lam_im) / den
    bre, bim = bre_ref[...], bim_ref[...]
    bmat_ref[:, :S5_LANES] = (coef_re * bre - coef_im * bim).astype(bmat_ref.dtype)
    bmat_ref[:, S5_LANES:] = (coef_re * bim + coef_im * bre).astype(bmat_ref.dtype)
    steps = (lax.broadcasted_iota(jnp.int32, (SUBLANES, 1), 0) + 1).astype(F32)
    pmag = jnp.exp(steps * z_re)
    pw_ref[:, :S5_LANES] = pmag * jnp.cos(steps * z_im)
    pw_ref[:, S5_LANES:] = pmag * jnp.sin(steps * z_im)
    sub = lax.broadcasted_iota(jnp.int32, (SUBLANES, 1), 0)
    for j in range(step_ref.shape[0]):
        dist = 2 ** j
        mag_j = jnp.where(sub >= dist, jnp.exp(dist * z_re), 0.0)
        step_ref[j, :, :S5_LANES] = mag_j * jnp.cos(dist * z_im)
        step_ref[j, :, S5_LANES:] = mag_j * jnp.sin(dist * z_im)


def _s5_kernel(u_ref, bmat_ref, pw_ref, step_ref, cre_ref, cim_ref, d_ref, wglu_ref, bglu_ref, gain_ref, o_ref, carry_ref):
    n = S5_CHUNK

    @pl.when(pl.program_id(1) == 0)
    def _():
        carry_ref[...] = jnp.zeros_like(carry_ref)

    u = u_ref[0]
    bu = jnp.dot(u.astype(BF16), bmat_ref[...], preferred_element_type=F32)
    x_re, x_im = bu[:, :S5_LANES], bu[:, S5_LANES:]
    groups = n // SUBLANES
    for j in range(step_ref.shape[0]):
        l_re = step_ref[j, :, :S5_LANES][None]
        l_im = step_ref[j, :, S5_LANES:][None]
        s_re = pltpu.roll(x_re, 2 ** j, axis=0).reshape(groups, SUBLANES, S5_LANES)
        s_im = pltpu.roll(x_im, 2 ** j, axis=0).reshape(groups, SUBLANES, S5_LANES)
        x_re = x_re + (l_re * s_re - l_im * s_im).reshape(n, S5_LANES)
        x_im = x_im + (l_re * s_im + l_im * s_re).reshape(n, S5_LANES)
    c_re, c_im = carry_ref[:, :S5_LANES], carry_ref[:, S5_LANES:]
    p_re, p_im = pw_ref[:, :S5_LANES], pw_ref[:, S5_LANES:]
    groups_re, groups_im = [], []
    for g in range(n // SUBLANES):
        g_re, g_im = x_re[g * SUBLANES:(g + 1) * SUBLANES], x_im[g * SUBLANES:(g + 1) * SUBLANES]
        g_re, g_im = g_re + p_re * c_re - p_im * c_im, g_im + p_re * c_im + p_im * c_re
        c_re, c_im = g_re[SUBLANES - 1:], g_im[SUBLANES - 1:]
        groups_re.append(g_re)
        groups_im.append(g_im)
    carry_ref[:, :S5_LANES] = c_re
    carry_ref[:, S5_LANES:] = c_im
    x_re, x_im = jnp.concatenate(groups_re, axis=0), jnp.concatenate(groups_im, axis=0)

    y = (jnp.dot(x_re.astype(BF16), cre_ref[...], preferred_element_type=F32)
         - jnp.dot(x_im.astype(BF16), cim_ref[...], preferred_element_type=F32)) + d_ref[...] * u
    y = jax.nn.gelu(y)
    y = y * jax.nn.sigmoid(jnp.dot(y.astype(BF16), wglu_ref[...], preferred_element_type=F32) + bglu_ref[...])
    o_ref[0] = _rms(y, gain_ref[...])


def _block_diag(t):
    g, rows, cols = t.shape
    return jnp.einsum('grc,gh->grhc', t, jnp.eye(g, dtype=t.dtype)).reshape(g * rows, g * cols)


def _s5_mix(u, a_re, a_im, log_step, b_re, b_im, c_re, c_im, d, w_glu, b_glu, gain):
    b, s, width = u.shape
    lane = lambda t: t.astype(F32).reshape(1, S5_LANES)
    to_cp = lambda t: _block_diag(jnp.swapaxes(t.astype(F32), 1, 2))
    bmat, pw, steps = pl.pallas_call(
        _s5_prep_kernel,
        name="s5_discretise",
        out_shape=[jax.ShapeDtypeStruct((width, 2 * S5_LANES), BF16),
                   jax.ShapeDtypeStruct((SUBLANES, 2 * S5_LANES), F32),
                   jax.ShapeDtypeStruct((int(math.log2(SUBLANES)), SUBLANES, 2 * S5_LANES), F32)],
        compiler_params=pltpu.CompilerParams(vmem_limit_bytes=VMEM_LIMIT),
    )(lane(a_re), lane(a_im), lane(jnp.repeat(log_step[:, None], S5_STATE, axis=1)), to_cp(b_re), to_cp(b_im))
    cre = _block_diag(jnp.swapaxes(c_re, 1, 2)).astype(BF16)
    cim = _block_diag(jnp.swapaxes(c_im, 1, 2)).astype(BF16)
    vec = lambda t: t.astype(F32).reshape(1, width)
    consts = [bmat, pw, steps, cre, cim, vec(d), w_glu.astype(BF16), vec(b_glu), vec(gain)]
    return pl.pallas_call(
        _s5_kernel,
        name="s5_scan",
        grid=(b, s // S5_CHUNK),
        in_specs=[pl.BlockSpec((1, S5_CHUNK, width), lambda i, c: (i, c, 0))]
        + [_const_spec(t.shape) for t in consts],
        out_specs=pl.BlockSpec((1, S5_CHUNK, width), lambda i, c: (i, c, 0)),
        out_shape=jax.ShapeDtypeStruct((b, s, width), F32),
        scratch_shapes=[pltpu.VMEM((1, 2 * S5_LANES), F32)],
        compiler_params=_params("parallel", "arbitrary"),
    )(u, *consts)


def _attn_kernel(q_ref, *refs):
    blocks = ATTN_KEY_BLOCKS + ATTN_STEP_TILES - 1
    k_refs = refs[:blocks]
    v_refs = refs[blocks:2 * blocks]
    bias_ref, gain_ref, o_ref = refs[2 * blocks:]
    n_keys = ATTN_KEY_BLOCKS * ATTN_TILE
    q_all = q_ref[0] * (HEAD_DIM ** -0.5)
    k_all = jnp.concatenate([kr[0] for kr in k_refs], axis=0)
    v_all = jnp.concatenate([vr[0] for vr in v_refs], axis=0)
    neg = jnp.finfo(F32).min
    in_band = bias_ref[0] != neg
    lane = lax.broadcasted_iota(jnp.int32, (1, n_keys), 1)
    heads = [slice(h * HEAD_DIM, (h + 1) * HEAD_DIM) for h in range(ATTN_HEADS)]
    work = []
    for i in range(ATTN_STEP_TILES):
        rows = slice(i * ATTN_TILE, (i + 1) * ATTN_TILE)
        keys = slice(i * ATTN_TILE, i * ATTN_TILE + n_keys)
        first_key = (pl.program_id(1) * ATTN_STEP_TILES + i - (ATTN_KEY_BLOCKS - 1)) * ATTN_TILE
        allowed = in_band & (first_key + lane >= 0)
        work += [(q_all[rows, sl], k_all[keys, sl], v_all[keys, sl], allowed, h) for h, sl in enumerate(heads)]
    s = [lax.dot_general(q, k, _CONTRACT["nt"], preferred_element_type=F32) for q, k, _, _, _ in work]
    s = [jnp.where(allowed, x + bias_ref[h], neg) for x, (_, _, _, allowed, h) in zip(s, work)]
    e = [jnp.exp(x - jnp.max(x, axis=-1, keepdims=True)) for x in s]
    pv = [jnp.dot(x.astype(BF16), v, preferred_element_type=F32) for x, (_, _, v, _, _) in zip(e, work)]
    outs = [x / jnp.sum(y, axis=-1, keepdims=True) for x, y in zip(pv, e)]
    tiles = [jnp.concatenate(outs[i * ATTN_HEADS:(i + 1) * ATTN_HEADS], axis=1) for i in range(ATTN_STEP_TILES)]
    o_ref[0] = _rms(jnp.concatenate(tiles, axis=0), gain_ref[...])


def _attn_bias_table(rel_bias):
    n_keys = ATTN_KEY_BLOCKS * ATTN_TILE
    left = ATTN_LEFT * ATTN_CHUNK
    period = n_keys + ATTN_TILE
    rb = rel_bias.astype(F32)
    heads = rb.shape[0]
    far = lambda width: jnp.broadcast_to(rb[:, 2 * MAX_REL:], (heads, width))
    g = jnp.concatenate([far(left - MAX_REL), rb[:, 1:][:, ::-1], far(period - left - MAX_REL)], axis=1)
    toeplitz = jnp.tile(g, (1, ATTN_TILE))[:, :ATTN_TILE * (period - 1)].reshape(heads, ATTN_TILE, period - 1)
    q = np.arange(ATTN_TILE)[:, None]
    j = np.arange(n_keys)[None, :]
    in_band = (j // ATTN_CHUNK >= q // ATTN_CHUNK) & (j // ATTN_CHUNK <= q // ATTN_CHUNK + ATTN_LEFT)
    return jnp.where(in_band[None], toeplitz[:, :, :n_keys], jnp.finfo(F32).min)


def _chunk_attention(qkv, rel_bias, gain):
    b, s, _ = qkv.shape
    bias = _attn_bias_table(rel_bias)
    back = ATTN_KEY_BLOCKS - 1
    blocks = ATTN_KEY_BLOCKS + ATTN_STEP_TILES - 1
    step_rows = ATTN_STEP_TILES * ATTN_TILE
    key_specs = lambda col: [pl.BlockSpec((1, ATTN_TILE, ATTN_WIDTH),
                                          lambda i, c, j=j: (i, jnp.maximum(c * ATTN_STEP_TILES + j - back, 0), col))
                             for j in range(blocks)]
    return pl.pallas_call(
        _attn_kernel,
        name="chunk_attention",
        grid=(b, s // step_rows),
        in_specs=[pl.BlockSpec((1, step_rows, ATTN_WIDTH), lambda i, c: (i, c, 0))] + key_specs(1) + key_specs(2)
        + [_const_spec(bias.shape), _const_spec((1, ATTN_WIDTH))],
        out_specs=pl.BlockSpec((1, step_rows, ATTN_WIDTH), lambda i, c: (i, c, 0)),
        out_shape=jax.ShapeDtypeStruct((b, s, ATTN_WIDTH), F32),
        compiler_params=_params("parallel", "parallel"),
    )(*([qkv] * (1 + 2 * blocks)), bias, gain.astype(F32).reshape(1, ATTN_WIDTH))


def _outproj_kernel(route, x_ref, orw_ref, os5_ref, oat_ref, w_ref, gpost_ref, gpre_ref, *refs):
    mixed = jnp.concatenate([orw_ref[...], os5_ref[...], oat_ref[...]], axis=1).astype(BF16)
    y = jnp.dot(mixed, w_ref[...], preferred_element_type=F32)
    x = x_ref[...] + _rms(y, gpost_ref[...])
    h = _rms(x, gpre_ref[...])
    if not route:
        x_out, h_out = refs
    else:
        router_ref, x_out, h_out, route_out = refs
        logits = _mm(_terms(h, 2), _terms(router_ref[...], 2))
        lane = lax.broadcasted_iota(jnp.int32, logits.shape, 1)
        neg = jnp.finfo(F32).min
        logits = jnp.where(lane < N_EXPERTS, logits, neg)
        top1 = jnp.max(logits, axis=-1, keepdims=True)
        idx1 = jnp.min(jnp.where(logits == top1, lane, LANES), axis=-1, keepdims=True)
        rest = jnp.where(lane == idx1, neg, logits)
        top2 = jnp.max(rest, axis=-1, keepdims=True)
        idx2 = jnp.min(jnp.where(rest == top2, lane, LANES), axis=-1, keepdims=True)
        e2 = jnp.exp(top2 - top1)
        route_out[...] = (jnp.where(lane == 0, idx1.astype(F32), 0.0) + jnp.where(lane == 1, idx2.astype(F32), 0.0)
                          + jnp.where(lane == 2, 1.0 / (1.0 + e2), 0.0) + jnp.where(lane == 3, e2 / (1.0 + e2), 0.0))
    x_out[...] = x
    h_out[...] = h.astype(h_out.dtype)


def _out_projection(x, o_rwkv, o_s5, o_attn, w_out, gain_post, gain_pre, router):
    t, d = x.shape
    row = lambda n: pl.BlockSpec((ROW_TILE, n), lambda i: (i, 0))
    vec = lambda g: g.astype(F32).reshape(1, d)
    ins = [x, o_rwkv, o_s5, o_attn, w_out.astype(BF16), vec(gain_post), vec(gain_pre)]
    in_specs = [row(d), row(o_rwkv.shape[1]), row(o_s5.shape[1]), row(o_attn.shape[1]),
                _const_spec(w_out.shape), _const_spec((1, d)), _const_spec((1, d))]
    out_specs = [row(d), row(d)]
    out_shape = [jax.ShapeDtypeStruct((t, d), F32), jax.ShapeDtypeStruct((t, d), BF16 if router is None else F32)]
    if router is not None:
        ins.append(jnp.pad(router.astype(F32), ((0, 0), (0, LANES - N_EXPERTS))))
        in_specs.append(_const_spec((d, LANES)))
        out_specs.append(row(LANES))
        out_shape.append(jax.ShapeDtypeStruct((t, LANES), F32))
    return pl.pallas_call(
        functools.partial(_outproj_kernel, router is not None),
        name="out_projection",
        grid=(t // ROW_TILE,),
        in_specs=in_specs, out_specs=out_specs, out_shape=out_shape,
        compiler_params=_params("parallel"),
    )(*ins)


def _ffn_kernel(x_ref, h_ref, wg_ref, wu_ref, wd_ref, gain_ref, o_ref):
    h = h_ref[...]
    y = None
    for f in range(wg_ref.shape[1] // FFN_TILE):
        cols = slice(f * FFN_TILE, (f + 1) * FFN_TILE)
        act = (jax.nn.silu(jnp.dot(h, wg_ref[:, cols], preferred_element_type=F32))
               * jnp.dot(h, wu_ref[:, cols], preferred_element_type=F32))
        part = jnp.dot(act.astype(BF16), wd_ref[cols, :], preferred_element_type=F32)
        y = part if y is None else y + part
    o_ref[...] = x_ref[...] + _rms(y, gain_ref[...])


def _ffn(x, h, w_gate, w_up, w_down, gain):
    t, d = x.shape
    d_ff = w_gate.shape[1]
    row = lambda n: pl.BlockSpec((ROW_TILE, n), lambda i: (i, 0))
    resident = lambda shape: pl.BlockSpec(shape, lambda i: (0, 0), pipeline_mode=pl.Buffered(1))
    return pl.pallas_call(
        _ffn_kernel,
        name="swiglu_ffn",
        grid=(t // ROW_TILE,),
        in_specs=[row(d), row(d), resident((d, d_ff)), resident((d, d_ff)), resident((d_ff, d)), _const_spec((1, d))],
        out_specs=row(d),
        out_shape=jax.ShapeDtypeStruct((t, d), F32),
        compiler_params=_params("parallel"),
    )(x, h, w_gate.astype(BF16), w_up.astype(BF16), w_down.astype(BF16), gain.astype(F32).reshape(1, d))


MOE_TILE = 512


def _moe_plan(route, n_experts):
    t = route.shape[0]
    n = 2 * t
    tm = MOE_TILE
    expert = route[:, :2].astype(jnp.int32).reshape(n)
    onehot = (expert[:, None] == jnp.arange(n_experts, dtype=jnp.int32)[None, :]).astype(jnp.int32)
    rank = jnp.sum(jnp.cumsum(onehot, axis=0) * onehot, axis=1) - 1
    counts = jnp.sum(onehot, axis=0)
    padded = (counts + tm - 1) // tm * tm
    ends = jnp.cumsum(padded)
    slot = jnp.sum((ends - padded)[None, :] * onehot, axis=1) + rank
    n_tiles = n // tm + n_experts
    slots = n_tiles * tm
    ids = jnp.arange(n, dtype=jnp.int32)
    every = jnp.arange(slots, dtype=jnp.int32)
    dump = n + every % tm + tm * ((every // tm) % 2)
    dst = dump.at[slot].set((ids % 2) * t + ids // 2, unique_indices=True)
    tok = jnp.where(dst < n, dst % t, 0)
    tile_start = jnp.arange(n_tiles, dtype=jnp.int32) * tm
    tile_expert = jnp.minimum(jnp.sum((tile_start[:, None] >= ends[None, :]).astype(jnp.int32), axis=1), n_experts - 1)
    tile_valid = (tile_start < ends[-1]).astype(jnp.int32)
    dst = dst.reshape(n_tiles, 1, tm)
    before_first = (n + tm + jnp.arange(tm, dtype=jnp.int32)).reshape(1, 1, tm)
    return tok.reshape(n_tiles, 1, tm), dst, jnp.concatenate([before_first, dst[:-1]], axis=0), tile_expert, tile_valid


def _moe_ffn_kernel(te_ref, tv_ref, tok0_ref, tokn_ref, dstp_ref, dst_ref, h_hbm, wg_ref, wu_ref, wd_ref,
                    out_hbm, xbuf, xb, ybuf, acc_ref, gsem, ssem):
    t, f = pl.program_id(0), pl.program_id(1)
    n_t, n_f = pl.num_programs(0), pl.num_programs(1)
    tm = MOE_TILE
    slot = t % 2
    valid = tv_ref[t] == 1
    prev_valid = tv_ref[jnp.maximum(t - 1, 0)] == 1
    last = t == n_t - 1

    gather_row = lambda tok_ref, s, i: pltpu.make_async_copy(
        h_hbm.at[pl.ds(tok_ref[0, 0, i], 1), :], xbuf.at[s, pl.ds(i, 1), :], gsem.at[s])
    scatter_row = lambda d_ref, s, i: pltpu.make_async_copy(
        ybuf.at[s, pl.ds(i, 1), :], out_hbm.at[pl.ds(d_ref[0, 0, i], 1), :], ssem.at[s])
    wait_gather = lambda s: pltpu.make_async_copy(h_hbm.at[pl.ds(0, tm), :], xbuf.at[s], gsem.at[s]).wait()
    wait_scatter = lambda s: pltpu.make_async_copy(ybuf.at[s], out_hbm.at[pl.ds(0, tm), :], ssem.at[s]).wait()

    def looped(copy, idx_ref, s):
        def body(i, carry):
            copy(idx_ref, s, i).start()
            return carry
        lax.fori_loop(0, tm, body, 0, unroll=8)

    def compute():
        x = xb[...]
        act = jax.nn.silu(jnp.dot(x, wg_ref[0], preferred_element_type=F32)) * jnp.dot(x, wu_ref[0], preferred_element_type=F32)
        return jnp.dot(act.astype(BF16), wd_ref[0], preferred_element_type=F32)

    @pl.when((f == 0) & (t == 0))
    def _():
        looped(gather_row, tok0_ref, 0)
        ybuf[1] = jnp.zeros(ybuf.shape[1:], F32)
        for bank in range(2):
            fill = pltpu.make_async_copy(ybuf.at[1], out_hbm.at[pl.ds(out_hbm.shape[0] - (bank + 1) * tm, tm), :],
                                         ssem.at[1])
            fill.start()
            fill.wait()

    @pl.when((f == 0) & ((t == 0) | prev_valid))
    def _():
        wait_gather(slot)

    @pl.when((f == 0) & valid)
    def _():
        xb[...] = xbuf[slot].astype(BF16)
        for i in range(tm):
            gather_row(tokn_ref, 1 - slot, i).start()
        acc_ref[...] = compute()

    @pl.when((f == n_f - 1) & ((t == 1) | ((t >= 2) & (tv_ref[jnp.maximum(t - 2, 0)] == 1))))
    def _():
        wait_scatter(slot)

    @pl.when((f == n_f - 1) & valid)
    def _():
        for i in range(tm):
            scatter_row(dstp_ref, 1 - slot, i).start()
        ybuf[slot] = acc_ref[...] + compute()

    @pl.when((f == n_f - 1) & jnp.logical_not(valid) & (t >= 1) & prev_valid)
    def _():
        looped(scatter_row, dstp_ref, 1 - slot)

    @pl.when((f == n_f - 1) & last)
    def _():
        @pl.when(valid)
        def _():
            looped(scatter_row, dst_ref, slot)
            wait_scatter(slot)
            wait_gather(1 - slot)

        @pl.when(prev_valid)
        def _():
            wait_scatter(1 - slot)


def _moe_combine_kernel(x_ref, y1_ref, y2_ref, route_ref, gain_ref, o_ref):
    y = route_ref[:, 2:3] * y1_ref[...] + route_ref[:, 3:4] * y2_ref[...]
    o_ref[...] = x_ref[...] + _rms(y, gain_ref[...])


def _moe(x, h, route, w_gate, w_up, w_down, gain):
    t, d = x.shape
    n_e, _, d_ff = w_gate.shape
    tm = MOE_TILE
    assert d_ff == 2 * FFN_TILE
    tok, dst, dst_prev, tile_expert, tile_valid = _moe_plan(route, n_e)
    n_tiles = tok.shape[0]
    idx_spec = lambda shift: pl.BlockSpec(
        (1, 1, tm), lambda i, f, te, tv: (jnp.clip(i + shift, 0, n_tiles - 1), 0, 0), memory_space=pltpu.SMEM)
    grid_spec = pltpu.PrefetchScalarGridSpec(
        num_scalar_prefetch=2,
        grid=(n_tiles, d_ff // FFN_TILE),
        in_specs=[idx_spec(0), idx_spec(1), idx_spec(0), idx_spec(0),
                  pl.BlockSpec(memory_space=pl.ANY),
                  pl.BlockSpec((1, d, FFN_TILE), lambda i, f, te, tv: (te[i], 0, f)),
                  pl.BlockSpec((1, d, FFN_TILE), lambda i, f, te, tv: (te[i], 0, f)),
                  pl.BlockSpec((1, FFN_TILE, d), lambda i, f, te, tv: (te[i], f, 0))],
        out_specs=pl.BlockSpec(memory_space=pl.ANY),
        scratch_shapes=[pltpu.VMEM((2, tm, d), F32), pltpu.VMEM((tm, d), BF16), pltpu.VMEM((2, tm, d), F32),
                        pltpu.VMEM((tm, d), F32), pltpu.SemaphoreType.DMA((2,)), pltpu.SemaphoreType.DMA((2,))],
    )
    routed = pl.pallas_call(
        _moe_ffn_kernel,
        name="moe_experts",
        grid_spec=grid_spec,
        out_shape=jax.ShapeDtypeStruct((2 * t + 2 * tm, d), F32),
        compiler_params=_params("arbitrary", "arbitrary"),
    )(tile_expert, tile_valid, tok, tok, dst_prev, dst, h, w_gate.astype(BF16), w_up.astype(BF16), w_down.astype(BF16))
    row = lambda n: pl.BlockSpec((ROW_TILE, n), lambda i: (i, 0))
    second = pl.BlockSpec((ROW_TILE, d), lambda i: (i + t // ROW_TILE, 0))
    return pl.pallas_call(
        _moe_combine_kernel,
        name="moe_combine",
        grid=(t // ROW_TILE,),
        in_specs=[row(d), row(d), second, row(LANES), _const_spec((1, d))],
        out_specs=row(d),
        out_shape=jax.ShapeDtypeStruct((t, d), F32),
        compiler_params=_params("parallel"),
    )(x, routed, routed, route, gain.astype(F32).reshape(1, d))


def kernel(x, norm_mix_pre, norm_mix_post, norm_ffn_pre, norm_ffn_post, w_in, w_out, rwkv_mu, rwkv_w0, rwkv_w_up, rwkv_a0, rwkv_a_up, rwkv_g_up, rwkv_k_k, rwkv_k_a, rwkv_r_k, rwkv_ln_w, rwkv_ln_b, s5_a_re, s5_a_im, s5_log_step, s5_b_re, s5_b_im, s5_c_re, s5_c_im, s5_d, s5_w_glu, s5_b_glu, s5_norm, attn_rel_bias, attn_norm, ffn_w_gate, ffn_w_up, ffn_w_down, moe_router, moe_w_gate, moe_w_up, moe_w_down):
    b, s, d = x.shape
    depth = w_in.shape[0]
    xt = x.reshape(b * s, d)
    for layer in range(depth):
        p, u, qkv = _in_projection(xt, norm_mix_pre[layer], w_in[layer])
        o_rwkv = _rwkv_mix(p.reshape(b, s, -1), rwkv_mu[layer], rwkv_w0[layer], rwkv_w_up[layer], rwkv_a0[layer],
                           rwkv_a_up[layer], rwkv_g_up[layer], rwkv_k_k[layer], rwkv_k_a[layer], rwkv_r_k[layer],
                           rwkv_ln_w[layer], rwkv_ln_b[layer])
        o_s5 = _s5_mix(u.reshape(b, s, -1), s5_a_re[layer], s5_a_im[layer], s5_log_step[layer], s5_b_re[layer],
                       s5_b_im[layer], s5_c_re[layer], s5_c_im[layer], s5_d[layer], s5_w_glu[layer],
                       s5_b_glu[layer], s5_norm[layer])
        o_attn = _chunk_attention(qkv.reshape(b, s, -1), attn_rel_bias[layer], attn_norm[layer])
        i = layer // 2
        moe = layer % 2 == 1
        outs = _out_projection(xt, o_rwkv.reshape(b * s, -1), o_s5.reshape(b * s, -1), o_attn.reshape(b * s, -1),
                               w_out[layer], norm_mix_post[layer], norm_ffn_pre[layer],
                               moe_router[i] if moe else None)
        if moe:
            xt, h, route = outs
            xt = _moe(xt, h, route, moe_w_gate[i], moe_w_up[i], moe_w_down[i], norm_ffn_post[layer])
        else:
            xt, h = outs
            xt = _ffn(xt, h, ffn_w_gate[i], ffn_w_up[i], ffn_w_down[i], norm_ffn_post[layer])
    return xt.reshape(b, s, d)
```

```python
import functools
import math

import jax
import jax.numpy as jnp
import numpy as np
from jax import lax
from jax.experimental import pallas as pl
from jax.experimental.pallas import tpu as pltpu

F32 = jnp.float32
BF16 = jnp.bfloat16

RMS_EPS = 1e-6
RWKV_GN_EPS = 64e-5
HEAD_DIM = 64
RWKV_HEADS = 6
RWKV_WIDTH = 384
RWKV_IN = 1408
S5_WIDTH = 256
S5_GROUPS = 16
S5_STATE = 64
S5_LANES = S5_GROUPS * S5_STATE
ATTN_HEADS = 6
ATTN_WIDTH = 384
ATTN_CHUNK = 64
ATTN_LEFT = 8
MAX_REL = 128
N_EXPERTS = 8
LANES = 128

RWKV_CHUNK = 64
RWKV_STEP_CHUNKS = 2
S5_CHUNK = 256
SUBLANES = 8
ATTN_TILE = 2 * ATTN_CHUNK
ATTN_KEY_BLOCKS = (ATTN_LEFT * ATTN_CHUNK) // ATTN_TILE + 1
ATTN_STEP_TILES = 2
ROW_TILE = 512
FFN_TILE = 1408
VMEM_LIMIT = 48 * 1024 * 1024

_HI = lax.Precision.HIGHEST
_CONTRACT = {"nn": (((1,), (0,)), ((), ())), "nt": (((1,), (1,)), ((), ())), "tn": (((0,), (0,)), ((), ()))}


def _hdot(a, b):
    return jnp.dot(a, b, precision=_HI, preferred_element_type=F32)


def _terms(x, n):
    out = []
    for i in range(n):
        t = x.astype(BF16)
        out.append(t)
        if i + 1 < n:
            x = x - t.astype(F32)
    return tuple(out)


def _mm(a, b, kind="nn"):
    order = max(len(a), len(b))
    acc = None
    for i, ai in enumerate(a):
        for j, bj in enumerate(b):
            if i + j < order:
                d = lax.dot_general(ai, bj, _CONTRACT[kind], preferred_element_type=F32)
                acc = d if acc is None else acc + d
    return acc


def _rms(x, gain):
    return x * lax.rsqrt(jnp.mean(x * x, axis=-1, keepdims=True) + RMS_EPS) * gain


def _params(*semantics):
    return pltpu.CompilerParams(dimension_semantics=semantics, vmem_limit_bytes=VMEM_LIMIT)


def _const_spec(shape):
    zeros = (0,) * len(shape)
    return pl.BlockSpec(shape, lambda *_: zeros)


def _inproj_kernel(x_ref, gain_ref, w_ref, p_ref, u_ref, qkv_ref):
    h = _rms(x_ref[...], gain_ref[...]).astype(BF16)
    s5_at, qkv_at = RWKV_IN, RWKV_IN + S5_WIDTH
    p_ref[...] = jnp.dot(h, w_ref[:, :s5_at], preferred_element_type=F32)
    u_ref[...] = jnp.dot(h, w_ref[:, s5_at:qkv_at], preferred_element_type=F32)
    qkv_ref[...] = jnp.dot(h, w_ref[:, qkv_at:], preferred_element_type=F32).astype(BF16)


def _in_projection(x, gain, w_in):
    t, d = x.shape
    n_in = w_in.shape[1]
    n_qkv = n_in - RWKV_IN - S5_WIDTH
    row = lambda n: pl.BlockSpec((ROW_TILE, n), lambda i: (i, 0))
    return pl.pallas_call(
        _inproj_kernel,
        name="in_projection",
        grid=(t // ROW_TILE,),
        in_specs=[row(d), _const_spec((1, d)), _const_spec((d, n_in))],
        out_specs=[row(RWKV_IN), row(S5_WIDTH), row(n_qkv)],
        out_shape=[jax.ShapeDtypeStruct((t, RWKV_IN), F32), jax.ShapeDtypeStruct((t, S5_WIDTH), F32),
                   jax.ShapeDtypeStruct((t, n_qkv), BF16)],
        compiler_params=_params("parallel"),
    )(x, gain.reshape(1, d), w_in.astype(BF16))


RWKV_TERMS = 1
HEADS_PER_VREG = LANES // HEAD_DIM


def _t(x):
    return _terms(x, RWKV_TERMS)


def _head_sums(x):
    first = lax.broadcasted_iota(jnp.int32, (1, LANES), 1) < HEAD_DIM
    outs = []
    for j in range(x.shape[1] // LANES):
        xp = x[:, j * LANES:(j + 1) * LANES]
        s0 = jnp.sum(jnp.where(first, xp, 0.0), axis=-1, keepdims=True)
        s1 = jnp.sum(jnp.where(first, 0.0, xp), axis=-1, keepdims=True)
        outs.append(jnp.where(first, s0, s1))
    return jnp.concatenate(outs, axis=1)


def _rwkv_kernel(p_ref, mu_ref, w0_ref, wup_ref, a0_ref, aup_ref, gup_ref, kk_ref, ka_ref, rk_ref,
                 lnw_ref, lnb_ref, o_ref, prev_ref, state_ref):
    nb, per_seq = p_ref.shape[0], p_ref.shape[1]
    n = RWKV_CHUNK
    rows = nb * per_seq
    items = rows // n
    w_at = 3 * RWKV_WIDTH
    pairs = RWKV_WIDTH // LANES

    @pl.when(pl.program_id(0) == 0)
    def _():
        prev_ref[...] = jnp.zeros_like(prev_ref)
        state_ref[...] = jnp.zeros_like(state_ref)

    p = p_ref[...].reshape(rows, RWKV_IN)
    row = lax.broadcasted_iota(jnp.int32, (rows, 1), 0)
    shifted = pltpu.roll(p, 1, axis=0)
    for b in range(nb):
        shifted = jnp.where(row == b * per_seq, prev_ref[b:b + 1, :], shifted)
    for b in range(nb):
        prev_ref[b:b + 1, :] = p[(b + 1) * per_seq - 1:(b + 1) * per_seq, :]
    p = p + (shifted - p) * mu_ref[...]

    r = p[:, :RWKV_WIDTH]
    k = p[:, RWKV_WIDTH:2 * RWKV_WIDTH]
    v = p[:, 2 * RWKV_WIDTH:w_at]
    w_lo = p[:, w_at:w_at + 64]
    a_lo = p[:, w_at + 64:w_at + 128]
    g_lo = p[:, w_at + 128:]

    lora = lambda x, w_ref: _mm(_t(x), tuple(w_ref[i] for i in range(RWKV_TERMS)))
    w = -jax.nn.softplus(-(w0_ref[...] + lora(jnp.tanh(w_lo), wup_ref))) - 0.5
    log_decay = -jnp.exp(w)
    a = jax.nn.sigmoid(a0_ref[...] + lora(a_lo, aup_ref))
    gate = lora(jax.nn.sigmoid(g_lo), gup_ref)

    kk = k * kk_ref[...]
    kk = kk * lax.rsqrt(jnp.maximum(_head_sums(kk * kk), 1e-24))
    k = k * (1.0 + (a - 1.0) * ka_ref[...])

    s_i = lax.broadcasted_iota(jnp.int32, (rows, rows), 0)
    s_j = lax.broadcasted_iota(jnp.int32, (rows, rows), 1)
    causal = ((s_i // n == s_j // n) & (s_i >= s_j)).astype(BF16)
    cum = _mm((causal,), _terms(log_decay, 3))
    cum_end = cum[n - 1:n, :]
    for it in range(1, items):
        cum_end = jnp.where(row >= it * n, cum[(it + 1) * n - 1:(it + 1) * n, :], cum_end)
    inv_g = jnp.exp(-cum)
    to_end = jnp.exp(cum_end - cum)
    beta = kk * a
    r_bar = r * jnp.exp(cum)
    a_bar = -kk * jnp.exp(cum - log_decay)
    b_til = beta * inv_g
    k_til = k * inv_g
    b_end = beta * to_end
    k_end = k * to_end
    g_end = jnp.exp(cum_end)

    lane = lax.broadcasted_iota(jnp.int32, (1, LANES), 1)
    q_i = lax.broadcasted_iota(jnp.int32, (n, LANES), 0)
    q_j = lax.broadcasted_iota(jnp.int32, (n, LANES), 1) % n
    lower = q_i >= q_j
    strictly_lower = q_i > q_j
    eye = (lax.broadcasted_iota(jnp.int32, (n, n), 0) == lax.broadcasted_iota(jnp.int32, (n, n), 1)).astype(F32)
    c_i = lax.broadcasted_iota(jnp.int32, (LANES, LANES), 0)
    c_j = lax.broadcasted_iota(jnp.int32, (LANES, LANES), 1)
    same_head = c_i // HEAD_DIM == c_j // HEAD_DIM
    diagonal = c_i == c_j
    zeros = jnp.zeros((n, LANES), F32)

    pair_ids = [(it, j) for it in range(items) for j in range(pairs)]
    head_ids = [(pi, h) for pi in range(len(pair_ids)) for h in range(HEADS_PER_VREG)]
    heads_of = lambda pi: [hi for hi, (p_i, _) in enumerate(head_ids) if p_i == pi]
    cut = lambda x, pi: x[pair_ids[pi][0] * n:(pair_ids[pi][0] + 1) * n,
                          pair_ids[pi][1] * LANES:(pair_ids[pi][1] + 1) * LANES]
    rb = [cut(r_bar, pi) for pi in range(len(pair_ids))]
    be = [cut(b_end, pi) for pi in range(len(pair_ids))]
    vv = [cut(v, pi) for pi in range(len(pair_ids))]
    bk = [_t(jnp.concatenate([cut(b_til, pi), cut(k_til, pi)], axis=0)) for pi in range(len(pair_ids))]
    mine = [lane // HEAD_DIM == h for _, h in head_ids]
    abm = [jnp.where(mine[hi], cut(a_bar, pi), 0.0) for hi, (pi, _) in enumerate(head_ids)]
    v0 = [_t(jnp.concatenate([zeros, jnp.where(mine[hi], vv[pi], 0.0)], axis=0)) for hi, (pi, _) in enumerate(head_ids)]
    m = [_mm(_t(jnp.concatenate([abm[hi], jnp.where(mine[hi], rb[pi], 0.0)], axis=0)), bk[pi], "nt")
         for hi, (pi, _) in enumerate(head_ids)]
    top = [jnp.where(strictly_lower, x[:n], 0.0) for x in m]
    bot = [jnp.where(lower, x[n:], 0.0) for x in m]
    y = [_mm(_t(top[hi]), v0[hi]) for hi in range(len(head_ids))]
    power = [x[:, :n] for x in top]
    inv = [eye + x for x in power]
    for _ in range(int(math.log2(n)) - 1):
        pt = [_t(x) for x in power]
        power = [_mm(x, x) for x in pt]
        inv = [x + _mm(_t(x), _t(pw)) for x, pw in zip(inv, power)]
    wu = [_mm(_t(inv[hi]), _t(jnp.concatenate([abm[hi], y[hi]], axis=1))) for hi in range(len(head_ids))]
    v_low = [tuple(jnp.concatenate([jnp.zeros((n, LANES), BF16), t[n:]], axis=1) for t in v0[hi])
             for hi in range(len(head_ids))]
    ro = [_mm(_t(bot[hi]), tuple(jnp.concatenate([a, b2], axis=0) for a, b2 in zip(_t(wu[hi]), v_low[hi])))
          for hi in range(len(head_ids))]
    both = lambda xs, pi, sl: sum(xs[hi][:, sl] for hi in heads_of(pi))
    first, second = slice(0, LANES), slice(LANES, 2 * LANES)
    g_t = [jnp.where(same_head, _mm(_t(be[pi]), _t(both(wu, pi, first)), "tn"), 0.0)
           + jnp.where(diagonal, cut(g_end, pi)[:1], 0.0) for pi in range(len(pair_ids))]
    s2_t = [_mm(_t(jnp.concatenate([be[pi], cut(k_end, pi)], axis=0)),
                _t(jnp.concatenate([both(wu, pi, second), vv[pi]], axis=0)), "tn") for pi in range(len(pair_ids))]
    r2 = [_t(rb[pi] + both(ro, pi, first)) for pi in range(len(pair_ids))]
    out = [None] * len(pair_ids)
    per_step = items // nb
    for b in range(nb):
        for j in range(pairs):
            state = state_ref[b * pairs + j]
            for c in range(per_step):
                pi = (b * per_step + c) * pairs + j
                out[pi] = _mm(r2[pi], _t(state)) + both(ro, pi, second)
                state = _mm(_t(g_t[pi]), _t(state)) + jnp.where(same_head, s2_t[pi], 0.0)
            state_ref[b * pairs + j] = state
    o = jnp.concatenate([jnp.concatenate(out[it * pairs:(it + 1) * pairs], axis=1) for it in range(items)], axis=0)

    inv_n = 1.0 / HEAD_DIM
    cen = o - _head_sums(o) * inv_n
    var = _head_sums(cen * cen) * inv_n
    o = cen * lax.rsqrt(var + RWKV_GN_EPS) * lnw_ref[...] + lnb_ref[...]
    o = o + _head_sums(r * k * rk_ref[...]) * v
    o_ref[...] = (o * gate).astype(o_ref.dtype).reshape(nb, per_seq, RWKV_WIDTH)


def _rwkv_mix(p, mu, w0, w_up, a0, a_up, g_up, k_k, k_a, r_k, ln_w, ln_b):
    b, s, _ = p.shape
    per_step = RWKV_CHUNK * RWKV_STEP_CHUNKS
    vec = lambda t: t.astype(F32).reshape(1, -1)
    split = lambda t: jnp.stack(_terms(t.astype(F32), RWKV_TERMS))
    consts = [vec(mu), vec(w0), split(w_up), vec(a0), split(a_up), split(g_up), vec(k_k), vec(k_a), vec(r_k),
              vec(ln_w), vec(ln_b)]
    return pl.pallas_call(
        _rwkv_kernel,
        name="rwkv7",
        grid=(s // per_step,),
        in_specs=[pl.BlockSpec((b, per_step, RWKV_IN), lambda c: (0, c, 0))]
        + [_const_spec(t.shape) for t in consts],
        out_specs=pl.BlockSpec((b, per_step, RWKV_WIDTH), lambda c: (0, c, 0)),
        out_shape=jax.ShapeDtypeStruct((b, s, RWKV_WIDTH), BF16),
        scratch_shapes=[pltpu.VMEM((b, RWKV_IN), F32),
                        pltpu.VMEM((b * RWKV_WIDTH // LANES, LANES, LANES), F32)],
        compiler_params=_params("arbitrary"),
    )(p, *consts)


def _s5_prep_kernel(are_ref, aim_ref, lstep_ref, bre_ref, bim_ref, bmat_ref, pw_ref, step_ref):
    lam_re = jnp.minimum(are_ref[...], -1e-4)
    lam_im = aim_ref[...]
    step = jnp.exp(lstep_ref[...])
    z_re, z_im = lam_re * step, lam_im * step
    mag = jnp.exp(z_re)
    num_re, num_im = mag * jnp.cos(z_im) - 1.0, mag * jnp.sin(z_im)
    den = lam_re * lam_re + lam_im * lam_im
    coef_re = (num_re * lam_re + num_im * lam_im) / den
    coef_im = (num_im * lam_re - num_re * lam_im) / den
    bre, bim = bre_ref[...], bim_ref[...]
    bmat_ref[:, :S5_LANES] = (coef_re * bre - coef_im * bim).astype(bmat_ref.dtype)
    bmat_ref[:, S5_LANES:] = (coef_re * bim + coef_im * bre).astype(bmat_ref.dtype)
    steps = (lax.broadcasted_iota(jnp.int32, (SUBLANES, 1), 0) + 1).astype(F32)
    pmag = jnp.exp(steps * z_re)
    pw_ref[:, :S5_LANES] = pmag * jnp.cos(steps * z_im)
    pw_ref[:, S5_LANES:] = pmag * jnp.sin(steps * z_im)
    sub = lax.broadcasted_iota(jnp.int32, (SUBLANES, 1), 0)
    for j in range(step_ref.shape[0]):
        dist = 2 ** j
        mag_j = jnp.where(sub >= dist, jnp.exp(dist * z_re), 0.0)
        step_ref[j, :, :S5_LANES] = mag_j * jnp.cos(dist * z_im)
        step_ref[j, :, S5_LANES:] = mag_j * jnp.sin(dist * z_im)


def _s5_kernel(u_ref, bmat_ref, pw_ref, step_ref, cre_ref, cim_ref, d_ref, wglu_ref, bglu_ref, gain_ref, o_ref, carry_ref):
    n = S5_CHUNK

    @pl.when(pl.program_id(1) == 0)
    def _():
        carry_ref[...] = jnp.zeros_like(carry_ref)

    u = u_ref[0]
    bu = jnp.dot(u.astype(BF16), bmat_ref[...], preferred_element_type=F32)
    x_re, x_im = bu[:, :S5_LANES], bu[:, S5_LANES:]
    groups = n // SUBLANES
    for j in range(step_ref.shape[0]):
        l_re = step_ref[j, :, :S5_LANES][None]
        l_im = step_ref[j, :, S5_LANES:][None]
        s_re = pltpu.roll(x_re, 2 ** j, axis=0).reshape(groups, SUBLANES, S5_LANES)
        s_im = pltpu.roll(x_im, 2 ** j, axis=0).reshape(groups, SUBLANES, S5_LANES)
        x_re = x_re + (l_re * s_re - l_im * s_im).reshape(n, S5_LANES)
        x_im = x_im + (l_re * s_im + l_im * s_re).reshape(n, S5_LANES)
    c_re, c_im = carry_ref[:, :S5_LANES], carry_ref[:, S5_LANES:]
    p_re, p_im = pw_ref[:, :S5_LANES], pw_ref[:, S5_LANES:]
    groups_re, groups_im = [], []
    for g in range(n // SUBLANES):
        g_re, g_im = x_re[g * SUBLANES:(g + 1) * SUBLANES], x_im[g * SUBLANES:(g + 1) * SUBLANES]
        g_re, g_im = g_re + p_re * c_re - p_im * c_im, g_im + p_re * c_im + p_im * c_re
        c_re, c_im = g_re[SUBLANES - 1:], g_im[SUBLANES - 1:]
        groups_re.append(g_re)
        groups_im.append(g_im)
    carry_ref[:, :S5_LANES] = c_re
    carry_ref[:, S5_LANES:] = c_im
    x_re, x_im = jnp.concatenate(groups_re, axis=0), jnp.concatenate(groups_im, axis=0)

    y = (jnp.dot(x_re.astype(BF16), cre_ref[...], preferred_element_type=F32)
         - jnp.dot(x_im.astype(BF16), cim_ref[...], preferred_element_type=F32)) + d_ref[...] * u
    y = jax.nn.gelu(y)
    y = y * jax.nn.sigmoid(jnp.dot(y.astype(BF16), wglu_ref[...], preferred_element_type=F32) + bglu_ref[...])
    o_ref[0] = _rms(y, gain_ref[...]).astype(o_ref.dtype)


def _block_diag(t):
    g, rows, cols = t.shape
    return jnp.einsum('grc,gh->grhc', t, jnp.eye(g, dtype=t.dtype)).reshape(g * rows, g * cols)


def _s5_mix(u, a_re, a_im, log_step, b_re, b_im, c_re, c_im, d, w_glu, b_glu, gain):
    b, s, width = u.shape
    lane = lambda t: t.astype(F32).reshape(1, S5_LANES)
    to_cp = lambda t: _block_diag(jnp.swapaxes(t.astype(F32), 1, 2))
    bmat, pw, steps = pl.pallas_call(
        _s5_prep_kernel,
        name="s5_discretise",
        out_shape=[jax.ShapeDtypeStruct((width, 2 * S5_LANES), BF16),
                   jax.ShapeDtypeStruct((SUBLANES, 2 * S5_LANES), F32),
                   jax.ShapeDtypeStruct((int(math.log2(SUBLANES)), SUBLANES, 2 * S5_LANES), F32)],
        compiler_params=pltpu.CompilerParams(vmem_limit_bytes=VMEM_LIMIT),
    )(lane(a_re), lane(a_im), lane(jnp.repeat(log_step[:, None], S5_STATE, axis=1)), to_cp(b_re), to_cp(b_im))
    cre = _block_diag(jnp.swapaxes(c_re, 1, 2)).astype(BF16)
    cim = _block_diag(jnp.swapaxes(c_im, 1, 2)).astype(BF16)
    vec = lambda t: t.astype(F32).reshape(1, width)
    consts = [bmat, pw, steps, cre, cim, vec(d), w_glu.astype(BF16), vec(b_glu), vec(gain)]
    return pl.pallas_call(
        _s5_kernel,
        name="s5_scan",
        grid=(b, s // S5_CHUNK),
        in_specs=[pl.BlockSpec((1, S5_CHUNK, width), lambda i, c: (i, c, 0))]
        + [_const_spec(t.shape) for t in consts],
        out_specs=pl.BlockSpec((1, S5_CHUNK, width), lambda i, c: (i, c, 0)),
        out_shape=jax.ShapeDtypeStruct((b, s, width), BF16),
        scratch_shapes=[pltpu.VMEM((1, 2 * S5_LANES), F32)],
        compiler_params=_params("parallel", "arbitrary"),
    )(u, *consts)


def _attn_kernel(q_ref, *refs):
    blocks = ATTN_KEY_BLOCKS + ATTN_STEP_TILES - 1
    k_refs = refs[:blocks]
    v_refs = refs[blocks:2 * blocks]
    bias_ref, gain_ref, o_ref = refs[2 * blocks:]
    n_keys = ATTN_KEY_BLOCKS * ATTN_TILE
    q_all = q_ref[0] * (HEAD_DIM ** -0.5)
    k_all = jnp.concatenate([kr[0] for kr in k_refs], axis=0)
    v_all = jnp.concatenate([vr[0] for vr in v_refs], axis=0)
    neg = jnp.finfo(F32).min
    in_band = bias_ref[0] != neg
    lane = lax.broadcasted_iota(jnp.int32, (1, n_keys), 1)
    heads = [slice(h * HEAD_DIM, (h + 1) * HEAD_DIM) for h in range(ATTN_HEADS)]
    work = []
    for i in range(ATTN_STEP_TILES):
        rows = slice(i * ATTN_TILE, (i + 1) * ATTN_TILE)
        keys = slice(i * ATTN_TILE, i * ATTN_TILE + n_keys)
        first_key = (pl.program_id(1) * ATTN_STEP_TILES + i - (ATTN_KEY_BLOCKS - 1)) * ATTN_TILE
        allowed = in_band & (first_key + lane >= 0)
        work += [(q_all[rows, sl], k_all[keys, sl], v_all[keys, sl], allowed, h) for h, sl in enumerate(heads)]
    s = [lax.dot_general(q, k, _CONTRACT["nt"], preferred_element_type=F32) for q, k, _, _, _ in work]
    s = [jnp.where(allowed, x + bias_ref[h], neg) for x, (_, _, _, allowed, h) in zip(s, work)]
    e = [jnp.exp(x - jnp.max(x, axis=-1, keepdims=True)) for x in s]
    pv = [jnp.dot(x.astype(BF16), v, preferred_element_type=F32) for x, (_, _, v, _, _) in zip(e, work)]
    outs = [x / jnp.sum(y, axis=-1, keepdims=True) for x, y in zip(pv, e)]
    tiles = [jnp.concatenate(outs[i * ATTN_HEADS:(i + 1) * ATTN_HEADS], axis=1) for i in range(ATTN_STEP_TILES)]
    o_ref[0] = _rms(jnp.concatenate(tiles, axis=0), gain_ref[...]).astype(o_ref.dtype)


def _attn_bias_table(rel_bias):
    n_keys = ATTN_KEY_BLOCKS * ATTN_TILE
    left = ATTN_LEFT * ATTN_CHUNK
    period = n_keys + ATTN_TILE
    rb = rel_bias.astype(F32)
    heads = rb.shape[0]
    far = lambda width: jnp.broadcast_to(rb[:, 2 * MAX_REL:], (heads, width))
    g = jnp.concatenate([far(left - MAX_REL), rb[:, 1:][:, ::-1], far(period - left - MAX_REL)], axis=1)
    toeplitz = jnp.tile(g, (1, ATTN_TILE))[:, :ATTN_TILE * (period - 1)].reshape(heads, ATTN_TILE, period - 1)
    q = np.arange(ATTN_TILE)[:, None]
    j = np.arange(n_keys)[None, :]
    in_band = (j // ATTN_CHUNK >= q // ATTN_CHUNK) & (j // ATTN_CHUNK <= q // ATTN_CHUNK + ATTN_LEFT)
    return jnp.where(in_band[None], toeplitz[:, :, :n_keys], jnp.finfo(F32).min)


def _chunk_attention(qkv, rel_bias, gain):
    b, s, _ = qkv.shape
    bias = _attn_bias_table(rel_bias)
    back = ATTN_KEY_BLOCKS - 1
    blocks = ATTN_KEY_BLOCKS + ATTN_STEP_TILES - 1
    step_rows = ATTN_STEP_TILES * ATTN_TILE
    key_specs = lambda col: [pl.BlockSpec((1, ATTN_TILE, ATTN_WIDTH),
                                          lambda i, c, j=j: (i, jnp.maximum(c * ATTN_STEP_TILES + j - back, 0), col))
                             for j in range(blocks)]
    return pl.pallas_call(
        _attn_kernel,
        name="chunk_attention",
        grid=(b, s // step_rows),
        in_specs=[pl.BlockSpec((1, step_rows, ATTN_WIDTH), lambda i, c: (i, c, 0))] + key_specs(1) + key_specs(2)
        + [_const_spec(bias.shape), _const_spec((1, ATTN_WIDTH))],
        out_specs=pl.BlockSpec((1, step_rows, ATTN_WIDTH), lambda i, c: (i, c, 0)),
        out_shape=jax.ShapeDtypeStruct((b, s, ATTN_WIDTH), BF16),
        compiler_params=_params("parallel", "parallel"),
    )(*([qkv] * (1 + 2 * blocks)), bias, gain.astype(F32).reshape(1, ATTN_WIDTH))


def _outproj_kernel(route, x_ref, orw_ref, os5_ref, oat_ref, w_ref, gpost_ref, gpre_ref, *refs):
    mixed = jnp.concatenate([orw_ref[...], os5_ref[...], oat_ref[...]], axis=1)
    y = jnp.dot(mixed, w_ref[...], preferred_element_type=F32)
    x = x_ref[...] + _rms(y, gpost_ref[...])
    h = _rms(x, gpre_ref[...])
    if not route:
        x_out, h_out = refs
    else:
        router_ref, x_out, h_out, route_out = refs
        logits = _mm(_terms(h, 2), _terms(router_ref[...], 2))
        lane = lax.broadcasted_iota(jnp.int32, logits.shape, 1).astype(F32)
        neg = jnp.finfo(F32).min
        logits = jnp.where(lane < N_EXPERTS, logits, neg)
        top1 = jnp.max(logits, axis=-1, keepdims=True)
        idx1 = jnp.min(jnp.where(logits == top1, lane, float(LANES)), axis=-1, keepdims=True)
        rest = jnp.where(lane == idx1, neg, logits)
        top2 = jnp.max(rest, axis=-1, keepdims=True)
        idx2 = jnp.min(jnp.where(rest == top2, lane, float(LANES)), axis=-1, keepdims=True)
        e2 = jnp.exp(top2 - top1)
        route_out[...] = (jnp.where(lane == 0, idx1, 0.0) + jnp.where(lane == 1, idx2, 0.0)
                          + jnp.where(lane == 2, 1.0 / (1.0 + e2), 0.0) + jnp.where(lane == 3, e2 / (1.0 + e2), 0.0))
    x_out[...] = x
    h_out[...] = h.astype(h_out.dtype)


def _out_projection(x, o_rwkv, o_s5, o_attn, w_out, gain_post, gain_pre, router):
    t, d = x.shape
    row = lambda n: pl.BlockSpec((ROW_TILE, n), lambda i: (i, 0))
    vec = lambda g: g.astype(F32).reshape(1, d)
    ins = [x, o_rwkv, o_s5, o_attn, w_out.astype(BF16), vec(gain_post), vec(gain_pre)]
    in_specs = [row(d), row(o_rwkv.shape[1]), row(o_s5.shape[1]), row(o_attn.shape[1]),
                _const_spec(w_out.shape), _const_spec((1, d)), _const_spec((1, d))]
    out_specs = [row(d), row(d)]
    out_shape = [jax.ShapeDtypeStruct((t, d), F32), jax.ShapeDtypeStruct((t, d), BF16 if router is None else F32)]
    if router is not None:
        ins.append(jnp.pad(router.astype(F32), ((0, 0), (0, LANES - N_EXPERTS))))
        in_specs.append(_const_spec((d, LANES)))
        out_specs.append(row(LANES))
        out_shape.append(jax.ShapeDtypeStruct((t, LANES), F32))
    return pl.pallas_call(
        functools.partial(_outproj_kernel, router is not None),
        name="out_projection",
        grid=(t // ROW_TILE,),
        in_specs=in_specs, out_specs=out_specs, out_shape=out_shape,
        compiler_params=_params("parallel"),
    )(*ins)


def _ffn_kernel(x_ref, h_ref, wg_ref, wu_ref, wd_ref, gain_ref, o_ref):
    h = h_ref[...]
    y = None
    for f in range(wg_ref.shape[1] // FFN_TILE):
        cols = slice(f * FFN_TILE, (f + 1) * FFN_TILE)
        act = (jax.nn.silu(jnp.dot(h, wg_ref[:, cols], preferred_element_type=F32))
               * jnp.dot(h, wu_ref[:, cols], preferred_element_type=F32))
        part = jnp.dot(act.astype(BF16), wd_ref[cols, :], preferred_element_type=F32)
        y = part if y is None else y + part
    o_ref[...] = x_ref[...] + _rms(y, gain_ref[...])


def _ffn(x, h, w_gate, w_up, w_down, gain):
    t, d = x.shape
    d_ff = w_gate.shape[1]
    row = lambda n: pl.BlockSpec((ROW_TILE, n), lambda i: (i, 0))
    resident = lambda shape: pl.BlockSpec(shape, lambda i: (0, 0), pipeline_mode=pl.Buffered(1))
    return pl.pallas_call(
        _ffn_kernel,
        name="swiglu_ffn",
        grid=(t // ROW_TILE,),
        in_specs=[row(d), row(d), resident((d, d_ff)), resident((d, d_ff)), resident((d_ff, d)), _const_spec((1, d))],
        out_specs=row(d),
        out_shape=jax.ShapeDtypeStruct((t, d), F32),
        compiler_params=_params("parallel"),
    )(x, h, w_gate.astype(BF16), w_up.astype(BF16), w_down.astype(BF16), gain.astype(F32).reshape(1, d))


MOE_TILE = 512


def _moe_plan(route, n_experts):
    t = route.shape[0]
    n = 2 * t
    tm = MOE_TILE
    expert = route[:, :2].astype(jnp.int32).reshape(n)
    onehot = (expert[:, None] == jnp.arange(n_experts, dtype=jnp.int32)[None, :]).astype(jnp.int32)
    rank = jnp.sum(jnp.cumsum(onehot, axis=0) * onehot, axis=1) - 1
    counts = jnp.sum(onehot, axis=0)
    padded = (counts + tm - 1) // tm * tm
    ends = jnp.cumsum(padded)
    slot = jnp.sum((ends - padded)[None, :] * onehot, axis=1) + rank
    n_tiles = n // tm + n_experts
    slots = n_tiles * tm
    ids = jnp.arange(n, dtype=jnp.int32)
    every = jnp.arange(slots, dtype=jnp.int32)
    dump = n + every % tm + tm * ((every // tm) % 2)
    dst = dump.at[slot].set((ids % 2) * t + ids // 2, unique_indices=True)
    tok = jnp.where(dst < n, dst % t, 0)
    tile_start = jnp.arange(n_tiles, dtype=jnp.int32) * tm
    tile_expert = jnp.minimum(jnp.sum((tile_start[:, None] >= ends[None, :]).astype(jnp.int32), axis=1), n_experts - 1)
    tile_valid = (tile_start < ends[-1]).astype(jnp.int32)
    dst = dst.reshape(n_tiles, 1, tm)
    before_first = (n + tm + jnp.arange(tm, dtype=jnp.int32)).reshape(1, 1, tm)
    return tok.reshape(n_tiles, 1, tm), dst, jnp.concatenate([before_first, dst[:-1]], axis=0), tile_expert, tile_valid


def _moe_ffn_kernel(te_ref, tv_ref, tok0_ref, tokn_ref, dstp_ref, dst_ref, h_hbm, wg_ref, wu_ref, wd_ref,
                    out_hbm, xbuf, xb, ybuf, acc_ref, gsem, ssem):
    t, f = pl.program_id(0), pl.program_id(1)
    n_t, n_f = pl.num_programs(0), pl.num_programs(1)
    tm = MOE_TILE
    slot = t % 2
    valid = tv_ref[t] == 1
    prev_valid = tv_ref[jnp.maximum(t - 1, 0)] == 1
    last = t == n_t - 1

    gather_row = lambda tok_ref, s, i: pltpu.make_async_copy(
        h_hbm.at[pl.ds(tok_ref[0, 0, i], 1), :], xbuf.at[s, pl.ds(i, 1), :], gsem.at[s])
    scatter_row = lambda d_ref, s, i: pltpu.make_async_copy(
        ybuf.at[s, pl.ds(i, 1), :], out_hbm.at[pl.ds(d_ref[0, 0, i], 1), :], ssem.at[s])
    wait_gather = lambda s: pltpu.make_async_copy(h_hbm.at[pl.ds(0, tm), :], xbuf.at[s], gsem.at[s]).wait()
    wait_scatter = lambda s: pltpu.make_async_copy(ybuf.at[s], out_hbm.at[pl.ds(0, tm), :], ssem.at[s]).wait()

    def looped(copy, idx_ref, s):
        def body(i, carry):
            copy(idx_ref, s, i).start()
            return carry
        lax.fori_loop(0, tm, body, 0, unroll=8)

    def compute():
        x = xb[...]
        act = jax.nn.silu(jnp.dot(x, wg_ref[0], preferred_element_type=F32)) * jnp.dot(x, wu_ref[0], preferred_element_type=F32)
        return jnp.dot(act.astype(BF16), wd_ref[0], preferred_element_type=F32)

    @pl.when((f == 0) & (t == 0))
    def _():
        looped(gather_row, tok0_ref, 0)
        ybuf[1] = jnp.zeros(ybuf.shape[1:], F32)
        for bank in range(2):
            fill = pltpu.make_async_copy(ybuf.at[1], out_hbm.at[pl.ds(out_hbm.shape[0] - (bank + 1) * tm, tm), :],
                                         ssem.at[1])
            fill.start()
            fill.wait()

    @pl.when((f == 0) & ((t == 0) | prev_valid))
    def _():
        wait_gather(slot)

    @pl.when((f == 0) & valid)
    def _():
        xb[...] = xbuf[slot].astype(BF16)
        for i in range(tm):
            gather_row(tokn_ref, 1 - slot, i).start()
        acc_ref[...] = compute()

    @pl.when((f == n_f - 1) & ((t == 1) | ((t >= 2) & (tv_ref[jnp.maximum(t - 2, 0)] == 1))))
    def _():
        wait_scatter(slot)

    @pl.when((f == n_f - 1) & valid)
    def _():
        for i in range(tm):
            scatter_row(dstp_ref, 1 - slot, i).start()
        ybuf[slot] = acc_ref[...] + compute()

    @pl.when((f == n_f - 1) & jnp.logical_not(valid) & (t >= 1) & prev_valid)
    def _():
        looped(scatter_row, dstp_ref, 1 - slot)

    @pl.when((f == n_f - 1) & last)
    def _():
        @pl.when(valid)
        def _():
            looped(scatter_row, dst_ref, slot)
            wait_scatter(slot)
            wait_gather(1 - slot)

        @pl.when(prev_valid)
        def _():
            wait_scatter(1 - slot)


def _moe_combine_kernel(x_ref, y1_ref, y2_ref, route_ref, gain_ref, o_ref):
    y = route_ref[:, 2:3] * y1_ref[...] + route_ref[:, 3:4] * y2_ref[...]
    o_ref[...] = x_ref[...] + _rms(y, gain_ref[...])


def _moe(x, h, route, w_gate, w_up, w_down, gain):
    t, d = x.shape
    n_e, _, d_ff = w_gate.shape
    tm = MOE_TILE
    assert d_ff == 2 * FFN_TILE
    tok, dst, dst_prev, tile_expert, tile_valid = _moe_plan(route, n_e)
    n_tiles = tok.shape[0]
    idx_spec = lambda shift: pl.BlockSpec(
        (1, 1, tm), lambda i, f, te, tv: (jnp.clip(i + shift, 0, n_tiles - 1), 0, 0), memory_space=pltpu.SMEM)
    grid_spec = pltpu.PrefetchScalarGridSpec(
        num_scalar_prefetch=2,
        grid=(n_tiles, d_ff // FFN_TILE),
        in_specs=[idx_spec(0), idx_spec(1), idx_spec(0), idx_spec(0),
                  pl.BlockSpec(memory_space=pl.ANY),
                  pl.BlockSpec((1, d, FFN_TILE), lambda i, f, te, tv: (te[i], 0, f)),
                  pl.BlockSpec((1, d, FFN_TILE), lambda i, f, te, tv: (te[i], 0, f)),
                  pl.BlockSpec((1, FFN_TILE, d), lambda i, f, te, tv: (te[i], f, 0))],
        out_specs=pl.BlockSpec(memory_space=pl.ANY),
        scratch_shapes=[pltpu.VMEM((2, tm, d), F32), pltpu.VMEM((tm, d), BF16), pltpu.VMEM((2, tm, d), F32),
                        pltpu.VMEM((tm, d), F32), pltpu.SemaphoreType.DMA((2,)), pltpu.SemaphoreType.DMA((2,))],
    )
    routed = pl.pallas_call(
        _moe_ffn_kernel,
        name="moe_experts",
        grid_spec=grid_spec,
        out_shape=jax.ShapeDtypeStruct((2 * t + 2 * tm, d), F32),
        compiler_params=_params("arbitrary", "arbitrary"),
    )(tile_expert, tile_valid, tok, tok, dst_prev, dst, h, w_gate.astype(BF16), w_up.astype(BF16), w_down.astype(BF16))
    row = lambda n: pl.BlockSpec((ROW_TILE, n), lambda i: (i, 0))
    second = pl.BlockSpec((ROW_TILE, d), lambda i: (i + t // ROW_TILE, 0))
    return pl.pallas_call(
        _moe_combine_kernel,
        name="moe_combine",
        grid=(t // ROW_TILE,),
        in_specs=[row(d), row(d), second, row(LANES), _const_spec((1, d))],
        out_specs=row(d),
        out_shape=jax.ShapeDtypeStruct((t, d), F32),
        compiler_params=_params("parallel"),
    )(x, routed, routed, route, gain.astype(F32).reshape(1, d))


def kernel(x, norm_mix_pre, norm_mix_post, norm_ffn_pre, norm_ffn_post, w_in, w_out, rwkv_mu, rwkv_w0, rwkv_w_up, rwkv_a0, rwkv_a_up, rwkv_g_up, rwkv_k_k, rwkv_k_a, rwkv_r_k, rwkv_ln_w, rwkv_ln_b, s5_a_re, s5_a_im, s5_log_step, s5_b_re, s5_b_im, s5_c_re, s5_c_im, s5_d, s5_w_glu, s5_b_glu, s5_norm, attn_rel_bias, attn_norm, ffn_w_gate, ffn_w_up, ffn_w_down, moe_router, moe_w_gate, moe_w_up, moe_w_down):
    b, s, d = x.shape
    depth = w_in.shape[0]
    xt = x.reshape(b * s, d)
    for layer in range(depth):
        p, u, qkv = _in_projection(xt, norm_mix_pre[layer], w_in[layer])
        o_rwkv = _rwkv_mix(p.reshape(b, s, -1), rwkv_mu[layer], rwkv_w0[layer], rwkv_w_up[layer], rwkv_a0[layer],
                           rwkv_a_up[layer], rwkv_g_up[layer], rwkv_k_k[layer], rwkv_k_a[layer], rwkv_r_k[layer],
                           rwkv_ln_w[layer], rwkv_ln_b[layer])
        o_s5 = _s5_mix(u.reshape(b, s, -1), s5_a_re[layer], s5_a_im[layer], s5_log_step[layer], s5_b_re[layer],
                       s5_b_im[layer], s5_c_re[layer], s5_c_im[layer], s5_d[layer], s5_w_glu[layer],
                       s5_b_glu[layer], s5_norm[layer])
        o_attn = _chunk_attention(qkv.reshape(b, s, -1), attn_rel_bias[layer], attn_norm[layer])
        i = layer // 2
        moe = layer % 2 == 1
        outs = _out_projection(xt, o_rwkv.reshape(b * s, -1), o_s5.reshape(b * s, -1), o_attn.reshape(b * s, -1),
                               w_out[layer], norm_mix_post[layer], norm_ffn_pre[layer],
                               moe_router[i] if moe else None)
        if moe:
            xt, h, route = outs
            xt = _moe(xt, h, route, moe_w_gate[i], moe_w_up[i], moe_w_down[i], norm_ffn_post[layer])
        else:
            xt, h = outs
            xt = _ffn(xt, h, ffn_w_gate[i], ffn_w_up[i], ffn_w_down[i], norm_ffn_post[layer])
    return xt.reshape(b, s, d)
```

```python
import functools
import math

import jax
import jax.numpy as jnp
import numpy as np
from jax import lax
from jax.experimental import pallas as pl
from jax.experimental.pallas import tpu as pltpu

F32 = jnp.float32
BF16 = jnp.bfloat16

RMS_EPS = 1e-6
RWKV_GN_EPS = 64e-5
HEAD_DIM = 64
RWKV_HEADS = 6
RWKV_WIDTH = 384
RWKV_IN = 1408
S5_WIDTH = 256
S5_GROUPS = 16
S5_STATE = 64
S5_LANES = S5_GROUPS * S5_STATE
ATTN_HEADS = 6
ATTN_WIDTH = 384
ATTN_CHUNK = 64
ATTN_LEFT = 8
MAX_REL = 128
N_EXPERTS = 8
LANES = 128

RWKV_CHUNK = 64
RWKV_STEP_CHUNKS = 2
S5_CHUNK = 256
SUBLANES = 8
ATTN_TILE = 2 * ATTN_CHUNK
ATTN_KEY_BLOCKS = (ATTN_LEFT * ATTN_CHUNK) // ATTN_TILE + 1
ATTN_STEP_TILES = 2
ROW_TILE = 512
FFN_TILE = 1408
VMEM_LIMIT = 48 * 1024 * 1024
MOE_VMEM_LIMIT = 58 * 1024 * 1024

_HI = lax.Precision.HIGHEST
_CONTRACT = {"nn": (((1,), (0,)), ((), ())), "nt": (((1,), (1,)), ((), ())), "tn": (((0,), (0,)), ((), ()))}


def _hdot(a, b):
    return jnp.dot(a, b, precision=_HI, preferred_element_type=F32)


def _terms(x, n):
    out = []
    for i in range(n):
        t = x.astype(BF16)
        out.append(t)
        if i + 1 < n:
            x = x - t.astype(F32)
    return tuple(out)


def _mm(a, b, kind="nn"):
    order = max(len(a), len(b))
    acc = None
    for i, ai in enumerate(a):
        for j, bj in enumerate(b):
            if i + j < order:
                d = lax.dot_general(ai, bj, _CONTRACT[kind], preferred_element_type=F32)
                acc = d if acc is None else acc + d
    return acc


def _rms(x, gain):
    return x * lax.rsqrt(jnp.mean(x * x, axis=-1, keepdims=True) + RMS_EPS) * gain


def _params(*semantics):
    return pltpu.CompilerParams(dimension_semantics=semantics, vmem_limit_bytes=VMEM_LIMIT)


def _const_spec(shape):
    zeros = (0,) * len(shape)
    return pl.BlockSpec(shape, lambda *_: zeros)


def _inproj_kernel(x_ref, gain_ref, w_ref, p_ref, u_ref, qkv_ref):
    h = _rms(x_ref[...], gain_ref[...]).astype(BF16)
    s5_at, qkv_at = RWKV_IN, RWKV_IN + S5_WIDTH
    p_ref[...] = jnp.dot(h, w_ref[:, :s5_at], preferred_element_type=F32)
    u_ref[...] = jnp.dot(h, w_ref[:, s5_at:qkv_at], preferred_element_type=F32)
    qkv_ref[...] = jnp.dot(h, w_ref[:, qkv_at:], preferred_element_type=F32).astype(BF16)


def _in_projection(x, gain, w_in):
    t, d = x.shape
    n_in = w_in.shape[1]
    n_qkv = n_in - RWKV_IN - S5_WIDTH
    row = lambda n: pl.BlockSpec((ROW_TILE, n), lambda i: (i, 0))
    return pl.pallas_call(
        _inproj_kernel,
        name="in_projection",
        grid=(t // ROW_TILE,),
        in_specs=[row(d), _const_spec((1, d)), _const_spec((d, n_in))],
        out_specs=[row(RWKV_IN), row(S5_WIDTH), row(n_qkv)],
        out_shape=[jax.ShapeDtypeStruct((t, RWKV_IN), F32), jax.ShapeDtypeStruct((t, S5_WIDTH), F32),
                   jax.ShapeDtypeStruct((t, n_qkv), BF16)],
        compiler_params=_params("parallel"),
    )(x, gain.reshape(1, d), w_in.astype(BF16))


RWKV_TERMS = 1
HEADS_PER_VREG = LANES // HEAD_DIM


def _t(x):
    return _terms(x, RWKV_TERMS)


def _head_sums(x):
    first = lax.broadcasted_iota(jnp.int32, (1, LANES), 1) < HEAD_DIM
    outs = []
    for j in range(x.shape[1] // LANES):
        xp = x[:, j * LANES:(j + 1) * LANES]
        s0 = jnp.sum(jnp.where(first, xp, 0.0), axis=-1, keepdims=True)
        s1 = jnp.sum(jnp.where(first, 0.0, xp), axis=-1, keepdims=True)
        outs.append(jnp.where(first, s0, s1))
    return jnp.concatenate(outs, axis=1)


def _rwkv_kernel(p_ref, mu_ref, w0_ref, wup_ref, a0_ref, aup_ref, gup_ref, kk_ref, ka_ref, rk_ref,
                 lnw_ref, lnb_ref, o_ref, prev_ref, state_ref):
    nb, per_seq = p_ref.shape[0], p_ref.shape[1]
    n = RWKV_CHUNK
    rows = nb * per_seq
    items = rows // n
    w_at = 3 * RWKV_WIDTH
    pairs = RWKV_WIDTH // LANES

    @pl.when(pl.program_id(0) == 0)
    def _():
        prev_ref[...] = jnp.zeros_like(prev_ref)
        state_ref[...] = jnp.zeros_like(state_ref)

    p = p_ref[...].reshape(rows, RWKV_IN)
    row = lax.broadcasted_iota(jnp.int32, (rows, 1), 0)
    shifted = pltpu.roll(p, 1, axis=0)
    for b in range(nb):
        shifted = jnp.where(row == b * per_seq, prev_ref[b:b + 1, :], shifted)
    for b in range(nb):
        prev_ref[b:b + 1, :] = p[(b + 1) * per_seq - 1:(b + 1) * per_seq, :]
    p = p + (shifted - p) * mu_ref[...]

    r = p[:, :RWKV_WIDTH]
    k = p[:, RWKV_WIDTH:2 * RWKV_WIDTH]
    v = p[:, 2 * RWKV_WIDTH:w_at]
    w_lo = p[:, w_at:w_at + 64]
    a_lo = p[:, w_at + 64:w_at + 128]
    g_lo = p[:, w_at + 128:]

    lora = lambda x, w_ref: _mm(_t(x), tuple(w_ref[i] for i in range(RWKV_TERMS)))
    w = -jax.nn.softplus(-(w0_ref[...] + lora(jnp.tanh(w_lo), wup_ref))) - 0.5
    log_decay = -jnp.exp(w)
    a = jax.nn.sigmoid(a0_ref[...] + lora(a_lo, aup_ref))
    gate = lora(jax.nn.sigmoid(g_lo), gup_ref)

    kk = k * kk_ref[...]
    kk = kk * lax.rsqrt(jnp.maximum(_head_sums(kk * kk), 1e-24))
    k = k * (1.0 + (a - 1.0) * ka_ref[...])

    s_i = lax.broadcasted_iota(jnp.int32, (rows, rows), 0)
    s_j = lax.broadcasted_iota(jnp.int32, (rows, rows), 1)
    causal = ((s_i // n == s_j // n) & (s_i >= s_j)).astype(BF16)
    cum = _mm((causal,), _terms(log_decay, 3))
    cum_end = cum[n - 1:n, :]
    for it in range(1, items):
        cum_end = jnp.where(row >= it * n, cum[(it + 1) * n - 1:(it + 1) * n, :], cum_end)
    inv_g = jnp.exp(-cum)
    to_end = jnp.exp(cum_end - cum)
    beta = kk * a
    r_bar = r * jnp.exp(cum)
    a_bar = -kk * jnp.exp(cum - log_decay)
    b_til = beta * inv_g
    k_til = k * inv_g
    b_end = beta * to_end
    k_end = k * to_end
    g_end = jnp.exp(cum_end)

    lane = lax.broadcasted_iota(jnp.int32, (1, LANES), 1)
    q_i = lax.broadcasted_iota(jnp.int32, (n, LANES), 0)
    q_j = lax.broadcasted_iota(jnp.int32, (n, LANES), 1) % n
    lower = q_i >= q_j
    strictly_lower = q_i > q_j
    eye = (lax.broadcasted_iota(jnp.int32, (n, n), 0) == lax.broadcasted_iota(jnp.int32, (n, n), 1)).astype(F32)
    c_i = lax.broadcasted_iota(jnp.int32, (LANES, LANES), 0)
    c_j = lax.broadcasted_iota(jnp.int32, (LANES, LANES), 1)
    same_head = c_i // HEAD_DIM == c_j // HEAD_DIM
    diagonal = c_i == c_j
    zeros = jnp.zeros((n, LANES), F32)

    pair_ids = [(it, j) for it in range(items) for j in range(pairs)]
    head_ids = [(pi, h) for pi in range(len(pair_ids)) for h in range(HEADS_PER_VREG)]
    heads_of = lambda pi: [hi for hi, (p_i, _) in enumerate(head_ids) if p_i == pi]
    cut = lambda x, pi: x[pair_ids[pi][0] * n:(pair_ids[pi][0] + 1) * n,
                          pair_ids[pi][1] * LANES:(pair_ids[pi][1] + 1) * LANES]
    rb = [cut(r_bar, pi) for pi in range(len(pair_ids))]
    be = [cut(b_end, pi) for pi in range(len(pair_ids))]
    vv = [cut(v, pi) for pi in range(len(pair_ids))]
    bk = [_t(jnp.concatenate([cut(b_til, pi), cut(k_til, pi)], axis=0)) for pi in range(len(pair_ids))]
    mine = [lane // HEAD_DIM == h for _, h in head_ids]
    abm = [jnp.where(mine[hi], cut(a_bar, pi), 0.0) for hi, (pi, _) in enumerate(head_ids)]
    v0 = [_t(jnp.concatenate([zeros, jnp.where(mine[hi], vv[pi], 0.0)], axis=0)) for hi, (pi, _) in enumerate(head_ids)]
    m = [_mm(_t(jnp.concatenate([abm[hi], jnp.where(mine[hi], rb[pi], 0.0)], axis=0)), bk[pi], "nt")
         for hi, (pi, _) in enumerate(head_ids)]
    top = [jnp.where(strictly_lower, x[:n], 0.0) for x in m]
    bot = [jnp.where(lower, x[n:], 0.0) for x in m]
    y = [_mm(_t(top[hi]), v0[hi]) for hi in range(len(head_ids))]
    power = [x[:, :n] for x in top]
    inv = [eye + x for x in power]
    for _ in range(int(math.log2(n)) - 1):
        pt = [_t(x) for x in power]
        power = [_mm(x, x) for x in pt]
        inv = [x + _mm(_t(x), _t(pw)) for x, pw in zip(inv, power)]
    wu = [_mm(_t(inv[hi]), _t(jnp.concatenate([abm[hi], y[hi]], axis=1))) for hi in range(len(head_ids))]
    v_low = [tuple(jnp.concatenate([jnp.zeros((n, LANES), BF16), t[n:]], axis=1) for t in v0[hi])
             for hi in range(len(head_ids))]
    ro = [_mm(_t(bot[hi]), tuple(jnp.concatenate([a, b2], axis=0) for a, b2 in zip(_t(wu[hi]), v_low[hi])))
          for hi in range(len(head_ids))]
    both = lambda xs, pi, sl: sum(xs[hi][:, sl] for hi in heads_of(pi))
    first, second = slice(0, LANES), slice(LANES, 2 * LANES)
    g_t = [jnp.where(same_head, _mm(_t(be[pi]), _t(both(wu, pi, first)), "tn"), 0.0)
           + jnp.where(diagonal, cut(g_end, pi)[:1], 0.0) for pi in range(len(pair_ids))]
    s2_t = [_mm(_t(jnp.concatenate([be[pi], cut(k_end, pi)], axis=0)),
                _t(jnp.concatenate([both(wu, pi, second), vv[pi]], axis=0)), "tn") for pi in range(len(pair_ids))]
    r2 = [_t(rb[pi] + both(ro, pi, first)) for pi in range(len(pair_ids))]
    out = [None] * len(pair_ids)
    per_step = items // nb
    for b in range(nb):
        for j in range(pairs):
            state = state_ref[b * pairs + j]
            for c in range(per_step):
                pi = (b * per_step + c) * pairs + j
                out[pi] = _mm(r2[pi], _t(state)) + both(ro, pi, second)
                state = _mm(_t(g_t[pi]), _t(state)) + jnp.where(same_head, s2_t[pi], 0.0)
            state_ref[b * pairs + j] = state
    o = jnp.concatenate([jnp.concatenate(out[it * pairs:(it + 1) * pairs], axis=1) for it in range(items)], axis=0)

    inv_n = 1.0 / HEAD_DIM
    cen = o - _head_sums(o) * inv_n
    var = _head_sums(cen * cen) * inv_n
    o = cen * lax.rsqrt(var + RWKV_GN_EPS) * lnw_ref[...] + lnb_ref[...]
    o = o + _head_sums(r * k * rk_ref[...]) * v
    o_ref[...] = (o * gate).astype(o_ref.dtype).reshape(nb, per_seq, RWKV_WIDTH)


def _rwkv_mix(p, mu, w0, w_up, a0, a_up, g_up, k_k, k_a, r_k, ln_w, ln_b):
    b, s, _ = p.shape
    per_step = RWKV_CHUNK * RWKV_STEP_CHUNKS
    vec = lambda t: t.astype(F32).reshape(1, -1)
    split = lambda t: jnp.stack(_terms(t.astype(F32), RWKV_TERMS))
    consts = [vec(mu), vec(w0), split(w_up), vec(a0), split(a_up), split(g_up), vec(k_k), vec(k_a), vec(r_k),
              vec(ln_w), vec(ln_b)]
    return pl.pallas_call(
        _rwkv_kernel,
        name="rwkv7",
        grid=(s // per_step,),
        in_specs=[pl.BlockSpec((b, per_step, RWKV_IN), lambda c: (0, c, 0))]
        + [_const_spec(t.shape) for t in consts],
        out_specs=pl.BlockSpec((b, per_step, RWKV_WIDTH), lambda c: (0, c, 0)),
        out_shape=jax.ShapeDtypeStruct((b, s, RWKV_WIDTH), BF16),
        scratch_shapes=[pltpu.VMEM((b, RWKV_IN), F32),
                        pltpu.VMEM((b * RWKV_WIDTH // LANES, LANES, LANES), F32)],
        compiler_params=_params("arbitrary"),
    )(p, *consts)


def _s5_prep_kernel(are_ref, aim_ref, lstep_ref, bre_ref, bim_ref, bmat_ref, pw_ref, step_ref):
    lam_re = jnp.minimum(are_ref[...], -1e-4)
    lam_im = aim_ref[...]
    step = jnp.exp(lstep_ref[...])
    z_re, z_im = lam_re * step, lam_im * step
    mag = jnp.exp(z_re)
    num_re, num_im = mag * jnp.cos(z_im) - 1.0, mag * jnp.sin(z_im)
    den = lam_re * lam_re + lam_im * lam_im
    coef_re = (num_re * lam_re + num_im * lam_im) / den
    coef_im = (num_im * lam_re - num_re * lam_im) / den
    bre, bim = bre_ref[...], bim_ref[...]
    bmat_ref[:, :S5_LANES] = (coef_re * bre - coef_im * bim).astype(bmat_ref.dtype)
    bmat_ref[:, S5_LANES:] = (coef_re * bim + coef_im * bre).astype(bmat_ref.dtype)
    steps = (lax.broadcasted_iota(jnp.int32, (SUBLANES, 1), 0) + 1).astype(F32)
    pmag = jnp.exp(steps * z_re)
    pw_ref[:, :S5_LANES] = pmag * jnp.cos(steps * z_im)
    pw_ref[:, S5_LANES:] = pmag * jnp.sin(steps * z_im)
    sub = lax.broadcasted_iota(jnp.int32, (SUBLANES, 1), 0)
    for j in range(step_ref.shape[0]):
        dist = 2 ** j
        mag_j = jnp.where(sub >= dist, jnp.exp(dist * z_re), 0.0)
        step_ref[j, :, :S5_LANES] = mag_j * jnp.cos(dist * z_im)
        step_ref[j, :, S5_LANES:] = mag_j * jnp.sin(dist * z_im)


def _s5_kernel(u_ref, bmat_ref, pw_ref, step_ref, cre_ref, cim_ref, d_ref, wglu_ref, bglu_ref, gain_ref, o_ref, carry_ref):
    n = S5_CHUNK

    @pl.when(pl.program_id(1) == 0)
    def _():
        carry_ref[...] = jnp.zeros_like(carry_ref)

    u = u_ref[0]
    bu = jnp.dot(u.astype(BF16), bmat_ref[...], preferred_element_type=F32)
    x_re, x_im = bu[:, :S5_LANES], bu[:, S5_LANES:]
    groups = n // SUBLANES
    for j in range(step_ref.shape[0]):
        l_re = step_ref[j, :, :S5_LANES][None]
        l_im = step_ref[j, :, S5_LANES:][None]
        s_re = pltpu.roll(x_re, 2 ** j, axis=0).reshape(groups, SUBLANES, S5_LANES)
        s_im = pltpu.roll(x_im, 2 ** j, axis=0).reshape(groups, SUBLANES, S5_LANES)
        x_re = x_re + (l_re * s_re - l_im * s_im).reshape(n, S5_LANES)
        x_im = x_im + (l_re * s_im + l_im * s_re).reshape(n, S5_LANES)
    c_re, c_im = carry_ref[:, :S5_LANES], carry_ref[:, S5_LANES:]
    p_re, p_im = pw_ref[:, :S5_LANES], pw_ref[:, S5_LANES:]
    groups_re, groups_im = [], []
    for g in range(n // SUBLANES):
        g_re, g_im = x_re[g * SUBLANES:(g + 1) * SUBLANES], x_im[g * SUBLANES:(g + 1) * SUBLANES]
        g_re, g_im = g_re + p_re * c_re - p_im * c_im, g_im + p_re * c_im + p_im * c_re
        c_re, c_im = g_re[SUBLANES - 1:], g_im[SUBLANES - 1:]
        groups_re.append(g_re)
        groups_im.append(g_im)
    carry_ref[:, :S5_LANES] = c_re
    carry_ref[:, S5_LANES:] = c_im
    x_re, x_im = jnp.concatenate(groups_re, axis=0), jnp.concatenate(groups_im, axis=0)

    y = (jnp.dot(x_re.astype(BF16), cre_ref[...], preferred_element_type=F32)
         - jnp.dot(x_im.astype(BF16), cim_ref[...], preferred_element_type=F32)) + d_ref[...] * u
    y = jax.nn.gelu(y)
    y = y * jax.nn.sigmoid(jnp.dot(y.astype(BF16), wglu_ref[...], preferred_element_type=F32) + bglu_ref[...])
    o_ref[0] = _rms(y, gain_ref[...]).astype(o_ref.dtype)


def _block_diag(t):
    g, rows, cols = t.shape
    return jnp.einsum('grc,gh->grhc', t, jnp.eye(g, dtype=t.dtype)).reshape(g * rows, g * cols)


def _s5_mix(u, a_re, a_im, log_step, b_re, b_im, c_re, c_im, d, w_glu, b_glu, gain):
    b, s, width = u.shape
    lane = lambda t: t.astype(F32).reshape(1, S5_LANES)
    to_cp = lambda t: _block_diag(jnp.swapaxes(t.astype(F32), 1, 2))
    bmat, pw, steps = pl.pallas_call(
        _s5_prep_kernel,
        name="s5_discretise",
        out_shape=[jax.ShapeDtypeStruct((width, 2 * S5_LANES), BF16),
                   jax.ShapeDtypeStruct((SUBLANES, 2 * S5_LANES), F32),
                   jax.ShapeDtypeStruct((int(math.log2(SUBLANES)), SUBLANES, 2 * S5_LANES), F32)],
        compiler_params=pltpu.CompilerParams(vmem_limit_bytes=VMEM_LIMIT),
    )(lane(a_re), lane(a_im), lane(jnp.repeat(log_step[:, None], S5_STATE, axis=1)), to_cp(b_re), to_cp(b_im))
    cre = _block_diag(jnp.swapaxes(c_re, 1, 2)).astype(BF16)
    cim = _block_diag(jnp.swapaxes(c_im, 1, 2)).astype(BF16)
    vec = lambda t: t.astype(F32).reshape(1, width)
    consts = [bmat, pw, steps, cre, cim, vec(d), w_glu.astype(BF16), vec(b_glu), vec(gain)]
    return pl.pallas_call(
        _s5_kernel,
        name="s5_scan",
        grid=(b, s // S5_CHUNK),
        in_specs=[pl.BlockSpec((1, S5_CHUNK, width), lambda i, c: (i, c, 0))]
        + [_const_spec(t.shape) for t in consts],
        out_specs=pl.BlockSpec((1, S5_CHUNK, width), lambda i, c: (i, c, 0)),
        out_shape=jax.ShapeDtypeStruct((b, s, width), BF16),
        scratch_shapes=[pltpu.VMEM((1, 2 * S5_LANES), F32)],
        compiler_params=_params("parallel", "arbitrary"),
    )(u, *consts)


def _attn_kernel(q_ref, *refs):
    blocks = ATTN_KEY_BLOCKS + ATTN_STEP_TILES - 1
    k_refs = refs[:blocks]
    v_refs = refs[blocks:2 * blocks]
    bias_ref, gain_ref, o_ref = refs[2 * blocks:]
    n_keys = ATTN_KEY_BLOCKS * ATTN_TILE
    q_all = q_ref[0] * (HEAD_DIM ** -0.5)
    k_all = jnp.concatenate([kr[0] for kr in k_refs], axis=0)
    v_all = jnp.concatenate([vr[0] for vr in v_refs], axis=0)
    neg = jnp.finfo(F32).min
    in_band = bias_ref[0] != neg
    lane = lax.broadcasted_iota(jnp.int32, (1, n_keys), 1)
    heads = [slice(h * HEAD_DIM, (h + 1) * HEAD_DIM) for h in range(ATTN_HEADS)]
    work = []
    for i in range(ATTN_STEP_TILES):
        rows = slice(i * ATTN_TILE, (i + 1) * ATTN_TILE)
        keys = slice(i * ATTN_TILE, i * ATTN_TILE + n_keys)
        first_key = (pl.program_id(1) * ATTN_STEP_TILES + i - (ATTN_KEY_BLOCKS - 1)) * ATTN_TILE
        allowed = in_band & (first_key + lane >= 0)
        work += [(q_all[rows, sl], k_all[keys, sl], v_all[keys, sl], allowed, h) for h, sl in enumerate(heads)]
    s = [lax.dot_general(q, k, _CONTRACT["nt"], preferred_element_type=F32) for q, k, _, _, _ in work]
    s = [jnp.where(allowed, x + bias_ref[h], neg) for x, (_, _, _, allowed, h) in zip(s, work)]
    e = [jnp.exp(x - jnp.max(x, axis=-1, keepdims=True)) for x in s]
    pv = [jnp.dot(x.astype(BF16), v, preferred_element_type=F32) for x, (_, _, v, _, _) in zip(e, work)]
    outs = [x / jnp.sum(y, axis=-1, keepdims=True) for x, y in zip(pv, e)]
    tiles = [jnp.concatenate(outs[i * ATTN_HEADS:(i + 1) * ATTN_HEADS], axis=1) for i in range(ATTN_STEP_TILES)]
    o_ref[0] = _rms(jnp.concatenate(tiles, axis=0), gain_ref[...]).astype(o_ref.dtype)


def _attn_bias_table(rel_bias):
    n_keys = ATTN_KEY_BLOCKS * ATTN_TILE
    left = ATTN_LEFT * ATTN_CHUNK
    period = n_keys + ATTN_TILE
    rb = rel_bias.astype(F32)
    heads = rb.shape[0]
    far = lambda width: jnp.broadcast_to(rb[:, 2 * MAX_REL:], (heads, width))
    g = jnp.concatenate([far(left - MAX_REL), rb[:, 1:][:, ::-1], far(period - left - MAX_REL)], axis=1)
    toeplitz = jnp.tile(g, (1, ATTN_TILE))[:, :ATTN_TILE * (period - 1)].reshape(heads, ATTN_TILE, period - 1)
    q = np.arange(ATTN_TILE)[:, None]
    j = np.arange(n_keys)[None, :]
    in_band = (j // ATTN_CHUNK >= q // ATTN_CHUNK) & (j // ATTN_CHUNK <= q // ATTN_CHUNK + ATTN_LEFT)
    return jnp.where(in_band[None], toeplitz[:, :, :n_keys], jnp.finfo(F32).min)


def _chunk_attention(qkv, rel_bias, gain):
    b, s, _ = qkv.shape
    bias = _attn_bias_table(rel_bias)
    back = ATTN_KEY_BLOCKS - 1
    blocks = ATTN_KEY_BLOCKS + ATTN_STEP_TILES - 1
    step_rows = ATTN_STEP_TILES * ATTN_TILE
    key_specs = lambda col: [pl.BlockSpec((1, ATTN_TILE, ATTN_WIDTH),
                                          lambda i, c, j=j: (i, jnp.maximum(c * ATTN_STEP_TILES + j - back, 0), col))
                             for j in range(blocks)]
    return pl.pallas_call(
        _attn_kernel,
        name="chunk_attention",
        grid=(b, s // step_rows),
        in_specs=[pl.BlockSpec((1, step_rows, ATTN_WIDTH), lambda i, c: (i, c, 0))] + key_specs(1) + key_specs(2)
        + [_const_spec(bias.shape), _const_spec((1, ATTN_WIDTH))],
        out_specs=pl.BlockSpec((1, step_rows, ATTN_WIDTH), lambda i, c: (i, c, 0)),
        out_shape=jax.ShapeDtypeStruct((b, s, ATTN_WIDTH), BF16),
        compiler_params=_params("parallel", "parallel"),
    )(*([qkv] * (1 + 2 * blocks)), bias, gain.astype(F32).reshape(1, ATTN_WIDTH))


def _outproj_kernel(route, x_ref, orw_ref, os5_ref, oat_ref, w_ref, gpost_ref, gpre_ref, *refs):
    mixed = jnp.concatenate([orw_ref[...], os5_ref[...], oat_ref[...]], axis=1)
    y = jnp.dot(mixed, w_ref[...], preferred_element_type=F32)
    x = x_ref[...] + _rms(y, gpost_ref[...])
    h = _rms(x, gpre_ref[...])
    if not route:
        x_out, h_out = refs
    else:
        router_ref, x_out, h_out, route_out = refs
        logits = _mm(_terms(h, 2), _terms(router_ref[...], 2))
        lane = lax.broadcasted_iota(jnp.int32, logits.shape, 1).astype(F32)
        neg = jnp.finfo(F32).min
        logits = jnp.where(lane < N_EXPERTS, logits, neg)
        top1 = jnp.max(logits, axis=-1, keepdims=True)
        idx1 = jnp.min(jnp.where(logits == top1, lane, float(LANES)), axis=-1, keepdims=True)
        rest = jnp.where(lane == idx1, neg, logits)
        top2 = jnp.max(rest, axis=-1, keepdims=True)
        idx2 = jnp.min(jnp.where(rest == top2, lane, float(LANES)), axis=-1, keepdims=True)
        e2 = jnp.exp(top2 - top1)
        route_out[...] = (jnp.where(lane == 0, idx1, 0.0) + jnp.where(lane == 1, idx2, 0.0)
                          + jnp.where(lane == 2, 1.0 / (1.0 + e2), 0.0) + jnp.where(lane == 3, e2 / (1.0 + e2), 0.0))
    x_out[...] = x
    h_out[...] = h.astype(h_out.dtype)


def _out_projection(x, o_rwkv, o_s5, o_attn, w_out, gain_post, gain_pre, router):
    t, d = x.shape
    row = lambda n: pl.BlockSpec((ROW_TILE, n), lambda i: (i, 0))
    vec = lambda g: g.astype(F32).reshape(1, d)
    ins = [x, o_rwkv, o_s5, o_attn, w_out.astype(BF16), vec(gain_post), vec(gain_pre)]
    in_specs = [row(d), row(o_rwkv.shape[1]), row(o_s5.shape[1]), row(o_attn.shape[1]),
                _const_spec(w_out.shape), _const_spec((1, d)), _const_spec((1, d))]
    out_specs = [row(d), row(d)]
    out_shape = [jax.ShapeDtypeStruct((t, d), F32), jax.ShapeDtypeStruct((t, d), BF16 if router is None else F32)]
    if router is not None:
        ins.append(jnp.pad(router.astype(F32), ((0, 0), (0, LANES - N_EXPERTS))))
        in_specs.append(_const_spec((d, LANES)))
        out_specs.append(row(LANES))
        out_shape.append(jax.ShapeDtypeStruct((t, LANES), F32))
    return pl.pallas_call(
        functools.partial(_outproj_kernel, router is not None),
        name="out_projection",
        grid=(t // ROW_TILE,),
        in_specs=in_specs, out_specs=out_specs, out_shape=out_shape,
        compiler_params=_params("parallel"),
    )(*ins)


def _ffn_kernel(x_ref, h_ref, wg_ref, wu_ref, wd_ref, gain_ref, o_ref):
    h = h_ref[...]
    y = None
    for f in range(wg_ref.shape[1] // FFN_TILE):
        cols = slice(f * FFN_TILE, (f + 1) * FFN_TILE)
        act = (jax.nn.silu(jnp.dot(h, wg_ref[:, cols], preferred_element_type=F32))
               * jnp.dot(h, wu_ref[:, cols], preferred_element_type=F32))
        part = jnp.dot(act.astype(BF16), wd_ref[cols, :], preferred_element_type=F32)
        y = part if y is None else y + part
    o_ref[...] = x_ref[...] + _rms(y, gain_ref[...])


def _ffn(x, h, w_gate, w_up, w_down, gain):
    t, d = x.shape
    d_ff = w_gate.shape[1]
    row = lambda n: pl.BlockSpec((ROW_TILE, n), lambda i: (i, 0))
    resident = lambda shape: pl.BlockSpec(shape, lambda i: (0, 0), pipeline_mode=pl.Buffered(1))
    return pl.pallas_call(
        _ffn_kernel,
        name="swiglu_ffn",
        grid=(t // ROW_TILE,),
        in_specs=[row(d), row(d), resident((d, d_ff)), resident((d, d_ff)), resident((d_ff, d)), _const_spec((1, d))],
        out_specs=row(d),
        out_shape=jax.ShapeDtypeStruct((t, d), F32),
        compiler_params=_params("parallel"),
    )(x, h, w_gate.astype(BF16), w_up.astype(BF16), w_down.astype(BF16), gain.astype(F32).reshape(1, d))


MOE_TILE = 512


def _moe_plan(route, n_experts):
    t = route.shape[0]
    n = 2 * t
    tm = MOE_TILE
    expert = route[:, :2].astype(jnp.int32).reshape(n)
    onehot = (expert[:, None] == jnp.arange(n_experts, dtype=jnp.int32)[None, :]).astype(jnp.int32)
    rank = jnp.sum(jnp.cumsum(onehot, axis=0) * onehot, axis=1) - 1
    counts = jnp.sum(onehot, axis=0)
    padded = (counts + tm - 1) // tm * tm
    ends = jnp.cumsum(padded)
    slot = jnp.sum((ends - padded)[None, :] * onehot, axis=1) + rank
    n_tiles = n // tm + n_experts
    slots = n_tiles * tm
    ids = jnp.arange(n, dtype=jnp.int32)
    every = jnp.arange(slots, dtype=jnp.int32)
    dump = n + every % tm + tm * ((every // tm) % 2)
    dst = dump.at[slot].set((ids % 2) * t + ids // 2, unique_indices=True)
    tok = jnp.where(dst < n, dst % t, 0)
    tile_start = jnp.arange(n_tiles, dtype=jnp.int32) * tm
    tile_expert = jnp.minimum(jnp.sum((tile_start[:, None] >= ends[None, :]).astype(jnp.int32), axis=1), n_experts - 1)
    tile_valid = (tile_start < ends[-1]).astype(jnp.int32)
    dst = dst.reshape(n_tiles, 1, tm)
    before_first = (n + tm + jnp.arange(tm, dtype=jnp.int32)).reshape(1, 1, tm)
    return tok.reshape(n_tiles, 1, tm), dst, jnp.concatenate([before_first, dst[:-1]], axis=0), tile_expert, tile_valid


def _moe_ffn_kernel(te_ref, tv_ref, tok0_ref, tokn_ref, dstp_ref, dst_ref, h_hbm, wg_ref, wu_ref, wd_ref,
                    out_hbm, xbuf, xb, ybuf, acc_ref, gsem, ssem):
    t = pl.program_id(0)
    n_t = pl.num_programs(0)
    tm = MOE_TILE
    slot = t % 2
    valid = tv_ref[t] == 1
    prev_valid = tv_ref[jnp.maximum(t - 1, 0)] == 1
    last = t == n_t - 1

    gather_row = lambda tok_ref, s, i: pltpu.make_async_copy(
        h_hbm.at[pl.ds(tok_ref[0, 0, i], 1), :], xbuf.at[s, pl.ds(i, 1), :], gsem.at[s])
    scatter_row = lambda d_ref, s, i: pltpu.make_async_copy(
        ybuf.at[s, pl.ds(i, 1), :], out_hbm.at[pl.ds(d_ref[0, 0, i], 1), :], ssem.at[s])
    wait_gather = lambda s: pltpu.make_async_copy(h_hbm.at[pl.ds(0, tm), :], xbuf.at[s], gsem.at[s]).wait()
    wait_scatter = lambda s: pltpu.make_async_copy(ybuf.at[s], out_hbm.at[pl.ds(0, tm), :], ssem.at[s]).wait()

    def looped(copy, idx_ref, s):
        def body(i, carry):
            copy(idx_ref, s, i).start()
            return carry
        lax.fori_loop(0, tm, body, 0, unroll=8)

    @pl.when(t == 0)
    def _():
        looped(gather_row, tok0_ref, 0)
        ybuf[1] = jnp.zeros(ybuf.shape[1:], F32)
        for bank in range(2):
            fill = pltpu.make_async_copy(ybuf.at[1], out_hbm.at[pl.ds(out_hbm.shape[0] - (bank + 1) * tm, tm), :],
                                         ssem.at[1])
            fill.start()
            fill.wait()

    @pl.when((t == 0) | prev_valid)
    def _():
        wait_gather(slot)

    @pl.when(valid)
    def _():
        xb[...] = xbuf[slot].astype(BF16)
        for i in range(tm):
            gather_row(tokn_ref, 1 - slot, i).start()
        for i in range(tm):
            scatter_row(dstp_ref, 1 - slot, i).start()
        x = xb[...]
        y = None
        for f in range(wg_ref.shape[2] // FFN_TILE):
            cols = slice(f * FFN_TILE, (f + 1) * FFN_TILE)
            act = (jax.nn.silu(jnp.dot(x, wg_ref[0, :, cols], preferred_element_type=F32))
                   * jnp.dot(x, wu_ref[0, :, cols], preferred_element_type=F32))
            part = jnp.dot(act.astype(BF16), wd_ref[0, cols, :], preferred_element_type=F32)
            y = part if y is None else y + part
        acc_ref[...] = y

    @pl.when(jnp.logical_not(valid) & (t >= 1) & prev_valid)
    def _():
        looped(scatter_row, dstp_ref, 1 - slot)

    @pl.when((t == 1) | ((t >= 2) & (tv_ref[jnp.maximum(t - 2, 0)] == 1)))
    def _():
        wait_scatter(slot)

    @pl.when(valid)
    def _():
        ybuf[slot] = acc_ref[...]

    @pl.when(last)
    def _():
        @pl.when(valid)
        def _():
            looped(scatter_row, dst_ref, slot)
            wait_scatter(slot)
            wait_gather(1 - slot)

        @pl.when(prev_valid)
        def _():
            wait_scatter(1 - slot)


def _moe_combine_kernel(x_ref, y1_ref, y2_ref, route_ref, gain_ref, o_ref):
    y = route_ref[:, 2:3] * y1_ref[...] + route_ref[:, 3:4] * y2_ref[...]
    o_ref[...] = x_ref[...] + _rms(y, gain_ref[...])


def _moe(x, h, route, w_gate, w_up, w_down, gain):
    t, d = x.shape
    n_e, _, d_ff = w_gate.shape
    tm = MOE_TILE
    tok, dst, dst_prev, tile_expert, tile_valid = _moe_plan(route, n_e)
    n_tiles = tok.shape[0]
    idx_spec = lambda shift: pl.BlockSpec(
        (1, 1, tm), lambda i, te, tv: (jnp.clip(i + shift, 0, n_tiles - 1), 0, 0), memory_space=pltpu.SMEM)
    grid_spec = pltpu.PrefetchScalarGridSpec(
        num_scalar_prefetch=2,
        grid=(n_tiles,),
        in_specs=[idx_spec(0), idx_spec(1), idx_spec(0), idx_spec(0),
                  pl.BlockSpec(memory_space=pl.ANY),
                  pl.BlockSpec((1, d, d_ff), lambda i, te, tv: (te[i], 0, 0)),
                  pl.BlockSpec((1, d, d_ff), lambda i, te, tv: (te[i], 0, 0)),
                  pl.BlockSpec((1, d_ff, d), lambda i, te, tv: (te[i], 0, 0))],
        out_specs=pl.BlockSpec(memory_space=pl.ANY),
        scratch_shapes=[pltpu.VMEM((2, tm, d), F32), pltpu.VMEM((tm, d), BF16), pltpu.VMEM((2, tm, d), F32),
                        pltpu.VMEM((tm, d), F32), pltpu.SemaphoreType.DMA((2,)), pltpu.SemaphoreType.DMA((2,))],
    )
    routed = pl.pallas_call(
        _moe_ffn_kernel,
        name="moe_experts",
        grid_spec=grid_spec,
        out_shape=jax.ShapeDtypeStruct((2 * t + 2 * tm, d), F32),
        compiler_params=pltpu.CompilerParams(dimension_semantics=("arbitrary",), vmem_limit_bytes=MOE_VMEM_LIMIT),
    )(tile_expert, tile_valid, tok, tok, dst_prev, dst, h, w_gate.astype(BF16), w_up.astype(BF16), w_down.astype(BF16))
    row = lambda n: pl.BlockSpec((ROW_TILE, n), lambda i: (i, 0))
    second = pl.BlockSpec((ROW_TILE, d), lambda i: (i + t // ROW_TILE, 0))
    return pl.pallas_call(
        _moe_combine_kernel,
        name="moe_combine",
        grid=(t // ROW_TILE,),
        in_specs=[row(d), row(d), second, row(LANES), _const_spec((1, d))],
        out_specs=row(d),
        out_shape=jax.ShapeDtypeStruct((t, d), F32),
        compiler_params=_params("parallel"),
    )(x, routed, routed, route, gain.astype(F32).reshape(1, d))


def kernel(x, norm_mix_pre, norm_mix_post, norm_ffn_pre, norm_ffn_post, w_in, w_out, rwkv_mu, rwkv_w0, rwkv_w_up, rwkv_a0, rwkv_a_up, rwkv_g_up, rwkv_k_k, rwkv_k_a, rwkv_r_k, rwkv_ln_w, rwkv_ln_b, s5_a_re, s5_a_im, s5_log_step, s5_b_re, s5_b_im, s5_c_re, s5_c_im, s5_d, s5_w_glu, s5_b_glu, s5_norm, attn_rel_bias, attn_norm, ffn_w_gate, ffn_w_up, ffn_w_down, moe_router, moe_w_gate, moe_w_up, moe_w_down):
    b, s, d = x.shape
    depth = w_in.shape[0]
    xt = x.reshape(b * s, d)
    for layer in range(depth):
        p, u, qkv = _in_projection(xt, norm_mix_pre[layer], w_in[layer])
        o_rwkv = _rwkv_mix(p.reshape(b, s, -1), rwkv_mu[layer], rwkv_w0[layer], rwkv_w_up[layer], rwkv_a0[layer],
                           rwkv_a_up[layer], rwkv_g_up[layer], rwkv_k_k[layer], rwkv_k_a[layer], rwkv_r_k[layer],
                           rwkv_ln_w[layer], rwkv_ln_b[layer])
        o_s5 = _s5_mix(u.reshape(b, s, -1), s5_a_re[layer], s5_a_im[layer], s5_log_step[layer], s5_b_re[layer],
                       s5_b_im[layer], s5_c_re[layer], s5_c_im[layer], s5_d[layer], s5_w_glu[layer],
                       s5_b_glu[layer], s5_norm[layer])
        o_attn = _chunk_attention(qkv.reshape(b, s, -1), attn_rel_bias[layer], attn_norm[layer])
        i = layer // 2
        moe = layer % 2 == 1
        outs = _out_projection(xt, o_rwkv.reshape(b * s, -1), o_s5.reshape(b * s, -1), o_attn.reshape(b * s, -1),
                               w_out[layer], norm_mix_post[layer], norm_ffn_pre[layer],
                               moe_router[i] if moe else None)
        if moe:
            xt, h, route = outs
            xt = _moe(xt, h, route, moe_w_gate[i], moe_w_up[i], moe_w_down[i], norm_ffn_post[layer])
        else:
            xt, h = outs
            xt = _ffn(xt, h, ffn_w_gate[i], ffn_w_up[i], ffn_w_down[i], norm_ffn_post[layer])
    return xt.reshape(b, s, d)
```

```python
import functools
import math

import jax
import jax.numpy as jnp
import numpy as np
from jax import lax
from jax.experimental import pallas as pl
from jax.experimental.pallas import tpu as pltpu

F32 = jnp.float32
BF16 = jnp.bfloat16

RMS_EPS = 1e-6
RWKV_GN_EPS = 64e-5
HEAD_DIM = 64
RWKV_HEADS = 6
RWKV_WIDTH = 384
RWKV_IN = 1408
S5_WIDTH = 256
S5_GROUPS = 16
S5_STATE = 64
S5_LANES = S5_GROUPS * S5_STATE
ATTN_HEADS = 6
ATTN_WIDTH = 384
ATTN_CHUNK = 64
ATTN_LEFT = 8
MAX_REL = 128
N_EXPERTS = 8
LANES = 128

RWKV_CHUNK = 64
RWKV_STEP_CHUNKS = 2
S5_CHUNK = 256
SUBLANES = 8
ATTN_TILE = 2 * ATTN_CHUNK
ATTN_KEY_BLOCKS = (ATTN_LEFT * ATTN_CHUNK) // ATTN_TILE + 1
ATTN_STEP_TILES = 2
ROW_TILE = 512
FFN_TILE = 1408
VMEM_LIMIT = 48 * 1024 * 1024

_HI = lax.Precision.HIGHEST
_CONTRACT = {"nn": (((1,), (0,)), ((), ())), "nt": (((1,), (1,)), ((), ())), "tn": (((0,), (0,)), ((), ()))}


def _hdot(a, b):
    return jnp.dot(a, b, precision=_HI, preferred_element_type=F32)


def _terms(x, n):
    out = []
    for i in range(n):
        t = x.astype(BF16)
        out.append(t)
        if i + 1 < n:
            x = x - t.astype(F32)
    return tuple(out)


def _mm(a, b, kind="nn"):
    order = max(len(a), len(b))
    acc = None
    for i, ai in enumerate(a):
        for j, bj in enumerate(b):
            if i + j < order:
                d = lax.dot_general(ai, bj, _CONTRACT[kind], preferred_element_type=F32)
                acc = d if acc is None else acc + d
    return acc


def _rms(x, gain):
    return x * lax.rsqrt(jnp.mean(x * x, axis=-1, keepdims=True) + RMS_EPS) * gain


def _params(*semantics):
    return pltpu.CompilerParams(dimension_semantics=semantics, vmem_limit_bytes=VMEM_LIMIT)


def _const_spec(shape):
    zeros = (0,) * len(shape)
    return pl.BlockSpec(shape, lambda *_: zeros)


def _inproj_kernel(x_ref, gain_ref, w_ref, p_ref, u_ref, qkv_ref):
    h = _rms(x_ref[...], gain_ref[...]).astype(BF16)
    s5_at, qkv_at = RWKV_IN, RWKV_IN + S5_WIDTH
    p_ref[...] = jnp.dot(h, w_ref[:, :s5_at], preferred_element_type=F32)
    u_ref[...] = jnp.dot(h, w_ref[:, s5_at:qkv_at], preferred_element_type=F32)
    qkv_ref[...] = jnp.dot(h, w_ref[:, qkv_at:], preferred_element_type=F32).astype(BF16)


def _in_projection(x, gain, w_in):
    t, d = x.shape
    n_in = w_in.shape[1]
    n_qkv = n_in - RWKV_IN - S5_WIDTH
    row = lambda n: pl.BlockSpec((ROW_TILE, n), lambda i: (i, 0))
    return pl.pallas_call(
        _inproj_kernel,
        name="in_projection",
        grid=(t // ROW_TILE,),
        in_specs=[row(d), _const_spec((1, d)), _const_spec((d, n_in))],
        out_specs=[row(RWKV_IN), row(S5_WIDTH), row(n_qkv)],
        out_shape=[jax.ShapeDtypeStruct((t, RWKV_IN), F32), jax.ShapeDtypeStruct((t, S5_WIDTH), F32),
                   jax.ShapeDtypeStruct((t, n_qkv), BF16)],
        compiler_params=_params("parallel"),
    )(x, gain.reshape(1, d), w_in.astype(BF16))


RWKV_TERMS = 1
HEADS_PER_VREG = LANES // HEAD_DIM


def _t(x):
    return _terms(x, RWKV_TERMS)


def _head_sums(x):
    first = lax.broadcasted_iota(jnp.int32, (1, LANES), 1) < HEAD_DIM
    outs = []
    for j in range(x.shape[1] // LANES):
        xp = x[:, j * LANES:(j + 1) * LANES]
        s0 = jnp.sum(jnp.where(first, xp, 0.0), axis=-1, keepdims=True)
        s1 = jnp.sum(jnp.where(first, 0.0, xp), axis=-1, keepdims=True)
        outs.append(jnp.where(first, s0, s1))
    return jnp.concatenate(outs, axis=1)


def _rwkv_kernel(p_ref, mu_ref, w0_ref, wup_ref, a0_ref, aup_ref, gup_ref, kk_ref, ka_ref, rk_ref,
                 lnw_ref, lnb_ref, o_ref, prev_ref, state_ref):
    nb, per_seq = p_ref.shape[0], p_ref.shape[1]
    n = RWKV_CHUNK
    rows = nb * per_seq
    items = rows // n
    w_at = 3 * RWKV_WIDTH
    pairs = RWKV_WIDTH // LANES

    @pl.when(pl.program_id(0) == 0)
    def _():
        prev_ref[...] = jnp.zeros_like(prev_ref)
        state_ref[...] = jnp.zeros_like(state_ref)

    p = p_ref[...].reshape(rows, RWKV_IN)
    row = lax.broadcasted_iota(jnp.int32, (rows, 1), 0)
    shifted = pltpu.roll(p, 1, axis=0)
    for b in range(nb):
        shifted = jnp.where(row == b * per_seq, prev_ref[b:b + 1, :], shifted)
    for b in range(nb):
        prev_ref[b:b + 1, :] = p[(b + 1) * per_seq - 1:(b + 1) * per_seq, :]
    p = p + (shifted - p) * mu_ref[...]

    r = p[:, :RWKV_WIDTH]
    k = p[:, RWKV_WIDTH:2 * RWKV_WIDTH]
    v = p[:, 2 * RWKV_WIDTH:w_at]
    w_lo = p[:, w_at:w_at + 64]
    a_lo = p[:, w_at + 64:w_at + 128]
    g_lo = p[:, w_at + 128:]

    lora = lambda x, w_ref: _mm(_t(x), tuple(w_ref[i] for i in range(RWKV_TERMS)))
    w = -jax.nn.softplus(-(w0_ref[...] + lora(jnp.tanh(w_lo), wup_ref))) - 0.5
    log_decay = -jnp.exp(w)
    a = jax.nn.sigmoid(a0_ref[...] + lora(a_lo, aup_ref))
    gate = lora(jax.nn.sigmoid(g_lo), gup_ref)

    kk = k * kk_ref[...]
    kk = kk * lax.rsqrt(jnp.maximum(_head_sums(kk * kk), 1e-24))
    k = k * (1.0 + (a - 1.0) * ka_ref[...])

    s_i = lax.broadcasted_iota(jnp.int32, (rows, rows), 0)
    s_j = lax.broadcasted_iota(jnp.int32, (rows, rows), 1)
    causal = ((s_i // n == s_j // n) & (s_i >= s_j)).astype(BF16)
    cum = _mm((causal,), _terms(log_decay, 3))
    cum_end = cum[n - 1:n, :]
    for it in range(1, items):
        cum_end = jnp.where(row >= it * n, cum[(it + 1) * n - 1:(it + 1) * n, :], cum_end)
    inv_g = jnp.exp(-cum)
    to_end = jnp.exp(cum_end - cum)
    beta = kk * a
    r_bar = r * jnp.exp(cum)
    a_bar = -kk * jnp.exp(cum - log_decay)
    b_til = beta * inv_g
    k_til = k * inv_g
    b_end = beta * to_end
    k_end = k * to_end
    g_end = jnp.exp(cum_end)

    lane = lax.broadcasted_iota(jnp.int32, (1, LANES), 1)
    q_i = lax.broadcasted_iota(jnp.int32, (n, LANES), 0)
    q_j = lax.broadcasted_iota(jnp.int32, (n, LANES), 1) % n
    lower = q_i >= q_j
    strictly_lower = q_i > q_j
    eye = (lax.broadcasted_iota(jnp.int32, (n, n), 0) == lax.broadcasted_iota(jnp.int32, (n, n), 1)).astype(F32)
    c_i = lax.broadcasted_iota(jnp.int32, (LANES, LANES), 0)
    c_j = lax.broadcasted_iota(jnp.int32, (LANES, LANES), 1)
    same_head = c_i // HEAD_DIM == c_j // HEAD_DIM
    diagonal = c_i == c_j
    zeros = jnp.zeros((n, LANES), F32)

    pair_ids = [(it, j) for it in range(items) for j in range(pairs)]
    head_ids = [(pi, h) for pi in range(len(pair_ids)) for h in range(HEADS_PER_VREG)]
    heads_of = lambda pi: [hi for hi, (p_i, _) in enumerate(head_ids) if p_i == pi]
    cut = lambda x, pi: x[pair_ids[pi][0] * n:(pair_ids[pi][0] + 1) * n,
                          pair_ids[pi][1] * LANES:(pair_ids[pi][1] + 1) * LANES]
    rb = [cut(r_bar, pi) for pi in range(len(pair_ids))]
    be = [cut(b_end, pi) for pi in range(len(pair_ids))]
    vv = [cut(v, pi) for pi in range(len(pair_ids))]
    bk = [_t(jnp.concatenate([cut(b_til, pi), cut(k_til, pi)], axis=0)) for pi in range(len(pair_ids))]
    mine = [lane // HEAD_DIM == h for _, h in head_ids]
    abm = [jnp.where(mine[hi], cut(a_bar, pi), 0.0) for hi, (pi, _) in enumerate(head_ids)]
    v0 = [_t(jnp.concatenate([zeros, jnp.where(mine[hi], vv[pi], 0.0)], axis=0)) for hi, (pi, _) in enumerate(head_ids)]
    m = [_mm(_t(jnp.concatenate([abm[hi], jnp.where(mine[hi], rb[pi], 0.0)], axis=0)), bk[pi], "nt")
         for hi, (pi, _) in enumerate(head_ids)]
    top = [jnp.where(strictly_lower, x[:n], 0.0) for x in m]
    bot = [jnp.where(lower, x[n:], 0.0) for x in m]
    y = [_mm(_t(top[hi]), v0[hi]) for hi in range(len(head_ids))]
    power = [x[:, :n] for x in top]
    inv = [eye + x for x in power]
    for _ in range(int(math.log2(n)) - 1):
        pt = [_t(x) for x in power]
        power = [_mm(x, x) for x in pt]
        inv = [x + _mm(_t(x), _t(pw)) for x, pw in zip(inv, power)]
    wu = [_mm(_t(inv[hi]), _t(jnp.concatenate([abm[hi], y[hi]], axis=1))) for hi in range(len(head_ids))]
    v_low = [tuple(jnp.concatenate([jnp.zeros((n, LANES), BF16), t[n:]], axis=1) for t in v0[hi])
             for hi in range(len(head_ids))]
    ro = [_mm(_t(bot[hi]), tuple(jnp.concatenate([a, b2], axis=0) for a, b2 in zip(_t(wu[hi]), v_low[hi])))
          for hi in range(len(head_ids))]
    both = lambda xs, pi, sl: sum(xs[hi][:, sl] for hi in heads_of(pi))
    first, second = slice(0, LANES), slice(LANES, 2 * LANES)
    g_t = [jnp.where(same_head, _mm(_t(be[pi]), _t(both(wu, pi, first)), "tn"), 0.0)
           + jnp.where(diagonal, cut(g_end, pi)[:1], 0.0) for pi in range(len(pair_ids))]
    s2_t = [_mm(_t(jnp.concatenate([be[pi], cut(k_end, pi)], axis=0)),
                _t(jnp.concatenate([both(wu, pi, second), vv[pi]], axis=0)), "tn") for pi in range(len(pair_ids))]
    r2 = [_t(rb[pi] + both(ro, pi, first)) for pi in range(len(pair_ids))]
    out = [None] * len(pair_ids)
    per_step = items // nb
    for b in range(nb):
        for j in range(pairs):
            state = state_ref[b * pairs + j]
            for c in range(per_step):
                pi = (b * per_step + c) * pairs + j
                out[pi] = _mm(r2[pi], _t(state)) + both(ro, pi, second)
                state = _mm(_t(g_t[pi]), _t(state)) + jnp.where(same_head, s2_t[pi], 0.0)
            state_ref[b * pairs + j] = state
    o = jnp.concatenate([jnp.concatenate(out[it * pairs:(it + 1) * pairs], axis=1) for it in range(items)], axis=0)

    inv_n = 1.0 / HEAD_DIM
    cen = o - _head_sums(o) * inv_n
    var = _head_sums(cen * cen) * inv_n
    o = cen * lax.rsqrt(var + RWKV_GN_EPS) * lnw_ref[...] + lnb_ref[...]
    o = o + _head_sums(r * k * rk_ref[...]) * v
    o_ref[...] = (o * gate).astype(o_ref.dtype).reshape(nb, per_seq, RWKV_WIDTH)


def _rwkv_mix(p, mu, w0, w_up, a0, a_up, g_up, k_k, k_a, r_k, ln_w, ln_b):
    b, s, _ = p.shape
    per_step = RWKV_CHUNK * RWKV_STEP_CHUNKS
    vec = lambda t: t.astype(F32).reshape(1, -1)
    split = lambda t: jnp.stack(_terms(t.astype(F32), RWKV_TERMS))
    consts = [vec(mu), vec(w0), split(w_up), vec(a0), split(a_up), split(g_up), vec(k_k), vec(k_a), vec(r_k),
              vec(ln_w), vec(ln_b)]
    return pl.pallas_call(
        _rwkv_kernel,
        name="rwkv7",
        grid=(s // per_step,),
        in_specs=[pl.BlockSpec((b, per_step, RWKV_IN), lambda c: (0, c, 0))]
        + [_const_spec(t.shape) for t in consts],
        out_specs=pl.BlockSpec((b, per_step, RWKV_WIDTH), lambda c: (0, c, 0)),
        out_shape=jax.ShapeDtypeStruct((b, s, RWKV_WIDTH), BF16),
        scratch_shapes=[pltpu.VMEM((b, RWKV_IN), F32),
                        pltpu.VMEM((b * RWKV_WIDTH // LANES, LANES, LANES), F32)],
        compiler_params=_params("arbitrary"),
    )(p, *consts)


def _s5_prep_kernel(are_ref, aim_ref, lstep_ref, bre_ref, bim_ref, bmat_ref, pw_ref, step_ref):
    lam_re = jnp.minimum(are_ref[...], -1e-4)
    lam_im = aim_ref[...]
    step = jnp.exp(lstep_ref[...])
    z_re, z_im = lam_re * step, lam_im * step
    mag = jnp.exp(z_re)
    num_re, num_im = mag * jnp.cos(z_im) - 1.0, mag * jnp.sin(z_im)
    den = lam_re * lam_re + lam_im * lam_im
    coef_re = (num_re * lam_re + num_im * lam_im) / den
    coef_im = (num_im * lam_re - num_re * lam_im) / den
    bre, bim = bre_ref[...], bim_ref[...]
    bmat_ref[:, :S5_LANES] = (coef_re * bre - coef_im * bim).astype(bmat_ref.dtype)
    bmat_ref[:, S5_LANES:] = (coef_re * bim + coef_im * bre).astype(bmat_ref.dtype)
    steps = (lax.broadcasted_iota(jnp.int32, (SUBLANES, 1), 0) + 1).astype(F32)
    pmag = jnp.exp(steps * z_re)
    pw_ref[:, :S5_LANES] = pmag * jnp.cos(steps * z_im)
    pw_ref[:, S5_LANES:] = pmag * jnp.sin(steps * z_im)
    sub = lax.broadcasted_iota(jnp.int32, (SUBLANES, 1), 0)
    for j in range(step_ref.shape[0]):
        dist = 2 ** j
        mag_j = jnp.where(sub >= dist, jnp.exp(dist * z_re), 0.0)
        step_ref[j, :, :S5_LANES] = mag_j * jnp.cos(dist * z_im)
        step_ref[j, :, S5_LANES:] = mag_j * jnp.sin(dist * z_im)


def _s5_kernel(u_ref, bmat_ref, pw_ref, step_ref, cre_ref, cim_ref, d_ref, wglu_ref, bglu_ref, gain_ref, o_ref, carry_ref):
    n = S5_CHUNK

    @pl.when(pl.program_id(1) == 0)
    def _():
        carry_ref[...] = jnp.zeros_like(carry_ref)

    u = u_ref[0]
    bu = jnp.dot(u.astype(BF16), bmat_ref[...], preferred_element_type=F32)
    x_re, x_im = bu[:, :S5_LANES], bu[:, S5_LANES:]
    groups = n // SUBLANES
    for j in range(step_ref.shape[0]):
        l_re = step_ref[j, :, :S5_LANES][None]
        l_im = step_ref[j, :, S5_LANES:][None]
        s_re = pltpu.roll(x_re, 2 ** j, axis=0).reshape(groups, SUBLANES, S5_LANES)
        s_im = pltpu.roll(x_im, 2 ** j, axis=0).reshape(groups, SUBLANES, S5_LANES)
        x_re = x_re + (l_re * s_re - l_im * s_im).reshape(n, S5_LANES)
        x_im = x_im + (l_re * s_im + l_im * s_re).reshape(n, S5_LANES)
    c_re, c_im = carry_ref[:, :S5_LANES], carry_ref[:, S5_LANES:]
    p_re, p_im = pw_ref[:, :S5_LANES], pw_ref[:, S5_LANES:]
    groups_re, groups_im = [], []
    for g in range(n // SUBLANES):
        g_re, g_im = x_re[g * SUBLANES:(g + 1) * SUBLANES], x_im[g * SUBLANES:(g + 1) * SUBLANES]
        g_re, g_im = g_re + p_re * c_re - p_im * c_im, g_im + p_re * c_im + p_im * c_re
        c_re, c_im = g_re[SUBLANES - 1:], g_im[SUBLANES - 1:]
        groups_re.append(g_re)
        groups_im.append(g_im)
    carry_ref[:, :S5_LANES] = c_re
    carry_ref[:, S5_LANES:] = c_im
    x_re, x_im = jnp.concatenate(groups_re, axis=0), jnp.concatenate(groups_im, axis=0)

    y = (jnp.dot(x_re.astype(BF16), cre_ref[...], preferred_element_type=F32)
         - jnp.dot(x_im.astype(BF16), cim_ref[...], preferred_element_type=F32)) + d_ref[...] * u
    y = jax.nn.gelu(y)
    y = y * jax.nn.sigmoid(jnp.dot(y.astype(BF16), wglu_ref[...], preferred_element_type=F32) + bglu_ref[...])
    o_ref[0] = _rms(y, gain_ref[...]).astype(o_ref.dtype)


def _block_diag(t):
    g, rows, cols = t.shape
    return jnp.einsum('grc,gh->grhc', t, jnp.eye(g, dtype=t.dtype)).reshape(g * rows, g * cols)


def _s5_mix(u, a_re, a_im, log_step, b_re, b_im, c_re, c_im, d, w_glu, b_glu, gain):
    b, s, width = u.shape
    lane = lambda t: t.astype(F32).reshape(1, S5_LANES)
    to_cp = lambda t: _block_diag(jnp.swapaxes(t.astype(F32), 1, 2))
    bmat, pw, steps = pl.pallas_call(
        _s5_prep_kernel,
        name="s5_discretise",
        out_shape=[jax.ShapeDtypeStruct((width, 2 * S5_LANES), BF16),
                   jax.ShapeDtypeStruct((SUBLANES, 2 * S5_LANES), F32),
                   jax.ShapeDtypeStruct((int(math.log2(SUBLANES)), SUBLANES, 2 * S5_LANES), F32)],
        compiler_params=pltpu.CompilerParams(vmem_limit_bytes=VMEM_LIMIT),
    )(lane(a_re), lane(a_im), lane(jnp.repeat(log_step[:, None], S5_STATE, axis=1)), to_cp(b_re), to_cp(b_im))
    cre = _block_diag(jnp.swapaxes(c_re, 1, 2)).astype(BF16)
    cim = _block_diag(jnp.swapaxes(c_im, 1, 2)).astype(BF16)
    vec = lambda t: t.astype(F32).reshape(1, width)
    consts = [bmat, pw, steps, cre, cim, vec(d), w_glu.astype(BF16), vec(b_glu), vec(gain)]
    return pl.pallas_call(
        _s5_kernel,
        name="s5_scan",
        grid=(b, s // S5_CHUNK),
        in_specs=[pl.BlockSpec((1, S5_CHUNK, width), lambda i, c: (i, c, 0))]
        + [_const_spec(t.shape) for t in consts],
        out_specs=pl.BlockSpec((1, S5_CHUNK, width), lambda i, c: (i, c, 0)),
        out_shape=jax.ShapeDtypeStruct((b, s, width), BF16),
        scratch_shapes=[pltpu.VMEM((1, 2 * S5_LANES), F32)],
        compiler_params=_params("parallel", "arbitrary"),
    )(u, *consts)


def _attn_kernel(q_ref, *refs):
    blocks = ATTN_KEY_BLOCKS + ATTN_STEP_TILES - 1
    k_refs = refs[:blocks]
    v_refs = refs[blocks:2 * blocks]
    bias_ref, gain_ref, o_ref = refs[2 * blocks:]
    n_keys = ATTN_KEY_BLOCKS * ATTN_TILE
    q_all = q_ref[0] * (HEAD_DIM ** -0.5)
    k_all = jnp.concatenate([kr[0] for kr in k_refs], axis=0)
    v_all = jnp.concatenate([vr[0] for vr in v_refs], axis=0)
    neg = jnp.finfo(F32).min
    in_band = bias_ref[0] != neg
    lane = lax.broadcasted_iota(jnp.int32, (1, n_keys), 1)
    heads = [slice(h * HEAD_DIM, (h + 1) * HEAD_DIM) for h in range(ATTN_HEADS)]
    work = []
    for i in range(ATTN_STEP_TILES):
        rows = slice(i * ATTN_TILE, (i + 1) * ATTN_TILE)
        keys = slice(i * ATTN_TILE, i * ATTN_TILE + n_keys)
        first_key = (pl.program_id(1) * ATTN_STEP_TILES + i - (ATTN_KEY_BLOCKS - 1)) * ATTN_TILE
        allowed = in_band & (first_key + lane >= 0)
        work += [(q_all[rows, sl], k_all[keys, sl], v_all[keys, sl], allowed, h) for h, sl in enumerate(heads)]
    s = [lax.dot_general(q, k, _CONTRACT["nt"], preferred_element_type=F32) for q, k, _, _, _ in work]
    s = [jnp.where(allowed, x + bias_ref[h], neg) for x, (_, _, _, allowed, h) in zip(s, work)]
    e = [jnp.exp(x - jnp.max(x, axis=-1, keepdims=True)) for x in s]
    pv = [jnp.dot(x.astype(BF16), v, preferred_element_type=F32) for x, (_, _, v, _, _) in zip(e, work)]
    outs = [x / jnp.sum(y, axis=-1, keepdims=True) for x, y in zip(pv, e)]
    tiles = [jnp.concatenate(outs[i * ATTN_HEADS:(i + 1) * ATTN_HEADS], axis=1) for i in range(ATTN_STEP_TILES)]
    o_ref[0] = _rms(jnp.concatenate(tiles, axis=0), gain_ref[...]).astype(o_ref.dtype)


def _attn_bias_table(rel_bias):
    n_keys = ATTN_KEY_BLOCKS * ATTN_TILE
    left = ATTN_LEFT * ATTN_CHUNK
    period = n_keys + ATTN_TILE
    rb = rel_bias.astype(F32)
    heads = rb.shape[0]
    far = lambda width: jnp.broadcast_to(rb[:, 2 * MAX_REL:], (heads, width))
    g = jnp.concatenate([far(left - MAX_REL), rb[:, 1:][:, ::-1], far(period - left - MAX_REL)], axis=1)
    toeplitz = jnp.tile(g, (1, ATTN_TILE))[:, :ATTN_TILE * (period - 1)].reshape(heads, ATTN_TILE, period - 1)
    q = np.arange(ATTN_TILE)[:, None]
    j = np.arange(n_keys)[None, :]
    in_band = (j // ATTN_CHUNK >= q // ATTN_CHUNK) & (j // ATTN_CHUNK <= q // ATTN_CHUNK + ATTN_LEFT)
    return jnp.where(in_band[None], toeplitz[:, :, :n_keys], jnp.finfo(F32).min)


def _chunk_attention(qkv, rel_bias, gain):
    b, s, _ = qkv.shape
    bias = _attn_bias_table(rel_bias)
    back = ATTN_KEY_BLOCKS - 1
    blocks = ATTN_KEY_BLOCKS + ATTN_STEP_TILES - 1
    step_rows = ATTN_STEP_TILES * ATTN_TILE
    key_specs = lambda col: [pl.BlockSpec((1, ATTN_TILE, ATTN_WIDTH),
                                          lambda i, c, j=j: (i, jnp.maximum(c * ATTN_STEP_TILES + j - back, 0), col))
                             for j in range(blocks)]
    return pl.pallas_call(
        _attn_kernel,
        name="chunk_attention",
        grid=(b, s // step_rows),
        in_specs=[pl.BlockSpec((1, step_rows, ATTN_WIDTH), lambda i, c: (i, c, 0))] + key_specs(1) + key_specs(2)
        + [_const_spec(bias.shape), _const_spec((1, ATTN_WIDTH))],
        out_specs=pl.BlockSpec((1, step_rows, ATTN_WIDTH), lambda i, c: (i, c, 0)),
        out_shape=jax.ShapeDtypeStruct((b, s, ATTN_WIDTH), BF16),
        compiler_params=_params("parallel", "parallel"),
    )(*([qkv] * (1 + 2 * blocks)), bias, gain.astype(F32).reshape(1, ATTN_WIDTH))


def _outproj_kernel(route, x_ref, orw_ref, os5_ref, oat_ref, w_ref, gpost_ref, gpre_ref, *refs):
    mixed = jnp.concatenate([orw_ref[...], os5_ref[...], oat_ref[...]], axis=1)
    y = jnp.dot(mixed, w_ref[...], preferred_element_type=F32)
    x = x_ref[...] + _rms(y, gpost_ref[...])
    h = _rms(x, gpre_ref[...])
    if not route:
        x_out, h_out = refs
    else:
        router_ref, x_out, h_out, route_out = refs
        logits = _mm(_terms(h, 2), _terms(router_ref[...], 2))
        lane = lax.broadcasted_iota(jnp.int32, logits.shape, 1).astype(F32)
        neg = jnp.finfo(F32).min
        logits = jnp.where(lane < N_EXPERTS, logits, neg)
        top1 = jnp.max(logits, axis=-1, keepdims=True)
        idx1 = jnp.min(jnp.where(logits == top1, lane, float(LANES)), axis=-1, keepdims=True)
        rest = jnp.where(lane == idx1, neg, logits)
        top2 = jnp.max(rest, axis=-1, keepdims=True)
        idx2 = jnp.min(jnp.where(rest == top2, lane, float(LANES)), axis=-1, keepdims=True)
        e2 = jnp.exp(top2 - top1)
        route_out[...] = (jnp.where(lane == 0, idx1, 0.0) + jnp.where(lane == 1, idx2, 0.0)
                          + jnp.where(lane == 2, 1.0 / (1.0 + e2), 0.0) + jnp.where(lane == 3, e2 / (1.0 + e2), 0.0))
    x_out[...] = x
    h_out[...] = h.astype(h_out.dtype)


def _out_projection(x, o_rwkv, o_s5, o_attn, w_out, gain_post, gain_pre, router):
    t, d = x.shape
    row = lambda n: pl.BlockSpec((ROW_TILE, n), lambda i: (i, 0))
    vec = lambda g: g.astype(F32).reshape(1, d)
    ins = [x, o_rwkv, o_s5, o_attn, w_out.astype(BF16), vec(gain_post), vec(gain_pre)]
    in_specs = [row(d), row(o_rwkv.shape[1]), row(o_s5.shape[1]), row(o_attn.shape[1]),
                _const_spec(w_out.shape), _const_spec((1, d)), _const_spec((1, d))]
    out_specs = [row(d), row(d)]
    out_shape = [jax.ShapeDtypeStruct((t, d), F32), jax.ShapeDtypeStruct((t, d), BF16 if router is None else F32)]
    if router is not None:
        ins.append(jnp.pad(router.astype(F32), ((0, 0), (0, LANES - N_EXPERTS))))
        in_specs.append(_const_spec((d, LANES)))
        out_specs.append(row(LANES))
        out_shape.append(jax.ShapeDtypeStruct((t, LANES), F32))
    return pl.pallas_call(
        functools.partial(_outproj_kernel, router is not None),
        name="out_projection",
        grid=(t // ROW_TILE,),
        in_specs=in_specs, out_specs=out_specs, out_shape=out_shape,
        compiler_params=_params("parallel"),
    )(*ins)


def _ffn_kernel(x_ref, h_ref, wg_ref, wu_ref, wd_ref, gain_ref, o_ref):
    h = h_ref[...]
    y = None
    for f in range(wg_ref.shape[1] // FFN_TILE):
        cols = slice(f * FFN_TILE, (f + 1) * FFN_TILE)
        act = (jax.nn.silu(jnp.dot(h, wg_ref[:, cols], preferred_element_type=F32))
               * jnp.dot(h, wu_ref[:, cols], preferred_element_type=F32))
        part = jnp.dot(act.astype(BF16), wd_ref[cols, :], preferred_element_type=F32)
        y = part if y is None else y + part
    o_ref[...] = x_ref[...] + _rms(y, gain_ref[...])


def _ffn(x, h, w_gate, w_up, w_down, gain):
    t, d = x.shape
    d_ff = w_gate.shape[1]
    row = lambda n: pl.BlockSpec((ROW_TILE, n), lambda i: (i, 0))
    resident = lambda shape: pl.BlockSpec(shape, lambda i: (0, 0), pipeline_mode=pl.Buffered(1))
    return pl.pallas_call(
        _ffn_kernel,
        name="swiglu_ffn",
        grid=(t // ROW_TILE,),
        in_specs=[row(d), row(d), resident((d, d_ff)), resident((d, d_ff)), resident((d_ff, d)), _const_spec((1, d))],
        out_specs=row(d),
        out_shape=jax.ShapeDtypeStruct((t, d), F32),
        compiler_params=_params("parallel"),
    )(x, h, w_gate.astype(BF16), w_up.astype(BF16), w_down.astype(BF16), gain.astype(F32).reshape(1, d))


MOE_TILE = 512
ROW_GATHER_PRIORITY = 1


def _moe_plan(route, n_experts):
    t = route.shape[0]
    n = 2 * t
    tm = MOE_TILE
    expert = route[:, :2].astype(jnp.int32).reshape(n)
    onehot = (expert[:, None] == jnp.arange(n_experts, dtype=jnp.int32)[None, :]).astype(jnp.int32)
    rank = jnp.sum(jnp.cumsum(onehot, axis=0) * onehot, axis=1) - 1
    counts = jnp.sum(onehot, axis=0)
    padded = (counts + tm - 1) // tm * tm
    ends = jnp.cumsum(padded)
    slot = jnp.sum((ends - padded)[None, :] * onehot, axis=1) + rank
    n_tiles = n // tm + n_experts
    slots = n_tiles * tm
    ids = jnp.arange(n, dtype=jnp.int32)
    every = jnp.arange(slots, dtype=jnp.int32)
    dump = n + every % tm + tm * ((every // tm) % 2)
    dst = dump.at[slot].set((ids % 2) * t + ids // 2, unique_indices=True)
    tok = jnp.where(dst < n, dst % t, 0)
    tile_start = jnp.arange(n_tiles, dtype=jnp.int32) * tm
    tile_expert = jnp.minimum(jnp.sum((tile_start[:, None] >= ends[None, :]).astype(jnp.int32), axis=1), n_experts - 1)
    tile_valid = (tile_start < ends[-1]).astype(jnp.int32)
    dst = dst.reshape(n_tiles, 1, tm)
    before_first = (n + tm + jnp.arange(tm, dtype=jnp.int32)).reshape(1, 1, tm)
    return tok.reshape(n_tiles, 1, tm), dst, jnp.concatenate([before_first, dst[:-1]], axis=0), tile_expert, tile_valid


def _moe_ffn_kernel(te_ref, tv_ref, tok0_ref, tokn_ref, dstp_ref, dst_ref, h_hbm, wg_ref, wu_ref, wd_ref,
                    out_hbm, xbuf, xb, ybuf, acc_ref, gsem, ssem):
    t, f = pl.program_id(0), pl.program_id(1)
    n_t, n_f = pl.num_programs(0), pl.num_programs(1)
    tm = MOE_TILE
    slot = t % 2
    valid = tv_ref[t] == 1
    prev_valid = tv_ref[jnp.maximum(t - 1, 0)] == 1
    last = t == n_t - 1

    gather_row = lambda tok_ref, s, i: pltpu.make_async_copy(
        h_hbm.at[pl.ds(tok_ref[0, 0, i], 1), :], xbuf.at[s, pl.ds(i, 1), :], gsem.at[s])
    scatter_row = lambda d_ref, s, i: pltpu.make_async_copy(
        ybuf.at[s, pl.ds(i, 1), :], out_hbm.at[pl.ds(d_ref[0, 0, i], 1), :], ssem.at[s])
    wait_gather = lambda s: pltpu.make_async_copy(h_hbm.at[pl.ds(0, tm), :], xbuf.at[s], gsem.at[s]).wait()
    wait_scatter = lambda s: pltpu.make_async_copy(ybuf.at[s], out_hbm.at[pl.ds(0, tm), :], ssem.at[s]).wait()

    def looped(copy, idx_ref, s):
        def body(i, carry):
            copy(idx_ref, s, i).start()
            return carry
        lax.fori_loop(0, tm, body, 0, unroll=8)

    def compute():
        x = xb[...]
        act = jax.nn.silu(jnp.dot(x, wg_ref[0], preferred_element_type=F32)) * jnp.dot(x, wu_ref[0], preferred_element_type=F32)
        return jnp.dot(act.astype(BF16), wd_ref[0], preferred_element_type=F32)

    @pl.when((f == 0) & (t == 0))
    def _():
        looped(gather_row, tok0_ref, 0)
        ybuf[1] = jnp.zeros(ybuf.shape[1:], F32)
        for bank in range(2):
            fill = pltpu.make_async_copy(ybuf.at[1], out_hbm.at[pl.ds(out_hbm.shape[0] - (bank + 1) * tm, tm), :],
                                         ssem.at[1])
            fill.start()
            fill.wait()

    @pl.when((f == 0) & ((t == 0) | prev_valid))
    def _():
        wait_gather(slot)

    @pl.when((f == 0) & valid)
    def _():
        xb[...] = xbuf[slot].astype(BF16)
        for i in range(tm):
            gather_row(tokn_ref, 1 - slot, i).start(priority=ROW_GATHER_PRIORITY)
        acc_ref[...] = compute()

    @pl.when((f == n_f - 1) & ((t == 1) | ((t >= 2) & (tv_ref[jnp.maximum(t - 2, 0)] == 1))))
    def _():
        wait_scatter(slot)

    @pl.when((f == n_f - 1) & valid)
    def _():
        for i in range(tm):
            scatter_row(dstp_ref, 1 - slot, i).start(priority=i % 2)
        ybuf[slot] = acc_ref[...] + compute()

    @pl.when((f == n_f - 1) & jnp.logical_not(valid) & (t >= 1) & prev_valid)
    def _():
        looped(scatter_row, dstp_ref, 1 - slot)

    @pl.when((f == n_f - 1) & last)
    def _():
        @pl.when(valid)
        def _():
            looped(scatter_row, dst_ref, slot)
            wait_scatter(slot)
            wait_gather(1 - slot)

        @pl.when(prev_valid)
        def _():
            wait_scatter(1 - slot)


def _moe_combine_kernel(x_ref, y1_ref, y2_ref, route_ref, gain_ref, o_ref):
    y = route_ref[:, 2:3] * y1_ref[...] + route_ref[:, 3:4] * y2_ref[...]
    o_ref[...] = x_ref[...] + _rms(y, gain_ref[...])


def _moe(x, h, route, w_gate, w_up, w_down, gain):
    t, d = x.shape
    n_e, _, d_ff = w_gate.shape
    tm = MOE_TILE
    assert d_ff == 2 * FFN_TILE
    tok, dst, dst_prev, tile_expert, tile_valid = _moe_plan(route, n_e)
    n_tiles = tok.shape[0]
    idx_spec = lambda shift: pl.BlockSpec(
        (1, 1, tm), lambda i, f, te, tv: (jnp.clip(i + shift, 0, n_tiles - 1), 0, 0), memory_space=pltpu.SMEM)
    grid_spec = pltpu.PrefetchScalarGridSpec(
        num_scalar_prefetch=2,
        grid=(n_tiles, d_ff // FFN_TILE),
        in_specs=[idx_spec(0), idx_spec(1), idx_spec(0), idx_spec(0),
                  pl.BlockSpec(memory_space=pl.ANY),
                  pl.BlockSpec((1, d, FFN_TILE), lambda i, f, te, tv: (te[i], 0, f)),
                  pl.BlockSpec((1, d, FFN_TILE), lambda i, f, te, tv: (te[i], 0, f)),
                  pl.BlockSpec((1, FFN_TILE, d), lambda i, f, te, tv: (te[i], f, 0))],
        out_specs=pl.BlockSpec(memory_space=pl.ANY),
        scratch_shapes=[pltpu.VMEM((2, tm, d), F32), pltpu.VMEM((tm, d), BF16), pltpu.VMEM((2, tm, d), F32),
                        pltpu.VMEM((tm, d), F32), pltpu.SemaphoreType.DMA((2,)), pltpu.SemaphoreType.DMA((2,))],
    )
    routed = pl.pallas_call(
        _moe_ffn_kernel,
        name="moe_experts",
        grid_spec=grid_spec,
        out_shape=jax.ShapeDtypeStruct((2 * t + 2 * tm, d), F32),
        compiler_params=_params("arbitrary", "arbitrary"),
    )(tile_expert, tile_valid, tok, tok, dst_prev, dst, h, w_gate.astype(BF16), w_up.astype(BF16), w_down.astype(BF16))
    row = lambda n: pl.BlockSpec((ROW_TILE, n), lambda i: (i, 0))
    second = pl.BlockSpec((ROW_TILE, d), lambda i: (i + t // ROW_TILE, 0))
    return pl.pallas_call(
        _moe_combine_kernel,
        name="moe_combine",
        grid=(t // ROW_TILE,),
        in_specs=[row(d), row(d), second, row(LANES), _const_spec((1, d))],
        out_specs=row(d),
        out_shape=jax.ShapeDtypeStruct((t, d), F32),
        compiler_params=_params("parallel"),
    )(x, routed, routed, route, gain.astype(F32).reshape(1, d))


def kernel(x, norm_mix_pre, norm_mix_post, norm_ffn_pre, norm_ffn_post, w_in, w_out, rwkv_mu, rwkv_w0, rwkv_w_up, rwkv_a0, rwkv_a_up, rwkv_g_up, rwkv_k_k, rwkv_k_a, rwkv_r_k, rwkv_ln_w, rwkv_ln_b, s5_a_re, s5_a_im, s5_log_step, s5_b_re, s5_b_im, s5_c_re, s5_c_im, s5_d, s5_w_glu, s5_b_glu, s5_norm, attn_rel_bias, attn_norm, ffn_w_gate, ffn_w_up, ffn_w_down, moe_router, moe_w_gate, moe_w_up, moe_w_down):
    b, s, d = x.shape
    depth = w_in.shape[0]
    xt = x.reshape(b * s, d)
    for layer in range(depth):
        p, u, qkv = _in_projection(xt, norm_mix_pre[layer], w_in[layer])
        o_rwkv = _rwkv_mix(p.reshape(b, s, -1), rwkv_mu[layer], rwkv_w0[layer], rwkv_w_up[layer], rwkv_a0[layer],
                           rwkv_a_up[layer], rwkv_g_up[layer], rwkv_k_k[layer], rwkv_k_a[layer], rwkv_r_k[layer],
                           rwkv_ln_w[layer], rwkv_ln_b[layer])
        o_s5 = _s5_mix(u.reshape(b, s, -1), s5_a_re[layer], s5_a_im[layer], s5_log_step[layer], s5_b_re[layer],
                       s5_b_im[layer], s5_c_re[layer], s5_c_im[layer], s5_d[layer], s5_w_glu[layer],
                       s5_b_glu[layer], s5_norm[layer])
        o_attn = _chunk_attention(qkv.reshape(b, s, -1), attn_rel_bias[layer], attn_norm[layer])
        i = layer // 2
        moe = layer % 2 == 1
        outs = _out_projection(xt, o_rwkv.reshape(b * s, -1), o_s5.reshape(b * s, -1), o_attn.reshape(b * s, -1),
                               w_out[layer], norm_mix_post[layer], norm_ffn_pre[layer],
                               moe_router[i] if moe else None)
        if moe:
            xt, h, route = outs
            xt = _moe(xt, h, route, moe_w_gate[i], moe_w_up[i], moe_w_down[i], norm_ffn_post[layer])
        else:
            xt, h = outs
            xt = _ffn(xt, h, ffn_w_gate[i], ffn_w_up[i], ffn_w_down[i], norm_ffn_post[layer])
    return xt.reshape(b, s, d)
```

```python
import functools
import math

import jax
import jax.numpy as jnp
import numpy as np
from jax import lax
from jax.experimental import pallas as pl
from jax.experimental.pallas import tpu as pltpu

F32 = jnp.float32
BF16 = jnp.bfloat16

RMS_EPS = 1e-6
RWKV_GN_EPS = 64e-5
HEAD_DIM = 64
RWKV_HEADS = 6
RWKV_WIDTH = 384
RWKV_IN = 1408
S5_WIDTH = 256
S5_GROUPS = 16
S5_STATE = 64
S5_LANES = S5_GROUPS * S5_STATE
ATTN_HEADS = 6
ATTN_WIDTH = 384
ATTN_CHUNK = 64
ATTN_LEFT = 8
MAX_REL = 128
N_EXPERTS = 8
LANES = 128

RWKV_CHUNK = 64
RWKV_STEP_CHUNKS = 2
S5_CHUNK = 256
SUBLANES = 8
ATTN_TILE = 2 * ATTN_CHUNK
ATTN_KEY_BLOCKS = (ATTN_LEFT * ATTN_CHUNK) // ATTN_TILE + 1
ATTN_STEP_TILES = 2
ROW_TILE = 512
FFN_TILE = 1408
VMEM_LIMIT = 48 * 1024 * 1024

_HI = lax.Precision.HIGHEST
_CONTRACT = {"nn": (((1,), (0,)), ((), ())), "nt": (((1,), (1,)), ((), ())), "tn": (((0,), (0,)), ((), ()))}


def _hdot(a, b):
    return jnp.dot(a, b, precision=_HI, preferred_element_type=F32)


def _terms(x, n):
    out = []
    for i in range(n):
        t = x.astype(BF16)
        out.append(t)
        if i + 1 < n:
            x = x - t.astype(F32)
    return tuple(out)


def _mm(a, b, kind="nn"):
    order = max(len(a), len(b))
    acc = None
    for i, ai in enumerate(a):
        for j, bj in enumerate(b):
            if i + j < order:
                d = lax.dot_general(ai, bj, _CONTRACT[kind], preferred_element_type=F32)
                acc = d if acc is None else acc + d
    return acc


def _rms(x, gain):
    return x * lax.rsqrt(jnp.mean(x * x, axis=-1, keepdims=True) + RMS_EPS) * gain


def _params(*semantics):
    return pltpu.CompilerParams(dimension_semantics=semantics, vmem_limit_bytes=VMEM_LIMIT)


def _const_spec(shape):
    zeros = (0,) * len(shape)
    return pl.BlockSpec(shape, lambda *_: zeros)


def _inproj_kernel(x_ref, gain_ref, w_ref, p_ref, u_ref, qkv_ref):
    h = _rms(x_ref[...], gain_ref[...]).astype(BF16)
    s5_at, qkv_at = RWKV_IN, RWKV_IN + S5_WIDTH
    p_ref[...] = jnp.dot(h, w_ref[:, :s5_at], preferred_element_type=F32)
    u_ref[...] = jnp.dot(h, w_ref[:, s5_at:qkv_at], preferred_element_type=F32)
    qkv_ref[...] = jnp.dot(h, w_ref[:, qkv_at:], preferred_element_type=F32).astype(BF16)


def _in_projection(x, gain, w_in):
    t, d = x.shape
    n_in = w_in.shape[1]
    n_qkv = n_in - RWKV_IN - S5_WIDTH
    row = lambda n: pl.BlockSpec((ROW_TILE, n), lambda i: (i, 0))
    return pl.pallas_call(
        _inproj_kernel,
        name="in_projection",
        grid=(t // ROW_TILE,),
        in_specs=[row(d), _const_spec((1, d)), _const_spec((d, n_in))],
        out_specs=[row(RWKV_IN), row(S5_WIDTH), row(n_qkv)],
        out_shape=[jax.ShapeDtypeStruct((t, RWKV_IN), F32), jax.ShapeDtypeStruct((t, S5_WIDTH), F32),
                   jax.ShapeDtypeStruct((t, n_qkv), BF16)],
        compiler_params=_params("parallel"),
    )(x, gain.reshape(1, d), w_in.astype(BF16))


RWKV_TERMS = 1
HEADS_PER_VREG = LANES // HEAD_DIM


def _t(x):
    return _terms(x, RWKV_TERMS)


def _head_sums(x):
    first = lax.broadcasted_iota(jnp.int32, (1, LANES), 1) < HEAD_DIM
    outs = []
    for j in range(x.shape[1] // LANES):
        xp = x[:, j * LANES:(j + 1) * LANES]
        s0 = jnp.sum(jnp.where(first, xp, 0.0), axis=-1, keepdims=True)
        s1 = jnp.sum(jnp.where(first, 0.0, xp), axis=-1, keepdims=True)
        outs.append(jnp.where(first, s0, s1))
    return jnp.concatenate(outs, axis=1)


def _rwkv_kernel(p_ref, mu_ref, w0_ref, wup_ref, a0_ref, aup_ref, gup_ref, kk_ref, ka_ref, rk_ref,
                 lnw_ref, lnb_ref, o_ref, prev_ref, state_ref):
    nb, per_seq = p_ref.shape[0], p_ref.shape[1]
    n = RWKV_CHUNK
    rows = nb * per_seq
    items = rows // n
    w_at = 3 * RWKV_WIDTH
    pairs = RWKV_WIDTH // LANES

    @pl.when(pl.program_id(0) == 0)
    def _():
        prev_ref[...] = jnp.zeros_like(prev_ref)
        state_ref[...] = jnp.zeros_like(state_ref)

    p = p_ref[...].reshape(rows, RWKV_IN)
    row = lax.broadcasted_iota(jnp.int32, (rows, 1), 0)
    shifted = pltpu.roll(p, 1, axis=0)
    for b in range(nb):
        shifted = jnp.where(row == b * per_seq, prev_ref[b:b + 1, :], shifted)
    for b in range(nb):
        prev_ref[b:b + 1, :] = p[(b + 1) * per_seq - 1:(b + 1) * per_seq, :]
    p = p + (shifted - p) * mu_ref[...]

    r = p[:, :RWKV_WIDTH]
    k = p[:, RWKV_WIDTH:2 * RWKV_WIDTH]
    v = p[:, 2 * RWKV_WIDTH:w_at]
    w_lo = p[:, w_at:w_at + 64]
    a_lo = p[:, w_at + 64:w_at + 128]
    g_lo = p[:, w_at + 128:]

    lora = lambda x, w_ref: _mm(_t(x), tuple(w_ref[i] for i in range(RWKV_TERMS)))
    w = -jax.nn.softplus(-(w0_ref[...] + lora(jnp.tanh(w_lo), wup_ref))) - 0.5
    log_decay = -jnp.exp(w)
    a = jax.nn.sigmoid(a0_ref[...] + lora(a_lo, aup_ref))
    gate = lora(jax.nn.sigmoid(g_lo), gup_ref)

    kk = k * kk_ref[...]
    kk = kk * lax.rsqrt(jnp.maximum(_head_sums(kk * kk), 1e-24))
    k = k * (1.0 + (a - 1.0) * ka_ref[...])

    s_i = lax.broadcasted_iota(jnp.int32, (rows, rows), 0)
    s_j = lax.broadcasted_iota(jnp.int32, (rows, rows), 1)
    causal = ((s_i // n == s_j // n) & (s_i >= s_j)).astype(BF16)
    cum = _mm((causal,), _terms(log_decay, 3))
    cum_end = cum[n - 1:n, :]
    for it in range(1, items):
        cum_end = jnp.where(row >= it * n, cum[(it + 1) * n - 1:(it + 1) * n, :], cum_end)
    inv_g = jnp.exp(-cum)
    to_end = jnp.exp(cum_end - cum)
    beta = kk * a
    r_bar = r * jnp.exp(cum)
    a_bar = -kk * jnp.exp(cum - log_decay)
    b_til = beta * inv_g
    k_til = k * inv_g
    b_end = beta * to_end
    k_end = k * to_end
    g_end = jnp.exp(cum_end)

    lane = lax.broadcasted_iota(jnp.int32, (1, LANES), 1)
    q_i = lax.broadcasted_iota(jnp.int32, (n, LANES), 0)
    q_j = lax.broadcasted_iota(jnp.int32, (n, LANES), 1) % n
    lower = q_i >= q_j
    strictly_lower = q_i > q_j
    eye = (lax.broadcasted_iota(jnp.int32, (n, n), 0) == lax.broadcasted_iota(jnp.int32, (n, n), 1)).astype(F32)
    c_i = lax.broadcasted_iota(jnp.int32, (LANES, LANES), 0)
    c_j = lax.broadcasted_iota(jnp.int32, (LANES, LANES), 1)
    same_head = c_i // HEAD_DIM == c_j // HEAD_DIM
    diagonal = c_i == c_j
    zeros = jnp.zeros((n, LANES), F32)

    pair_ids = [(it, j) for it in range(items) for j in range(pairs)]
    head_ids = [(pi, h) for pi in range(len(pair_ids)) for h in range(HEADS_PER_VREG)]
    heads_of = lambda pi: [hi for hi, (p_i, _) in enumerate(head_ids) if p_i == pi]
    cut = lambda x, pi: x[pair_ids[pi][0] * n:(pair_ids[pi][0] + 1) * n,
                          pair_ids[pi][1] * LANES:(pair_ids[pi][1] + 1) * LANES]
    rb = [cut(r_bar, pi) for pi in range(len(pair_ids))]
    be = [cut(b_end, pi) for pi in range(len(pair_ids))]
    vv = [cut(v, pi) for pi in range(len(pair_ids))]
    bk = [_t(jnp.concatenate([cut(b_til, pi), cut(k_til, pi)], axis=0)) for pi in range(len(pair_ids))]
    mine = [lane // HEAD_DIM == h for _, h in head_ids]
    abm = [jnp.where(mine[hi], cut(a_bar, pi), 0.0) for hi, (pi, _) in enumerate(head_ids)]
    v0 = [_t(jnp.concatenate([zeros, jnp.where(mine[hi], vv[pi], 0.0)], axis=0)) for hi, (pi, _) in enumerate(head_ids)]
    m = [_mm(_t(jnp.concatenate([abm[hi], jnp.where(mine[hi], rb[pi], 0.0)], axis=0)), bk[pi], "nt")
         for hi, (pi, _) in enumerate(head_ids)]
    top = [jnp.where(strictly_lower, x[:n], 0.0) for x in m]
    bot = [jnp.where(lower, x[n:], 0.0) for x in m]
    y = [_mm(_t(top[hi]), v0[hi]) for hi in range(len(head_ids))]
    power = [x[:, :n] for x in top]
    inv = [eye + x for x in power]
    for _ in range(int(math.log2(n)) - 1):
        pt = [_t(x) for x in power]
        power = [_mm(x, x) for x in pt]
        inv = [x + _mm(_t(x), _t(pw)) for x, pw in zip(inv, power)]
    wu = [_mm(_t(inv[hi]), _t(jnp.concatenate([abm[hi], y[hi]], axis=1))) for hi in range(len(head_ids))]
    v_low = [tuple(jnp.concatenate([jnp.zeros((n, LANES), BF16), t[n:]], axis=1) for t in v0[hi])
             for hi in range(len(head_ids))]
    ro = [_mm(_t(bot[hi]), tuple(jnp.concatenate([a, b2], axis=0) for a, b2 in zip(_t(wu[hi]), v_low[hi])))
          for hi in range(len(head_ids))]
    both = lambda xs, pi, sl: sum(xs[hi][:, sl] for hi in heads_of(pi))
    first, second = slice(0, LANES), slice(LANES, 2 * LANES)
    g_t = [jnp.where(same_head, _mm(_t(be[pi]), _t(both(wu, pi, first)), "tn"), 0.0)
           + jnp.where(diagonal, cut(g_end, pi)[:1], 0.0) for pi in range(len(pair_ids))]
    s2_t = [_mm(_t(jnp.concatenate([be[pi], cut(k_end, pi)], axis=0)),
                _t(jnp.concatenate([both(wu, pi, second), vv[pi]], axis=0)), "tn") for pi in range(len(pair_ids))]
    r2 = [_t(rb[pi] + both(ro, pi, first)) for pi in range(len(pair_ids))]
    out = [None] * len(pair_ids)
    per_step = items // nb
    for b in range(nb):
        for j in range(pairs):
            state = state_ref[b * pairs + j]
            for c in range(per_step):
                pi = (b * per_step + c) * pairs + j
                out[pi] = _mm(r2[pi], _t(state)) + both(ro, pi, second)
                state = _mm(_t(g_t[pi]), _t(state)) + jnp.where(same_head, s2_t[pi], 0.0)
            state_ref[b * pairs + j] = state
    o = jnp.concatenate([jnp.concatenate(out[it * pairs:(it + 1) * pairs], axis=1) for it in range(items)], axis=0)

    inv_n = 1.0 / HEAD_DIM
    cen = o - _head_sums(o) * inv_n
    var = _head_sums(cen * cen) * inv_n
    o = cen * lax.rsqrt(var + RWKV_GN_EPS) * lnw_ref[...] + lnb_ref[...]
    o = o + _head_sums(r * k * rk_ref[...]) * v
    o_ref[...] = (o * gate).astype(o_ref.dtype).reshape(nb, per_seq, RWKV_WIDTH)


def _rwkv_mix(p, mu, w0, w_up, a0, a_up, g_up, k_k, k_a, r_k, ln_w, ln_b):
    b, s, _ = p.shape
    per_step = RWKV_CHUNK * RWKV_STEP_CHUNKS
    vec = lambda t: t.astype(F32).reshape(1, -1)
    split = lambda t: jnp.stack(_terms(t.astype(F32), RWKV_TERMS))
    consts = [vec(mu), vec(w0), split(w_up), vec(a0), split(a_up), split(g_up), vec(k_k), vec(k_a), vec(r_k),
              vec(ln_w), vec(ln_b)]
    return pl.pallas_call(
        _rwkv_kernel,
        name="rwkv7",
        grid=(s // per_step,),
        in_specs=[pl.BlockSpec((b, per_step, RWKV_IN), lambda c: (0, c, 0))]
        + [_const_spec(t.shape) for t in consts],
        out_specs=pl.BlockSpec((b, per_step, RWKV_WIDTH), lambda c: (0, c, 0)),
        out_shape=jax.ShapeDtypeStruct((b, s, RWKV_WIDTH), BF16),
        scratch_shapes=[pltpu.VMEM((b, RWKV_IN), F32),
                        pltpu.VMEM((b * RWKV_WIDTH // LANES, LANES, LANES), F32)],
        compiler_params=_params("arbitrary"),
    )(p, *consts)


def _s5_prep_kernel(are_ref, aim_ref, lstep_ref, bre_ref, bim_ref, bmat_ref, pw_ref, step_ref):
    lam_re = jnp.minimum(are_ref[...], -1e-4)
    lam_im = aim_ref[...]
    step = jnp.exp(lstep_ref[...])
    z_re, z_im = lam_re * step, lam_im * step
    mag = jnp.exp(z_re)
    num_re, num_im = mag * jnp.cos(z_im) - 1.0, mag * jnp.sin(z_im)
    den = lam_re * lam_re + lam_im * lam_im
    coef_re = (num_re * lam_re + num_im * lam_im) / den
    coef_im = (num_im * lam_re - num_re * lam_im) / den
    bre, bim = bre_ref[...], bim_ref[...]
    bmat_ref[:, :S5_LANES] = (coef_re * bre - coef_im * bim).astype(bmat_ref.dtype)
    bmat_ref[:, S5_LANES:] = (coef_re * bim + coef_im * bre).astype(bmat_ref.dtype)
    steps = (lax.broadcasted_iota(jnp.int32, (SUBLANES, 1), 0) + 1).astype(F32)
    pmag = jnp.exp(steps * z_re)
    pw_ref[:, :S5_LANES] = pmag * jnp.cos(steps * z_im)
    pw_ref[:, S5_LANES:] = pmag * jnp.sin(steps * z_im)
    sub = lax.broadcasted_iota(jnp.int32, (SUBLANES, 1), 0)
    for j in range(step_ref.shape[0]):
        dist = 2 ** j
        mag_j = jnp.where(sub >= dist, jnp.exp(dist * z_re), 0.0)
        step_ref[j, :, :S5_LANES] = mag_j * jnp.cos(dist * z_im)
        step_ref[j, :, S5_LANES:] = mag_j * jnp.sin(dist * z_im)


def _s5_kernel(u_ref, bmat_ref, pw_ref, step_ref, cre_ref, cim_ref, d_ref, wglu_ref, bglu_ref, gain_ref, o_ref, carry_ref):
    n = S5_CHUNK

    @pl.when(pl.program_id(1) == 0)
    def _():
        carry_ref[...] = jnp.zeros_like(carry_ref)

    u = u_ref[0]
    bu = jnp.dot(u.astype(BF16), bmat_ref[...], preferred_element_type=F32)
    x_re, x_im = bu[:, :S5_LANES], bu[:, S5_LANES:]
    groups = n // SUBLANES
    for j in range(step_ref.shape[0]):
        l_re = step_ref[j, :, :S5_LANES][None]
        l_im = step_ref[j, :, S5_LANES:][None]
        s_re = pltpu.roll(x_re, 2 ** j, axis=0).reshape(groups, SUBLANES, S5_LANES)
        s_im = pltpu.roll(x_im, 2 ** j, axis=0).reshape(groups, SUBLANES, S5_LANES)
        x_re = x_re + (l_re * s_re - l_im * s_im).reshape(n, S5_LANES)
        x_im = x_im + (l_re * s_im + l_im * s_re).reshape(n, S5_LANES)
    c_re, c_im = carry_ref[:, :S5_LANES], carry_ref[:, S5_LANES:]
    p_re, p_im = pw_ref[:, :S5_LANES], pw_ref[:, S5_LANES:]
    groups_re, groups_im = [], []
    for g in range(n // SUBLANES):
        g_re, g_im = x_re[g * SUBLANES:(g + 1) * SUBLANES], x_im[g * SUBLANES:(g + 1) * SUBLANES]
        g_re, g_im = g_re + p_re * c_re - p_im * c_im, g_im + p_re * c_im + p_im * c_re
        c_re, c_im = g_re[SUBLANES - 1:], g_im[SUBLANES - 1:]
        groups_re.append(g_re)
        groups_im.append(g_im)
    carry_ref[:, :S5_LANES] = c_re
    carry_ref[:, S5_LANES:] = c_im
    x_re, x_im = jnp.concatenate(groups_re, axis=0), jnp.concatenate(groups_im, axis=0)

    y = (jnp.dot(x_re.astype(BF16), cre_ref[...], preferred_element_type=F32)
         - jnp.dot(x_im.astype(BF16), cim_ref[...], preferred_element_type=F32)) + d_ref[...] * u
    y = jax.nn.gelu(y)
    y = y * jax.nn.sigmoid(jnp.dot(y.astype(BF16), wglu_ref[...], preferred_element_type=F32) + bglu_ref[...])
    o_ref[0] = _rms(y, gain_ref[...]).astype(o_ref.dtype)


def _block_diag(t):
    g, rows, cols = t.shape
    return jnp.einsum('grc,gh->grhc', t, jnp.eye(g, dtype=t.dtype)).reshape(g * rows, g * cols)


def _s5_mix(u, a_re, a_im, log_step, b_re, b_im, c_re, c_im, d, w_glu, b_glu, gain):
    b, s, width = u.shape
    lane = lambda t: t.astype(F32).reshape(1, S5_LANES)
    to_cp = lambda t: _block_diag(jnp.swapaxes(t.astype(F32), 1, 2))
    bmat, pw, steps = pl.pallas_call(
        _s5_prep_kernel,
        name="s5_discretise",
        out_shape=[jax.ShapeDtypeStruct((width, 2 * S5_LANES), BF16),
                   jax.ShapeDtypeStruct((SUBLANES, 2 * S5_LANES), F32),
                   jax.ShapeDtypeStruct((int(math.log2(SUBLANES)), SUBLANES, 2 * S5_LANES), F32)],
        compiler_params=pltpu.CompilerParams(vmem_limit_bytes=VMEM_LIMIT),
    )(lane(a_re), lane(a_im), lane(jnp.repeat(log_step[:, None], S5_STATE, axis=1)), to_cp(b_re), to_cp(b_im))
    cre = _block_diag(jnp.swapaxes(c_re, 1, 2)).astype(BF16)
    cim = _block_diag(jnp.swapaxes(c_im, 1, 2)).astype(BF16)
    vec = lambda t: t.astype(F32).reshape(1, width)
    consts = [bmat, pw, steps, cre, cim, vec(d), w_glu.astype(BF16), vec(b_glu), vec(gain)]
    return pl.pallas_call(
        _s5_kernel,
        name="s5_scan",
        grid=(b, s // S5_CHUNK),
        in_specs=[pl.BlockSpec((1, S5_CHUNK, width), lambda i, c: (i, c, 0))]
        + [_const_spec(t.shape) for t in consts],
        out_specs=pl.BlockSpec((1, S5_CHUNK, width), lambda i, c: (i, c, 0)),
        out_shape=jax.ShapeDtypeStruct((b, s, width), BF16),
        scratch_shapes=[pltpu.VMEM((1, 2 * S5_LANES), F32)],
        compiler_params=_params("parallel", "arbitrary"),
    )(u, *consts)


def _attn_kernel(q_ref, *refs):
    blocks = ATTN_KEY_BLOCKS + ATTN_STEP_TILES - 1
    k_refs = refs[:blocks]
    v_refs = refs[blocks:2 * blocks]
    bias_ref, gain_ref, o_ref = refs[2 * blocks:]
    n_keys = ATTN_KEY_BLOCKS * ATTN_TILE
    q_all = q_ref[0] * (HEAD_DIM ** -0.5)
    k_all = jnp.concatenate([kr[0] for kr in k_refs], axis=0)
    v_all = jnp.concatenate([vr[0] for vr in v_refs], axis=0)
    neg = jnp.finfo(F32).min
    in_band = bias_ref[0] != neg
    lane = lax.broadcasted_iota(jnp.int32, (1, n_keys), 1)
    heads = [slice(h * HEAD_DIM, (h + 1) * HEAD_DIM) for h in range(ATTN_HEADS)]
    work = []
    for i in range(ATTN_STEP_TILES):
        rows = slice(i * ATTN_TILE, (i + 1) * ATTN_TILE)
        keys = slice(i * ATTN_TILE, i * ATTN_TILE + n_keys)
        first_key = (pl.program_id(1) * ATTN_STEP_TILES + i - (ATTN_KEY_BLOCKS - 1)) * ATTN_TILE
        allowed = in_band & (first_key + lane >= 0)
        work += [(q_all[rows, sl], k_all[keys, sl], v_all[keys, sl], allowed, h) for h, sl in enumerate(heads)]
    s = [lax.dot_general(q, k, _CONTRACT["nt"], preferred_element_type=F32) for q, k, _, _, _ in work]
    s = [jnp.where(allowed, x + bias_ref[h], neg) for x, (_, _, _, allowed, h) in zip(s, work)]
    e = [jnp.exp(x - jnp.max(x, axis=-1, keepdims=True)) for x in s]
    pv = [jnp.dot(x.astype(BF16), v, preferred_element_type=F32) for x, (_, _, v, _, _) in zip(e, work)]
    outs = [x / jnp.sum(y, axis=-1, keepdims=True) for x, y in zip(pv, e)]
    tiles = [jnp.concatenate(outs[i * ATTN_HEADS:(i + 1) * ATTN_HEADS], axis=1) for i in range(ATTN_STEP_TILES)]
    o_ref[0] = _rms(jnp.concatenate(tiles, axis=0), gain_ref[...]).astype(o_ref.dtype)


def _attn_bias_table(rel_bias):
    n_keys = ATTN_KEY_BLOCKS * ATTN_TILE
    left = ATTN_LEFT * ATTN_CHUNK
    period = n_keys + ATTN_TILE
    rb = rel_bias.astype(F32)
    heads = rb.shape[0]
    far = lambda width: jnp.broadcast_to(rb[:, 2 * MAX_REL:], (heads, width))
    g = jnp.concatenate([far(left - MAX_REL), rb[:, 1:][:, ::-1], far(period - left - MAX_REL)], axis=1)
    toeplitz = jnp.tile(g, (1, ATTN_TILE))[:, :ATTN_TILE * (period - 1)].reshape(heads, ATTN_TILE, period - 1)
    q = np.arange(ATTN_TILE)[:, None]
    j = np.arange(n_keys)[None, :]
    in_band = (j // ATTN_CHUNK >= q // ATTN_CHUNK) & (j // ATTN_CHUNK <= q // ATTN_CHUNK + ATTN_LEFT)
    return jnp.where(in_band[None], toeplitz[:, :, :n_keys], jnp.finfo(F32).min)


def _chunk_attention(qkv, rel_bias, gain):
    b, s, _ = qkv.shape
    bias = _attn_bias_table(rel_bias)
    back = ATTN_KEY_BLOCKS - 1
    blocks = ATTN_KEY_BLOCKS + ATTN_STEP_TILES - 1
    step_rows = ATTN_STEP_TILES * ATTN_TILE
    key_specs = lambda col: [pl.BlockSpec((1, ATTN_TILE, ATTN_WIDTH),
                                          lambda i, c, j=j: (i, jnp.maximum(c * ATTN_STEP_TILES + j - back, 0), col))
                             for j in range(blocks)]
    return pl.pallas_call(
        _attn_kernel,
        name="chunk_attention",
        grid=(b, s // step_rows),
        in_specs=[pl.BlockSpec((1, step_rows, ATTN_WIDTH), lambda i, c: (i, c, 0))] + key_specs(1) + key_specs(2)
        + [_const_spec(bias.shape), _const_spec((1, ATTN_WIDTH))],
        out_specs=pl.BlockSpec((1, step_rows, ATTN_WIDTH), lambda i, c: (i, c, 0)),
        out_shape=jax.ShapeDtypeStruct((b, s, ATTN_WIDTH), BF16),
        compiler_params=_params("parallel", "parallel"),
    )(*([qkv] * (1 + 2 * blocks)), bias, gain.astype(F32).reshape(1, ATTN_WIDTH))


def _outproj_kernel(route, x_ref, orw_ref, os5_ref, oat_ref, w_ref, gpost_ref, gpre_ref, *refs):
    mixed = jnp.concatenate([orw_ref[...], os5_ref[...], oat_ref[...]], axis=1)
    y = jnp.dot(mixed, w_ref[...], preferred_element_type=F32)
    x = x_ref[...] + _rms(y, gpost_ref[...])
    h = _rms(x, gpre_ref[...])
    if not route:
        x_out, h_out = refs
    else:
        router_ref, x_out, h_out, route_out = refs
        logits = _mm(_terms(h, 2), _terms(router_ref[...], 2))
        lane = lax.broadcasted_iota(jnp.int32, logits.shape, 1).astype(F32)
        neg = jnp.finfo(F32).min
        logits = jnp.where(lane < N_EXPERTS, logits, neg)
        top1 = jnp.max(logits, axis=-1, keepdims=True)
        idx1 = jnp.min(jnp.where(logits == top1, lane, float(LANES)), axis=-1, keepdims=True)
        rest = jnp.where(lane == idx1, neg, logits)
        top2 = jnp.max(rest, axis=-1, keepdims=True)
        idx2 = jnp.min(jnp.where(rest == top2, lane, float(LANES)), axis=-1, keepdims=True)
        e2 = jnp.exp(top2 - top1)
        route_out[...] = (jnp.where(lane == 0, idx1, 0.0) + jnp.where(lane == 1, idx2, 0.0)
                          + jnp.where(lane == 2, 1.0 / (1.0 + e2), 0.0) + jnp.where(lane == 3, e2 / (1.0 + e2), 0.0))
    x_out[...] = x
    h_out[...] = h.astype(h_out.dtype)


def _out_projection(x, o_rwkv, o_s5, o_attn, w_out, gain_post, gain_pre, router):
    t, d = x.shape
    row = lambda n: pl.BlockSpec((ROW_TILE, n), lambda i: (i, 0))
    vec = lambda g: g.astype(F32).reshape(1, d)
    ins = [x, o_rwkv, o_s5, o_attn, w_out.astype(BF16), vec(gain_post), vec(gain_pre)]
    in_specs = [row(d), row(o_rwkv.shape[1]), row(o_s5.shape[1]), row(o_attn.shape[1]),
                _const_spec(w_out.shape), _const_spec((1, d)), _const_spec((1, d))]
    out_specs = [row(d), row(d)]
    out_shape = [jax.ShapeDtypeStruct((t, d), F32), jax.ShapeDtypeStruct((t, d), BF16 if router is None else F32)]
    if router is not None:
        ins.append(jnp.pad(router.astype(F32), ((0, 0), (0, LANES - N_EXPERTS))))
        in_specs.append(_const_spec((d, LANES)))
        out_specs.append(row(LANES))
        out_shape.append(jax.ShapeDtypeStruct((t, LANES), F32))
    return pl.pallas_call(
        functools.partial(_outproj_kernel, router is not None),
        name="out_projection",
        grid=(t // ROW_TILE,),
        in_specs=in_specs, out_specs=out_specs, out_shape=out_shape,
        compiler_params=_params("parallel"),
    )(*ins)


def _ffn_kernel(x_ref, h_ref, wg_ref, wu_ref, wd_ref, gain_ref, o_ref):
    h = h_ref[...]
    y = None
    for f in range(wg_ref.shape[1] // FFN_TILE):
        cols = slice(f * FFN_TILE, (f + 1) * FFN_TILE)
        act = (jax.nn.silu(jnp.dot(h, wg_ref[:, cols], preferred_element_type=F32))
               * jnp.dot(h, wu_ref[:, cols], preferred_element_type=F32))
        part = jnp.dot(act.astype(BF16), wd_ref[cols, :], preferred_element_type=F32)
        y = part if y is None else y + part
    o_ref[...] = x_ref[...] + _rms(y, gain_ref[...])


def _ffn(x, h, w_gate, w_up, w_down, gain):
    t, d = x.shape
    d_ff = w_gate.shape[1]
    row = lambda n: pl.BlockSpec((ROW_TILE, n), lambda i: (i, 0))
    resident = lambda shape: pl.BlockSpec(shape, lambda i: (0, 0), pipeline_mode=pl.Buffered(1))
    return pl.pallas_call(
        _ffn_kernel,
        name="swiglu_ffn",
        grid=(t // ROW_TILE,),
        in_specs=[row(d), row(d), resident((d, d_ff)), resident((d, d_ff)), resident((d_ff, d)), _const_spec((1, d))],
        out_specs=row(d),
        out_shape=jax.ShapeDtypeStruct((t, d), F32),
        compiler_params=_params("parallel"),
    )(x, h, w_gate.astype(BF16), w_up.astype(BF16), w_down.astype(BF16), gain.astype(F32).reshape(1, d))


MOE_TILE = 512


def _moe_plan(route, n_experts):
    t = route.shape[0]
    n = 2 * t
    tm = MOE_TILE
    expert = route[:, :2].astype(jnp.int32).reshape(n)
    onehot = (expert[:, None] == jnp.arange(n_experts, dtype=jnp.int32)[None, :]).astype(jnp.int32)
    rank = jnp.sum(jnp.cumsum(onehot, axis=0) * onehot, axis=1) - 1
    counts = jnp.sum(onehot, axis=0)
    padded = (counts + tm - 1) // tm * tm
    ends = jnp.cumsum(padded)
    slot = jnp.sum((ends - padded)[None, :] * onehot, axis=1) + rank
    n_tiles = n // tm + n_experts
    slots = n_tiles * tm
    ids = jnp.arange(n, dtype=jnp.int32)
    every = jnp.arange(slots, dtype=jnp.int32)
    dump = n + every % tm + tm * ((every // tm) % 2)
    dst = dump.at[slot].set((ids % 2) * t + ids // 2, unique_indices=True)
    tok = jnp.where(dst < n, dst % t, 0)
    tile_start = jnp.arange(n_tiles, dtype=jnp.int32) * tm
    tile_expert = jnp.minimum(jnp.sum((tile_start[:, None] >= ends[None, :]).astype(jnp.int32), axis=1), n_experts - 1)
    tile_valid = (tile_start < ends[-1]).astype(jnp.int32)
    dst = dst.reshape(n_tiles, 1, tm)
    before_first = (n + tm + jnp.arange(tm, dtype=jnp.int32)).reshape(1, 1, tm)
    return tok.reshape(n_tiles, 1, tm), dst, jnp.concatenate([before_first, dst[:-1]], axis=0), tile_expert, tile_valid


def _moe_ffn_kernel(te_ref, tv_ref, tok0_ref, tokn_ref, dstp_ref, dst_ref, h_hbm, wg_ref, wu_ref, wd_ref,
                    out_hbm, xbuf, xb, ybuf, acc_ref, gsem, ssem):
    t, f = pl.program_id(0), pl.program_id(1)
    n_t, n_f = pl.num_programs(0), pl.num_programs(1)
    tm = MOE_TILE
    slot = t % 2
    valid = tv_ref[t] == 1
    prev_valid = tv_ref[jnp.maximum(t - 1, 0)] == 1
    last = t == n_t - 1

    gather_row = lambda tok_ref, s, i: pltpu.make_async_copy(
        h_hbm.at[pl.ds(tok_ref[0, 0, i], 1), :], xbuf.at[s, pl.ds(i, 1), :], gsem.at[s])
    scatter_row = lambda d_ref, s, i: pltpu.make_async_copy(
        ybuf.at[s, pl.ds(i, 1), :], out_hbm.at[pl.ds(d_ref[0, 0, i], 1), :], ssem.at[s])
    wait_gather = lambda s: pltpu.make_async_copy(h_hbm.at[pl.ds(0, tm), :], xbuf.at[s], gsem.at[s]).wait()
    wait_scatter = lambda s: pltpu.make_async_copy(ybuf.at[s], out_hbm.at[pl.ds(0, tm), :], ssem.at[s]).wait()

    def looped(copy, idx_ref, s):
        def body(i, carry):
            copy(idx_ref, s, i).start()
            return carry
        lax.fori_loop(0, tm, body, 0, unroll=8)

    def compute():
        x = xb[...]
        act = jax.nn.silu(jnp.dot(x, wg_ref[0], preferred_element_type=F32)) * jnp.dot(x, wu_ref[0], preferred_element_type=F32)
        return jnp.dot(act.astype(BF16), wd_ref[0], preferred_element_type=F32)

    @pl.when((f == 0) & (t == 0))
    def _():
        looped(gather_row, tok0_ref, 0)
        ybuf[1] = jnp.zeros(ybuf.shape[1:], F32)
        for bank in range(2):
            fill = pltpu.make_async_copy(ybuf.at[1], out_hbm.at[pl.ds(out_hbm.shape[0] - (bank + 1) * tm, tm), :],
                                         ssem.at[1])
            fill.start()
            fill.wait()

    @pl.when((f == 0) & ((t == 0) | prev_valid))
    def _():
        wait_gather(slot)

    for parity in range(2):
        @pl.when((f == 0) & valid & (slot == parity))
        def _(parity=parity):
            xb[...] = xbuf[parity].astype(BF16)
            for i in range(tm):
                gather_row(tokn_ref, 1 - parity, i).start()
            acc_ref[...] = compute()

    @pl.when((f == n_f - 1) & ((t == 1) | ((t >= 2) & (tv_ref[jnp.maximum(t - 2, 0)] == 1))))
    def _():
        wait_scatter(slot)

    for parity in range(2):
        @pl.when((f == n_f - 1) & valid & (slot == parity))
        def _(parity=parity):
            for i in range(tm):
                scatter_row(dstp_ref, 1 - parity, i).start()
            ybuf[parity] = acc_ref[...] + compute()

    @pl.when((f == n_f - 1) & jnp.logical_not(valid) & (t >= 1) & prev_valid)
    def _():
        looped(scatter_row, dstp_ref, 1 - slot)

    @pl.when((f == n_f - 1) & last)
    def _():
        @pl.when(valid)
        def _():
            looped(scatter_row, dst_ref, slot)
            wait_scatter(slot)
            wait_gather(1 - slot)

        @pl.when(prev_valid)
        def _():
            wait_scatter(1 - slot)


def _moe_combine_kernel(x_ref, y1_ref, y2_ref, route_ref, gain_ref, o_ref):
    y = route_ref[:, 2:3] * y1_ref[...] + route_ref[:, 3:4] * y2_ref[...]
    o_ref[...] = x_ref[...] + _rms(y, gain_ref[...])


def _moe(x, h, route, w_gate, w_up, w_down, gain):
    t, d = x.shape
    n_e, _, d_ff = w_gate.shape
    tm = MOE_TILE
    assert d_ff == 2 * FFN_TILE
    tok, dst, dst_prev, tile_expert, tile_valid = _moe_plan(route, n_e)
    n_tiles = tok.shape[0]
    idx_spec = lambda shift: pl.BlockSpec(
        (1, 1, tm), lambda i, f, te, tv: (jnp.clip(i + shift, 0, n_tiles - 1), 0, 0), memory_space=pltpu.SMEM)
    grid_spec = pltpu.PrefetchScalarGridSpec(
        num_scalar_prefetch=2,
        grid=(n_tiles, d_ff // FFN_TILE),
        in_specs=[idx_spec(0), idx_spec(1), idx_spec(0), idx_spec(0),
                  pl.BlockSpec(memory_space=pl.ANY),
                  pl.BlockSpec((1, d, FFN_TILE), lambda i, f, te, tv: (te[i], 0, f)),
                  pl.BlockSpec((1, d, FFN_TILE), lambda i, f, te, tv: (te[i], 0, f)),
                  pl.BlockSpec((1, FFN_TILE, d), lambda i, f, te, tv: (te[i], f, 0))],
        out_specs=pl.BlockSpec(memory_space=pl.ANY),
        scratch_shapes=[pltpu.VMEM((2, tm, d), F32), pltpu.VMEM((tm, d), BF16), pltpu.VMEM((2, tm, d), F32),
                        pltpu.VMEM((tm, d), F32), pltpu.SemaphoreType.DMA((2,)), pltpu.SemaphoreType.DMA((2,))],
    )
    routed = pl.pallas_call(
        _moe_ffn_kernel,
        name="moe_experts",
        grid_spec=grid_spec,
        out_shape=jax.ShapeDtypeStruct((2 * t + 2 * tm, d), F32),
        compiler_params=_params("arbitrary", "arbitrary"),
    )(tile_expert, tile_valid, tok, tok, dst_prev, dst, h, w_gate.astype(BF16), w_up.astype(BF16), w_down.astype(BF16))
    row = lambda n: pl.BlockSpec((ROW_TILE, n), lambda i: (i, 0))
    second = pl.BlockSpec((ROW_TILE, d), lambda i: (i + t // ROW_TILE, 0))
    return pl.pallas_call(
        _moe_combine_kernel,
        name="moe_combine",
        grid=(t // ROW_TILE,),
        in_specs=[row(d), row(d), second, row(LANES), _const_spec((1, d))],
        out_specs=row(d),
        out_shape=jax.ShapeDtypeStruct((t, d), F32),
        compiler_params=_params("parallel"),
    )(x, routed, routed, route, gain.astype(F32).reshape(1, d))


def kernel(x, norm_mix_pre, norm_mix_post, norm_ffn_pre, norm_ffn_post, w_in, w_out, rwkv_mu, rwkv_w0, rwkv_w_up, rwkv_a0, rwkv_a_up, rwkv_g_up, rwkv_k_k, rwkv_k_a, rwkv_r_k, rwkv_ln_w, rwkv_ln_b, s5_a_re, s5_a_im, s5_log_step, s5_b_re, s5_b_im, s5_c_re, s5_c_im, s5_d, s5_w_glu, s5_b_glu, s5_norm, attn_rel_bias, attn_norm, ffn_w_gate, ffn_w_up, ffn_w_down, moe_router, moe_w_gate, moe_w_up, moe_w_down):
    b, s, d = x.shape
    depth = w_in.shape[0]
    xt = x.reshape(b * s, d)
    for layer in range(depth):
        p, u, qkv = _in_projection(xt, norm_mix_pre[layer], w_in[layer])
        o_rwkv = _rwkv_mix(p.reshape(b, s, -1), rwkv_mu[layer], rwkv_w0[layer], rwkv_w_up[layer], rwkv_a0[layer],
                           rwkv_a_up[layer], rwkv_g_up[layer], rwkv_k_k[layer], rwkv_k_a[layer], rwkv_r_k[layer],
                           rwkv_ln_w[layer], rwkv_ln_b[layer])
        o_s5 = _s5_mix(u.reshape(b, s, -1), s5_a_re[layer], s5_a_im[layer], s5_log_step[layer], s5_b_re[layer],
                       s5_b_im[layer], s5_c_re[layer], s5_c_im[layer], s5_d[layer], s5_w_glu[layer],
                       s5_b_glu[layer], s5_norm[layer])
        o_attn = _chunk_attention(qkv.reshape(b, s, -1), attn_rel_bias[layer], attn_norm[layer])
        i = layer // 2
        moe = layer % 2 == 1
        outs = _out_projection(xt, o_rwkv.reshape(b * s, -1), o_s5.reshape(b * s, -1), o_attn.reshape(b * s, -1),
                               w_out[layer], norm_mix_post[layer], norm_ffn_pre[layer],
                               moe_router[i] if moe else None)
        if moe:
            xt, h, route = outs
            xt = _moe(xt, h, route, moe_w_gate[i], moe_w_up[i], moe_w_down[i], norm_ffn_post[layer])
        else:
            xt, h = outs
            xt = _ffn(xt, h, ffn_w_gate[i], ffn_w_up[i], ffn_w_down[i], norm_ffn_post[layer])
    return xt.reshape(b, s, d)
```

```python
import functools
import math

import jax
import jax.numpy as jnp
import numpy as np
from jax import lax
from jax.experimental import pallas as pl
from jax.experimental.pallas import tpu as pltpu

F32 = jnp.float32
BF16 = jnp.bfloat16

RMS_EPS = 1e-6
RWKV_GN_EPS = 64e-5
HEAD_DIM = 64
RWKV_HEADS = 6
RWKV_WIDTH = 384
RWKV_IN = 1408
S5_WIDTH = 256
S5_GROUPS = 16
S5_STATE = 64
S5_LANES = S5_GROUPS * S5_STATE
ATTN_HEADS = 6
ATTN_WIDTH = 384
ATTN_CHUNK = 64
ATTN_LEFT = 8
MAX_REL = 128
N_EXPERTS = 8
LANES = 128

RWKV_CHUNK = 64
RWKV_STEP_CHUNKS = 2
S5_CHUNK = 512
SUBLANES = 8
ATTN_TILE = 2 * ATTN_CHUNK
ATTN_KEY_BLOCKS = (ATTN_LEFT * ATTN_CHUNK) // ATTN_TILE + 1
ATTN_STEP_TILES = 2
ROW_TILE = 512
FFN_TILE = 1408
VMEM_LIMIT = 48 * 1024 * 1024

_HI = lax.Precision.HIGHEST
_CONTRACT = {"nn": (((1,), (0,)), ((), ())), "nt": (((1,), (1,)), ((), ())), "tn": (((0,), (0,)), ((), ()))}


def _hdot(a, b):
    return jnp.dot(a, b, precision=_HI, preferred_element_type=F32)


def _terms(x, n):
    out = []
    for i in range(n):
        t = x.astype(BF16)
        out.append(t)
        if i + 1 < n:
            x = x - t.astype(F32)
    return tuple(out)


def _mm(a, b, kind="nn"):
    order = max(len(a), len(b))
    acc = None
    for i, ai in enumerate(a):
        for j, bj in enumerate(b):
            if i + j < order:
                d = lax.dot_general(ai, bj, _CONTRACT[kind], preferred_element_type=F32)
                acc = d if acc is None else acc + d
    return acc


def _rms(x, gain):
    return x * lax.rsqrt(jnp.mean(x * x, axis=-1, keepdims=True) + RMS_EPS) * gain


def _params(*semantics):
    return pltpu.CompilerParams(dimension_semantics=semantics, vmem_limit_bytes=VMEM_LIMIT)


def _const_spec(shape):
    zeros = (0,) * len(shape)
    return pl.BlockSpec(shape, lambda *_: zeros)


def _inproj_kernel(x_ref, gain_ref, w_ref, p_ref, u_ref, qkv_ref):
    h = _rms(x_ref[...], gain_ref[...]).astype(BF16)
    s5_at, qkv_at = RWKV_IN, RWKV_IN + S5_WIDTH
    p_ref[...] = jnp.dot(h, w_ref[:, :s5_at], preferred_element_type=F32)
    u_ref[...] = jnp.dot(h, w_ref[:, s5_at:qkv_at], preferred_element_type=F32)
    qkv_ref[...] = jnp.dot(h, w_ref[:, qkv_at:], preferred_element_type=F32).astype(BF16)


def _in_projection(x, gain, w_in):
    t, d = x.shape
    n_in = w_in.shape[1]
    n_qkv = n_in - RWKV_IN - S5_WIDTH
    row = lambda n: pl.BlockSpec((ROW_TILE, n), lambda i: (i, 0))
    return pl.pallas_call(
        _inproj_kernel,
        name="in_projection",
        grid=(t // ROW_TILE,),
        in_specs=[row(d), _const_spec((1, d)), _const_spec((d, n_in))],
        out_specs=[row(RWKV_IN), row(S5_WIDTH), row(n_qkv)],
        out_shape=[jax.ShapeDtypeStruct((t, RWKV_IN), F32), jax.ShapeDtypeStruct((t, S5_WIDTH), F32),
                   jax.ShapeDtypeStruct((t, n_qkv), BF16)],
        compiler_params=_params("parallel"),
    )(x, gain.reshape(1, d), w_in.astype(BF16))


RWKV_TERMS = 1
HEADS_PER_VREG = LANES // HEAD_DIM


def _t(x):
    return _terms(x, RWKV_TERMS)


def _head_sums(x):
    first = lax.broadcasted_iota(jnp.int32, (1, LANES), 1) < HEAD_DIM
    outs = []
    for j in range(x.shape[1] // LANES):
        xp = x[:, j * LANES:(j + 1) * LANES]
        s0 = jnp.sum(jnp.where(first, xp, 0.0), axis=-1, keepdims=True)
        s1 = jnp.sum(jnp.where(first, 0.0, xp), axis=-1, keepdims=True)
        outs.append(jnp.where(first, s0, s1))
    return jnp.concatenate(outs, axis=1)


def _rwkv_kernel(p_ref, mu_ref, w0_ref, wup_ref, a0_ref, aup_ref, gup_ref, kk_ref, ka_ref, rk_ref,
                 lnw_ref, lnb_ref, o_ref, prev_ref, state_ref):
    nb, per_seq = p_ref.shape[0], p_ref.shape[1]
    n = RWKV_CHUNK
    rows = nb * per_seq
    items = rows // n
    w_at = 3 * RWKV_WIDTH
    pairs = RWKV_WIDTH // LANES

    @pl.when(pl.program_id(0) == 0)
    def _():
        prev_ref[...] = jnp.zeros_like(prev_ref)
        state_ref[...] = jnp.zeros_like(state_ref)

    p = p_ref[...].reshape(rows, RWKV_IN)
    row = lax.broadcasted_iota(jnp.int32, (rows, 1), 0)
    shifted = pltpu.roll(p, 1, axis=0)
    for b in range(nb):
        shifted = jnp.where(row == b * per_seq, prev_ref[b:b + 1, :], shifted)
    for b in range(nb):
        prev_ref[b:b + 1, :] = p[(b + 1) * per_seq - 1:(b + 1) * per_seq, :]
    p = p + (shifted - p) * mu_ref[...]

    r = p[:, :RWKV_WIDTH]
    k = p[:, RWKV_WIDTH:2 * RWKV_WIDTH]
    v = p[:, 2 * RWKV_WIDTH:w_at]
    w_lo = p[:, w_at:w_at + 64]
    a_lo = p[:, w_at + 64:w_at + 128]
    g_lo = p[:, w_at + 128:]

    lora = lambda x, w_ref: _mm(_t(x), tuple(w_ref[i] for i in range(RWKV_TERMS)))
    w = -jax.nn.softplus(-(w0_ref[...] + lora(jnp.tanh(w_lo), wup_ref))) - 0.5
    log_decay = -jnp.exp(w)
    a = jax.nn.sigmoid(a0_ref[...] + lora(a_lo, aup_ref))
    gate = lora(jax.nn.sigmoid(g_lo), gup_ref)

    kk = k * kk_ref[...]
    kk = kk * lax.rsqrt(jnp.maximum(_head_sums(kk * kk), 1e-24))
    k = k * (1.0 + (a - 1.0) * ka_ref[...])

    s_i = lax.broadcasted_iota(jnp.int32, (rows, rows), 0)
    s_j = lax.broadcasted_iota(jnp.int32, (rows, rows), 1)
    causal = ((s_i // n == s_j // n) & (s_i >= s_j)).astype(BF16)
    cum = _mm((causal,), _terms(log_decay, 3))
    cum_end = cum[n - 1:n, :]
    for it in range(1, items):
        cum_end = jnp.where(row >= it * n, cum[(it + 1) * n - 1:(it + 1) * n, :], cum_end)
    inv_g = jnp.exp(-cum)
    to_end = jnp.exp(cum_end - cum)
    beta = kk * a
    r_bar = r * jnp.exp(cum)
    a_bar = -kk * jnp.exp(cum - log_decay)
    b_til = beta * inv_g
    k_til = k * inv_g
    b_end = beta * to_end
    k_end = k * to_end
    g_end = jnp.exp(cum_end)

    lane = lax.broadcasted_iota(jnp.int32, (1, LANES), 1)
    q_i = lax.broadcasted_iota(jnp.int32, (n, LANES), 0)
    q_j = lax.broadcasted_iota(jnp.int32, (n, LANES), 1) % n
    lower = q_i >= q_j
    strictly_lower = q_i > q_j
    eye = (lax.broadcasted_iota(jnp.int32, (n, n), 0) == lax.broadcasted_iota(jnp.int32, (n, n), 1)).astype(F32)
    c_i = lax.broadcasted_iota(jnp.int32, (LANES, LANES), 0)
    c_j = lax.broadcasted_iota(jnp.int32, (LANES, LANES), 1)
    same_head = c_i // HEAD_DIM == c_j // HEAD_DIM
    diagonal = c_i == c_j
    zeros = jnp.zeros((n, LANES), F32)

    pair_ids = [(it, j) for it in range(items) for j in range(pairs)]
    head_ids = [(pi, h) for pi in range(len(pair_ids)) for h in range(HEADS_PER_VREG)]
    heads_of = lambda pi: [hi for hi, (p_i, _) in enumerate(head_ids) if p_i == pi]
    cut = lambda x, pi: x[pair_ids[pi][0] * n:(pair_ids[pi][0] + 1) * n,
                          pair_ids[pi][1] * LANES:(pair_ids[pi][1] + 1) * LANES]
    rb = [cut(r_bar, pi) for pi in range(len(pair_ids))]
    be = [cut(b_end, pi) for pi in range(len(pair_ids))]
    vv = [cut(v, pi) for pi in range(len(pair_ids))]
    bk = [_t(jnp.concatenate([cut(b_til, pi), cut(k_til, pi)], axis=0)) for pi in range(len(pair_ids))]
    mine = [lane // HEAD_DIM == h for _, h in head_ids]
    abm = [jnp.where(mine[hi], cut(a_bar, pi), 0.0) for hi, (pi, _) in enumerate(head_ids)]
    v0 = [_t(jnp.concatenate([zeros, jnp.where(mine[hi], vv[pi], 0.0)], axis=0)) for hi, (pi, _) in enumerate(head_ids)]
    m = [_mm(_t(jnp.concatenate([abm[hi], jnp.where(mine[hi], rb[pi], 0.0)], axis=0)), bk[pi], "nt")
         for hi, (pi, _) in enumerate(head_ids)]
    top = [jnp.where(strictly_lower, x[:n], 0.0) for x in m]
    bot = [jnp.where(lower, x[n:], 0.0) for x in m]
    y = [_mm(_t(top[hi]), v0[hi]) for hi in range(len(head_ids))]
    power = [x[:, :n] for x in top]
    inv = [eye + x for x in power]
    for _ in range(int(math.log2(n)) - 1):
        pt = [_t(x) for x in power]
        power = [_mm(x, x) for x in pt]
        inv = [x + _mm(_t(x), _t(pw)) for x, pw in zip(inv, power)]
    wu = [_mm(_t(inv[hi]), _t(jnp.concatenate([abm[hi], y[hi]], axis=1))) for hi in range(len(head_ids))]
    v_low = [tuple(jnp.concatenate([jnp.zeros((n, LANES), BF16), t[n:]], axis=1) for t in v0[hi])
             for hi in range(len(head_ids))]
    ro = [_mm(_t(bot[hi]), tuple(jnp.concatenate([a, b2], axis=0) for a, b2 in zip(_t(wu[hi]), v_low[hi])))
          for hi in range(len(head_ids))]
    both = lambda xs, pi, sl: sum(xs[hi][:, sl] for hi in heads_of(pi))
    first, second = slice(0, LANES), slice(LANES, 2 * LANES)
    g_t = [jnp.where(same_head, _mm(_t(be[pi]), _t(both(wu, pi, first)), "tn"), 0.0)
           + jnp.where(diagonal, cut(g_end, pi)[:1], 0.0) for pi in range(len(pair_ids))]
    s2_t = [_mm(_t(jnp.concatenate([be[pi], cut(k_end, pi)], axis=0)),
                _t(jnp.concatenate([both(wu, pi, second), vv[pi]], axis=0)), "tn") for pi in range(len(pair_ids))]
    r2 = [_t(rb[pi] + both(ro, pi, first)) for pi in range(len(pair_ids))]
    out = [None] * len(pair_ids)
    per_step = items // nb
    for b in range(nb):
        for j in range(pairs):
            state = state_ref[b * pairs + j]
            for c in range(per_step):
                pi = (b * per_step + c) * pairs + j
                out[pi] = _mm(r2[pi], _t(state)) + both(ro, pi, second)
                state = _mm(_t(g_t[pi]), _t(state)) + jnp.where(same_head, s2_t[pi], 0.0)
            state_ref[b * pairs + j] = state
    o = jnp.concatenate([jnp.concatenate(out[it * pairs:(it + 1) * pairs], axis=1) for it in range(items)], axis=0)

    inv_n = 1.0 / HEAD_DIM
    cen = o - _head_sums(o) * inv_n
    var = _head_sums(cen * cen) * inv_n
    o = cen * lax.rsqrt(var + RWKV_GN_EPS) * lnw_ref[...] + lnb_ref[...]
    o = o + _head_sums(r * k * rk_ref[...]) * v
    o_ref[...] = (o * gate).astype(o_ref.dtype).reshape(nb, per_seq, RWKV_WIDTH)


def _rwkv_mix(p, mu, w0, w_up, a0, a_up, g_up, k_k, k_a, r_k, ln_w, ln_b):
    b, s, _ = p.shape
    per_step = RWKV_CHUNK * RWKV_STEP_CHUNKS
    vec = lambda t: t.astype(F32).reshape(1, -1)
    split = lambda t: jnp.stack(_terms(t.astype(F32), RWKV_TERMS))
    consts = [vec(mu), vec(w0), split(w_up), vec(a0), split(a_up), split(g_up), vec(k_k), vec(k_a), vec(r_k),
              vec(ln_w), vec(ln_b)]
    return pl.pallas_call(
        _rwkv_kernel,
        name="rwkv7",
        grid=(s // per_step,),
        in_specs=[pl.BlockSpec((b, per_step, RWKV_IN), lambda c: (0, c, 0))]
        + [_const_spec(t.shape) for t in consts],
        out_specs=pl.BlockSpec((b, per_step, RWKV_WIDTH), lambda c: (0, c, 0)),
        out_shape=jax.ShapeDtypeStruct((b, s, RWKV_WIDTH), BF16),
        scratch_shapes=[pltpu.VMEM((b, RWKV_IN), F32),
                        pltpu.VMEM((b * RWKV_WIDTH // LANES, LANES, LANES), F32)],
        compiler_params=_params("arbitrary"),
    )(p, *consts)


def _s5_prep_kernel(are_ref, aim_ref, lstep_ref, bre_ref, bim_ref, bmat_ref, pw_ref, step_ref):
    lam_re = jnp.minimum(are_ref[...], -1e-4)
    lam_im = aim_ref[...]
    step = jnp.exp(lstep_ref[...])
    z_re, z_im = lam_re * step, lam_im * step
    mag = jnp.exp(z_re)
    num_re, num_im = mag * jnp.cos(z_im) - 1.0, mag * jnp.sin(z_im)
    den = lam_re * lam_re + lam_im * lam_im
    coef_re = (num_re * lam_re + num_im * lam_im) / den
    coef_im = (num_im * lam_re - num_re * lam_im) / den
    bre, bim = bre_ref[...], bim_ref[...]
    bmat_ref[:, :S5_LANES] = (coef_re * bre - coef_im * bim).astype(bmat_ref.dtype)
    bmat_ref[:, S5_LANES:] = (coef_re * bim + coef_im * bre).astype(bmat_ref.dtype)
    steps = (lax.broadcasted_iota(jnp.int32, (SUBLANES, 1), 0) + 1).astype(F32)
    pmag = jnp.exp(steps * z_re)
    pw_ref[:, :S5_LANES] = pmag * jnp.cos(steps * z_im)
    pw_ref[:, S5_LANES:] = pmag * jnp.sin(steps * z_im)
    sub = lax.broadcasted_iota(jnp.int32, (SUBLANES, 1), 0)
    for j in range(step_ref.shape[0]):
        dist = 2 ** j
        mag_j = jnp.where(sub >= dist, jnp.exp(dist * z_re), 0.0)
        step_ref[j, :, :S5_LANES] = mag_j * jnp.cos(dist * z_im)
        step_ref[j, :, S5_LANES:] = mag_j * jnp.sin(dist * z_im)


def _s5_kernel(u_ref, bmat_ref, pw_ref, step_ref, cre_ref, cim_ref, d_ref, wglu_ref, bglu_ref, gain_ref, o_ref, carry_ref):
    n = S5_CHUNK

    @pl.when(pl.program_id(1) == 0)
    def _():
        carry_ref[...] = jnp.zeros_like(carry_ref)

    u = u_ref[0]
    bu = jnp.dot(u.astype(BF16), bmat_ref[...], preferred_element_type=F32)
    x_re, x_im = bu[:, :S5_LANES], bu[:, S5_LANES:]
    groups = n // SUBLANES
    for j in range(step_ref.shape[0]):
        l_re = step_ref[j, :, :S5_LANES][None]
        l_im = step_ref[j, :, S5_LANES:][None]
        s_re = pltpu.roll(x_re, 2 ** j, axis=0).reshape(groups, SUBLANES, S5_LANES)
        s_im = pltpu.roll(x_im, 2 ** j, axis=0).reshape(groups, SUBLANES, S5_LANES)
        x_re = x_re + (l_re * s_re - l_im * s_im).reshape(n, S5_LANES)
        x_im = x_im + (l_re * s_im + l_im * s_re).reshape(n, S5_LANES)
    c_re, c_im = carry_ref[:, :S5_LANES], carry_ref[:, S5_LANES:]
    p_re, p_im = pw_ref[:, :S5_LANES], pw_ref[:, S5_LANES:]
    groups_re, groups_im = [], []
    for g in range(n // SUBLANES):
        g_re, g_im = x_re[g * SUBLANES:(g + 1) * SUBLANES], x_im[g * SUBLANES:(g + 1) * SUBLANES]
        g_re, g_im = g_re + p_re * c_re - p_im * c_im, g_im + p_re * c_im + p_im * c_re
        c_re, c_im = g_re[SUBLANES - 1:], g_im[SUBLANES - 1:]
        groups_re.append(g_re)
        groups_im.append(g_im)
    carry_ref[:, :S5_LANES] = c_re
    carry_ref[:, S5_LANES:] = c_im
    x_re, x_im = jnp.concatenate(groups_re, axis=0), jnp.concatenate(groups_im, axis=0)

    y = (jnp.dot(x_re.astype(BF16), cre_ref[...], preferred_element_type=F32)
         - jnp.dot(x_im.astype(BF16), cim_ref[...], preferred_element_type=F32)) + d_ref[...] * u
    y = jax.nn.gelu(y)
    y = y * jax.nn.sigmoid(jnp.dot(y.astype(BF16), wglu_ref[...], preferred_element_type=F32) + bglu_ref[...])
    o_ref[0] = _rms(y, gain_ref[...]).astype(o_ref.dtype)


def _block_diag(t):
    g, rows, cols = t.shape
    return jnp.einsum('grc,gh->grhc', t, jnp.eye(g, dtype=t.dtype)).reshape(g * rows, g * cols)


def _s5_mix(u, a_re, a_im, log_step, b_re, b_im, c_re, c_im, d, w_glu, b_glu, gain):
    b, s, width = u.shape
    lane = lambda t: t.astype(F32).reshape(1, S5_LANES)
    to_cp = lambda t: _block_diag(jnp.swapaxes(t.astype(F32), 1, 2))
    bmat, pw, steps = pl.pallas_call(
        _s5_prep_kernel,
        name="s5_discretise",
        out_shape=[jax.ShapeDtypeStruct((width, 2 * S5_LANES), BF16),
                   jax.ShapeDtypeStruct((SUBLANES, 2 * S5_LANES), F32),
                   jax.ShapeDtypeStruct((int(math.log2(SUBLANES)), SUBLANES, 2 * S5_LANES), F32)],
        compiler_params=pltpu.CompilerParams(vmem_limit_bytes=VMEM_LIMIT),
    )(lane(a_re), lane(a_im), lane(jnp.repeat(log_step[:, None], S5_STATE, axis=1)), to_cp(b_re), to_cp(b_im))
    cre = _block_diag(jnp.swapaxes(c_re, 1, 2)).astype(BF16)
    cim = _block_diag(jnp.swapaxes(c_im, 1, 2)).astype(BF16)
    vec = lambda t: t.astype(F32).reshape(1, width)
    consts = [bmat, pw, steps, cre, cim, vec(d), w_glu.astype(BF16), vec(b_glu), vec(gain)]
    return pl.pallas_call(
        _s5_kernel,
        name="s5_scan",
        grid=(b, s // S5_CHUNK),
        in_specs=[pl.BlockSpec((1, S5_CHUNK, width), lambda i, c: (i, c, 0))]
        + [_const_spec(t.shape) for t in consts],
        out_specs=pl.BlockSpec((1, S5_CHUNK, width), lambda i, c: (i, c, 0)),
        out_shape=jax.ShapeDtypeStruct((b, s, width), BF16),
        scratch_shapes=[pltpu.VMEM((1, 2 * S5_LANES), F32)],
        compiler_params=_params("parallel", "arbitrary"),
    )(u, *consts)


def _attn_kernel(q_ref, *refs):
    blocks = ATTN_KEY_BLOCKS + ATTN_STEP_TILES - 1
    k_refs = refs[:blocks]
    v_refs = refs[blocks:2 * blocks]
    bias_ref, gain_ref, o_ref = refs[2 * blocks:]
    n_keys = ATTN_KEY_BLOCKS * ATTN_TILE
    q_all = q_ref[0] * (HEAD_DIM ** -0.5)
    k_all = jnp.concatenate([kr[0] for kr in k_refs], axis=0)
    v_all = jnp.concatenate([vr[0] for vr in v_refs], axis=0)
    neg = jnp.finfo(F32).min
    in_band = bias_ref[0] != neg
    lane = lax.broadcasted_iota(jnp.int32, (1, n_keys), 1)
    heads = [slice(h * HEAD_DIM, (h + 1) * HEAD_DIM) for h in range(ATTN_HEADS)]
    work = []
    for i in range(ATTN_STEP_TILES):
        rows = slice(i * ATTN_TILE, (i + 1) * ATTN_TILE)
        keys = slice(i * ATTN_TILE, i * ATTN_TILE + n_keys)
        first_key = (pl.program_id(1) * ATTN_STEP_TILES + i - (ATTN_KEY_BLOCKS - 1)) * ATTN_TILE
        allowed = in_band & (first_key + lane >= 0)
        work += [(q_all[rows, sl], k_all[keys, sl], v_all[keys, sl], allowed, h) for h, sl in enumerate(heads)]
    s = [lax.dot_general(q, k, _CONTRACT["nt"], preferred_element_type=F32) for q, k, _, _, _ in work]
    s = [jnp.where(allowed, x + bias_ref[h], neg) for x, (_, _, _, allowed, h) in zip(s, work)]
    e = [jnp.exp(x - jnp.max(x, axis=-1, keepdims=True)) for x in s]
    pv = [jnp.dot(x.astype(BF16), v, preferred_element_type=F32) for x, (_, _, v, _, _) in zip(e, work)]
    outs = [x / jnp.sum(y, axis=-1, keepdims=True) for x, y in zip(pv, e)]
    tiles = [jnp.concatenate(outs[i * ATTN_HEADS:(i + 1) * ATTN_HEADS], axis=1) for i in range(ATTN_STEP_TILES)]
    o_ref[0] = _rms(jnp.concatenate(tiles, axis=0), gain_ref[...]).astype(o_ref.dtype)


def _attn_bias_table(rel_bias):
    n_keys = ATTN_KEY_BLOCKS * ATTN_TILE
    left = ATTN_LEFT * ATTN_CHUNK
    period = n_keys + ATTN_TILE
    rb = rel_bias.astype(F32)
    heads = rb.shape[0]
    far = lambda width: jnp.broadcast_to(rb[:, 2 * MAX_REL:], (heads, width))
    g = jnp.concatenate([far(left - MAX_REL), rb[:, 1:][:, ::-1], far(period - left - MAX_REL)], axis=1)
    toeplitz = jnp.tile(g, (1, ATTN_TILE))[:, :ATTN_TILE * (period - 1)].reshape(heads, ATTN_TILE, period - 1)
    q = np.arange(ATTN_TILE)[:, None]
    j = np.arange(n_keys)[None, :]
    in_band = (j // ATTN_CHUNK >= q // ATTN_CHUNK) & (j // ATTN_CHUNK <= q // ATTN_CHUNK + ATTN_LEFT)
    return jnp.where(in_band[None], toeplitz[:, :, :n_keys], jnp.finfo(F32).min)


def _chunk_attention(qkv, rel_bias, gain):
    b, s, _ = qkv.shape
    bias = _attn_bias_table(rel_bias)
    back = ATTN_KEY_BLOCKS - 1
    blocks = ATTN_KEY_BLOCKS + ATTN_STEP_TILES - 1
    step_rows = ATTN_STEP_TILES * ATTN_TILE
    key_specs = lambda col: [pl.BlockSpec((1, ATTN_TILE, ATTN_WIDTH),
                                          lambda i, c, j=j: (i, jnp.maximum(c * ATTN_STEP_TILES + j - back, 0), col))
                             for j in range(blocks)]
    return pl.pallas_call(
        _attn_kernel,
        name="chunk_attention",
        grid=(b, s // step_rows),
        in_specs=[pl.BlockSpec((1, step_rows, ATTN_WIDTH), lambda i, c: (i, c, 0))] + key_specs(1) + key_specs(2)
        + [_const_spec(bias.shape), _const_spec((1, ATTN_WIDTH))],
        out_specs=pl.BlockSpec((1, step_rows, ATTN_WIDTH), lambda i, c: (i, c, 0)),
        out_shape=jax.ShapeDtypeStruct((b, s, ATTN_WIDTH), BF16),
        compiler_params=_params("parallel", "parallel"),
    )(*([qkv] * (1 + 2 * blocks)), bias, gain.astype(F32).reshape(1, ATTN_WIDTH))


def _outproj_kernel(route, x_ref, orw_ref, os5_ref, oat_ref, w_ref, gpost_ref, gpre_ref, *refs):
    mixed = jnp.concatenate([orw_ref[...], os5_ref[...], oat_ref[...]], axis=1)
    y = jnp.dot(mixed, w_ref[...], preferred_element_type=F32)
    x = x_ref[...] + _rms(y, gpost_ref[...])
    h = _rms(x, gpre_ref[...])
    if not route:
        x_out, h_out = refs
    else:
        router_ref, x_out, h_out, route_out = refs
        logits = _mm(_terms(h, 2), _terms(router_ref[...], 2))
        lane = lax.broadcasted_iota(jnp.int32, logits.shape, 1).astype(F32)
        neg = jnp.finfo(F32).min
        logits = jnp.where(lane < N_EXPERTS, logits, neg)
        top1 = jnp.max(logits, axis=-1, keepdims=True)
        idx1 = jnp.min(jnp.where(logits == top1, lane, float(LANES)), axis=-1, keepdims=True)
        rest = jnp.where(lane == idx1, neg, logits)
        top2 = jnp.max(rest, axis=-1, keepdims=True)
        idx2 = jnp.min(jnp.where(rest == top2, lane, float(LANES)), axis=-1, keepdims=True)
        e2 = jnp.exp(top2 - top1)
        route_out[...] = (jnp.where(lane == 0, idx1, 0.0) + jnp.where(lane == 1, idx2, 0.0)
                          + jnp.where(lane == 2, 1.0 / (1.0 + e2), 0.0) + jnp.where(lane == 3, e2 / (1.0 + e2), 0.0))
    x_out[...] = x
    h_out[...] = h.astype(h_out.dtype)


def _out_projection(x, o_rwkv, o_s5, o_attn, w_out, gain_post, gain_pre, router):
    t, d = x.shape
    row = lambda n: pl.BlockSpec((ROW_TILE, n), lambda i: (i, 0))
    vec = lambda g: g.astype(F32).reshape(1, d)
    ins = [x, o_rwkv, o_s5, o_attn, w_out.astype(BF16), vec(gain_post), vec(gain_pre)]
    in_specs = [row(d), row(o_rwkv.shape[1]), row(o_s5.shape[1]), row(o_attn.shape[1]),
                _const_spec(w_out.shape), _const_spec((1, d)), _const_spec((1, d))]
    out_specs = [row(d), row(d)]
    out_shape = [jax.ShapeDtypeStruct((t, d), F32), jax.ShapeDtypeStruct((t, d), BF16 if router is None else F32)]
    if router is not None:
        ins.append(jnp.pad(router.astype(F32), ((0, 0), (0, LANES - N_EXPERTS))))
        in_specs.append(_const_spec((d, LANES)))
        out_specs.append(row(LANES))
        out_shape.append(jax.ShapeDtypeStruct((t, LANES), F32))
    return pl.pallas_call(
        functools.partial(_outproj_kernel, router is not None),
        name="out_projection",
        grid=(t // ROW_TILE,),
        in_specs=in_specs, out_specs=out_specs, out_shape=out_shape,
        compiler_params=_params("parallel"),
    )(*ins)


def _ffn_kernel(x_ref, h_ref, wg_ref, wu_ref, wd_ref, gain_ref, o_ref):
    h = h_ref[...]
    y = None
    for f in range(wg_ref.shape[1] // FFN_TILE):
        cols = slice(f * FFN_TILE, (f + 1) * FFN_TILE)
        act = (jax.nn.silu(jnp.dot(h, wg_ref[:, cols], preferred_element_type=F32))
               * jnp.dot(h, wu_ref[:, cols], preferred_element_type=F32))
        part = jnp.dot(act.astype(BF16), wd_ref[cols, :], preferred_element_type=F32)
        y = part if y is None else y + part
    o_ref[...] = x_ref[...] + _rms(y, gain_ref[...])


def _ffn(x, h, w_gate, w_up, w_down, gain):
    t, d = x.shape
    d_ff = w_gate.shape[1]
    row = lambda n: pl.BlockSpec((ROW_TILE, n), lambda i: (i, 0))
    resident = lambda shape: pl.BlockSpec(shape, lambda i: (0, 0), pipeline_mode=pl.Buffered(1))
    return pl.pallas_call(
        _ffn_kernel,
        name="swiglu_ffn",
        grid=(t // ROW_TILE,),
        in_specs=[row(d), row(d), resident((d, d_ff)), resident((d, d_ff)), resident((d_ff, d)), _const_spec((1, d))],
        out_specs=row(d),
        out_shape=jax.ShapeDtypeStruct((t, d), F32),
        compiler_params=_params("parallel"),
    )(x, h, w_gate.astype(BF16), w_up.astype(BF16), w_down.astype(BF16), gain.astype(F32).reshape(1, d))


MOE_TILE = 512


def _moe_plan(route, n_experts):
    t = route.shape[0]
    n = 2 * t
    tm = MOE_TILE
    expert = route[:, :2].astype(jnp.int32).reshape(n)
    onehot = (expert[:, None] == jnp.arange(n_experts, dtype=jnp.int32)[None, :]).astype(jnp.int32)
    rank = jnp.sum(jnp.cumsum(onehot, axis=0) * onehot, axis=1) - 1
    counts = jnp.sum(onehot, axis=0)
    padded = (counts + tm - 1) // tm * tm
    ends = jnp.cumsum(padded)
    slot = jnp.sum((ends - padded)[None, :] * onehot, axis=1) + rank
    n_tiles = n // tm + n_experts
    slots = n_tiles * tm
    ids = jnp.arange(n, dtype=jnp.int32)
    every = jnp.arange(slots, dtype=jnp.int32)
    dump = n + every % tm + tm * ((every // tm) % 2)
    dst = dump.at[slot].set((ids % 2) * t + ids // 2, unique_indices=True)
    tok = jnp.where(dst < n, dst % t, 0)
    tile_start = jnp.arange(n_tiles, dtype=jnp.int32) * tm
    tile_expert = jnp.minimum(jnp.sum((tile_start[:, None] >= ends[None, :]).astype(jnp.int32), axis=1), n_experts - 1)
    tile_valid = (tile_start < ends[-1]).astype(jnp.int32)
    dst = dst.reshape(n_tiles, 1, tm)
    before_first = (n + tm + jnp.arange(tm, dtype=jnp.int32)).reshape(1, 1, tm)
    return tok.reshape(n_tiles, 1, tm), dst, jnp.concatenate([before_first, dst[:-1]], axis=0), tile_expert, tile_valid


def _moe_ffn_kernel(te_ref, tv_ref, tok0_ref, tokn_ref, dstp_ref, dst_ref, h_hbm, wg_ref, wu_ref, wd_ref,
                    out_hbm, xbuf, xb, ybuf, acc_ref, gsem, ssem):
    t, f = pl.program_id(0), pl.program_id(1)
    n_t, n_f = pl.num_programs(0), pl.num_programs(1)
    tm = MOE_TILE
    slot = t % 2
    valid = tv_ref[t] == 1
    prev_valid = tv_ref[jnp.maximum(t - 1, 0)] == 1
    last = t == n_t - 1

    gather_row = lambda tok_ref, s, i: pltpu.make_async_copy(
        h_hbm.at[pl.ds(tok_ref[0, 0, i], 1), :], xbuf.at[s, pl.ds(i, 1), :], gsem.at[s])
    scatter_row = lambda d_ref, s, i: pltpu.make_async_copy(
        ybuf.at[s, pl.ds(i, 1), :], out_hbm.at[pl.ds(d_ref[0, 0, i], 1), :], ssem.at[s])
    wait_gather = lambda s: pltpu.make_async_copy(h_hbm.at[pl.ds(0, tm), :], xbuf.at[s], gsem.at[s]).wait()
    wait_scatter = lambda s: pltpu.make_async_copy(ybuf.at[s], out_hbm.at[pl.ds(0, tm), :], ssem.at[s]).wait()

    def looped(copy, idx_ref, s):
        def body(i, carry):
            copy(idx_ref, s, i).start()
            return carry
        lax.fori_loop(0, tm, body, 0, unroll=8)

    def compute():
        x = xb[...]
        act = jax.nn.silu(jnp.dot(x, wg_ref[0], preferred_element_type=F32)) * jnp.dot(x, wu_ref[0], preferred_element_type=F32)
        return jnp.dot(act.astype(BF16), wd_ref[0], preferred_element_type=F32)

    @pl.when((f == 0) & (t == 0))
    def _():
        looped(gather_row, tok0_ref, 0)
        ybuf[1] = jnp.zeros(ybuf.shape[1:], F32)
        for bank in range(2):
            fill = pltpu.make_async_copy(ybuf.at[1], out_hbm.at[pl.ds(out_hbm.shape[0] - (bank + 1) * tm, tm), :],
                                         ssem.at[1])
            fill.start()
            fill.wait()

    @pl.when((f == 0) & ((t == 0) | prev_valid))
    def _():
        wait_gather(slot)

    for parity in range(2):
        @pl.when((f == 0) & valid & (slot == parity))
        def _(parity=parity):
            xb[...] = xbuf[parity].astype(BF16)
            for i in range(tm):
                gather_row(tokn_ref, 1 - parity, i).start()
            acc_ref[...] = compute()

    @pl.when((f == n_f - 1) & ((t == 1) | ((t >= 2) & (tv_ref[jnp.maximum(t - 2, 0)] == 1))))
    def _():
        wait_scatter(slot)

    for parity in range(2):
        @pl.when((f == n_f - 1) & valid & (slot == parity))
        def _(parity=parity):
            for i in range(tm):
                scatter_row(dstp_ref, 1 - parity, i).start()
            ybuf[parity] = acc_ref[...] + compute()

    @pl.when((f == n_f - 1) & jnp.logical_not(valid) & (t >= 1) & prev_valid)
    def _():
        looped(scatter_row, dstp_ref, 1 - slot)

    @pl.when((f == n_f - 1) & last)
    def _():
        @pl.when(valid)
        def _():
            looped(scatter_row, dst_ref, slot)
            wait_scatter(slot)
            wait_gather(1 - slot)

        @pl.when(prev_valid)
        def _():
            wait_scatter(1 - slot)


def _moe_combine_kernel(x_ref, y1_ref, y2_ref, route_ref, gain_ref, o_ref):
    y = route_ref[:, 2:3] * y1_ref[...] + route_ref[:, 3:4] * y2_ref[...]
    o_ref[...] = x_ref[...] + _rms(y, gain_ref[...])


def _moe(x, h, route, w_gate, w_up, w_down, gain):
    t, d = x.shape
    n_e, _, d_ff = w_gate.shape
    tm = MOE_TILE
    assert d_ff == 2 * FFN_TILE
    tok, dst, dst_prev, tile_expert, tile_valid = _moe_plan(route, n_e)
    n_tiles = tok.shape[0]
    idx_spec = lambda shift: pl.BlockSpec(
        (1, 1, tm), lambda i, f, te, tv: (jnp.clip(i + shift, 0, n_tiles - 1), 0, 0), memory_space=pltpu.SMEM)
    grid_spec = pltpu.PrefetchScalarGridSpec(
        num_scalar_prefetch=2,
        grid=(n_tiles, d_ff // FFN_TILE),
        in_specs=[idx_spec(0), idx_spec(1), idx_spec(0), idx_spec(0),
                  pl.BlockSpec(memory_space=pl.ANY),
                  pl.BlockSpec((1, d, FFN_TILE), lambda i, f, te, tv: (te[i], 0, f)),
                  pl.BlockSpec((1, d, FFN_TILE), lambda i, f, te, tv: (te[i], 0, f)),
                  pl.BlockSpec((1, FFN_TILE, d), lambda i, f, te, tv: (te[i], f, 0))],
        out_specs=pl.BlockSpec(memory_space=pl.ANY),
        scratch_shapes=[pltpu.VMEM((2, tm, d), F32), pltpu.VMEM((tm, d), BF16), pltpu.VMEM((2, tm, d), F32),
                        pltpu.VMEM((tm, d), F32), pltpu.SemaphoreType.DMA((2,)), pltpu.SemaphoreType.DMA((2,))],
    )
    routed = pl.pallas_call(
        _moe_ffn_kernel,
        name="moe_experts",
        grid_spec=grid_spec,
        out_shape=jax.ShapeDtypeStruct((2 * t + 2 * tm, d), F32),
        compiler_params=_params("arbitrary", "arbitrary"),
    )(tile_expert, tile_valid, tok, tok, dst_prev, dst, h, w_gate.astype(BF16), w_up.astype(BF16), w_down.astype(BF16))
    row = lambda n: pl.BlockSpec((ROW_TILE, n), lambda i: (i, 0))
    second = pl.BlockSpec((ROW_TILE, d), lambda i: (i + t // ROW_TILE, 0))
    return pl.pallas_call(
        _moe_combine_kernel,
        name="moe_combine",
        grid=(t // ROW_TILE,),
        in_specs=[row(d), row(d), second, row(LANES), _const_spec((1, d))],
        out_specs=row(d),
        out_shape=jax.ShapeDtypeStruct((t, d), F32),
        compiler_params=_params("parallel"),
    )(x, routed, routed, route, gain.astype(F32).reshape(1, d))


def kernel(x, norm_mix_pre, norm_mix_post, norm_ffn_pre, norm_ffn_post, w_in, w_out, rwkv_mu, rwkv_w0, rwkv_w_up, rwkv_a0, rwkv_a_up, rwkv_g_up, rwkv_k_k, rwkv_k_a, rwkv_r_k, rwkv_ln_w, rwkv_ln_b, s5_a_re, s5_a_im, s5_log_step, s5_b_re, s5_b_im, s5_c_re, s5_c_im, s5_d, s5_w_glu, s5_b_glu, s5_norm, attn_rel_bias, attn_norm, ffn_w_gate, ffn_w_up, ffn_w_down, moe_router, moe_w_gate, moe_w_up, moe_w_down):
    b, s, d = x.shape
    depth = w_in.shape[0]
    xt = x.reshape(b * s, d)
    for layer in range(depth):
        p, u, qkv = _in_projection(xt, norm_mix_pre[layer], w_in[layer])
        o_rwkv = _rwkv_mix(p.reshape(b, s, -1), rwkv_mu[layer], rwkv_w0[layer], rwkv_w_up[layer], rwkv_a0[layer],
                           rwkv_a_up[layer], rwkv_g_up[layer], rwkv_k_k[layer], rwkv_k_a[layer], rwkv_r_k[layer],
                           rwkv_ln_w[layer], rwkv_ln_b[layer])
        o_s5 = _s5_mix(u.reshape(b, s, -1), s5_a_re[layer], s5_a_im[layer], s5_log_step[layer], s5_b_re[layer],
                       s5_b_im[layer], s5_c_re[layer], s5_c_im[layer], s5_d[layer], s5_w_glu[layer],
                       s5_b_glu[layer], s5_norm[layer])
        o_attn = _chunk_attention(qkv.reshape(b, s, -1), attn_rel_bias[layer], attn_norm[layer])
        i = layer // 2
        moe = layer % 2 == 1
        outs = _out_projection(xt, o_rwkv.reshape(b * s, -1), o_s5.reshape(b * s, -1), o_attn.reshape(b * s, -1),
                               w_out[layer], norm_mix_post[layer], norm_ffn_pre[layer],
                               moe_router[i] if moe else None)
        if moe:
            xt, h, route = outs
            xt = _moe(xt, h, route, moe_w_gate[i], moe_w_up[i], moe_w_down[i], norm_ffn_post[layer])
        else:
            xt, h = outs
            xt = _ffn(xt, h, ffn_w_gate[i], ffn_w_up[i], ffn_w_down[i], norm_ffn_post[layer])
    return xt.reshape(b, s, d)
```
